```python
import math
import jax
import jax.numpy as jnp
from jax import lax
import numpy as np

D_MODEL = 1024
BATCH = 2
SEQ = 8192
DEPTH = 4

HEAD_DIM = 64
GROUP_HEADS = 4
GROUP_WIDTH = GROUP_HEADS * HEAD_DIM
N_GROUPS = 4
MIX_WIDTH = N_GROUPS * GROUP_WIDTH
Q_BLOCK = 128

FORGET_BIAS_INIT = 2.0
DIFF_QK_DIM = HEAD_DIM // 2
DIFF_SUBLN_EPS = 1e-5
MOBA_BLOCK = 256
MOBA_TOPK = 3
MLA_Q_LORA = 256
MLA_KV_LORA = 128
MLA_NOPE = 64
MLA_ROPE = 32
MLA_V = HEAD_DIM
ROPE_THETA = 10000.0
REL_BUCKETS = 32
REL_MAX_DIST = 128
N_REL_HEADS = 2 * GROUP_HEADS
D_FF = 2752
CONV_WIDTH = 3
DEEPNORM_ALPHA = (2 * DEPTH) ** 0.25
DEEPNORM_BETA = (8 * DEPTH) ** -0.25
LN_EPS = 1e-5
RMS_EPS = 1e-6

FOX_COLS = 3 * GROUP_WIDTH + GROUP_HEADS
DIFF_COLS = 3 * GROUP_WIDTH
MOBA_COLS = 3 * GROUP_WIDTH
MLA_COLS = MLA_Q_LORA + MLA_KV_LORA + MLA_ROPE
N_IN = FOX_COLS + DIFF_COLS + MOBA_COLS + MLA_COLS
IN_SPLITS = (FOX_COLS, FOX_COLS + DIFF_COLS, FOX_COLS + DIFF_COLS + MOBA_COLS)

kernel_name = 'hybrid_fox_diff_moba_mla_convffn_deepnorm'


def _layernorm(t, g, b):
    t32 = t.astype(jnp.float32)
    mu = jnp.mean(t32, axis=-1, keepdims=True)
    var = jnp.mean(jnp.square(t32 - mu), axis=-1, keepdims=True)
    return ((t32 - mu) * lax.rsqrt(var + LN_EPS) * g + b).astype(t.dtype)


def _rmsnorm(t, g, eps=RMS_EPS):
    t32 = t.astype(jnp.float32)
    return (t32 * lax.rsqrt(jnp.mean(jnp.square(t32), axis=-1, keepdims=True) + eps) * g).astype(t.dtype)


def _heads(t, n):
    b, s, w = t.shape
    return t.reshape(b, s, n, w // n).transpose(0, 2, 1, 3)


def _merge(t):
    b, h, s, d = t.shape
    return t.transpose(0, 2, 1, 3).reshape(b, s, h * d)


def _rel_bucket(dist):
    max_exact = REL_BUCKETS // 2
    d_large = jnp.maximum(dist, max_exact).astype(jnp.float32)
    large = max_exact + (jnp.log(d_large / max_exact) / math.log(REL_MAX_DIST / max_exact)
                         * (REL_BUCKETS - max_exact)).astype(jnp.int32)
    large = jnp.minimum(large, REL_BUCKETS - 1)
    return jnp.where(dist < max_exact, dist, large)


def _rope_tables(s):
    inv = ROPE_THETA ** (-jnp.arange(0, MLA_ROPE // 2, dtype=jnp.float32) * 2.0 / MLA_ROPE)
    ang = jnp.arange(s, dtype=jnp.float32)[:, None] * inv[None, :]
    return jnp.cos(ang), jnp.sin(ang)


def _rope(t, cos, sin):
    t1, t2 = jnp.split(t, 2, axis=-1)
    cos = cos.astype(t.dtype)
    sin = sin.astype(t.dtype)
    return jnp.concatenate([t1 * cos - t2 * sin, t2 * cos + t1 * sin], axis=-1)


def _sweep_query_blocks(fn, *q_like):
    n = q_like[0].shape[2] // Q_BLOCK

    def split(t):
        t = t.reshape(t.shape[:2] + (n, Q_BLOCK) + t.shape[3:])
        return jnp.moveaxis(t, 2, 0)

    starts = jnp.arange(n, dtype=jnp.int32) * Q_BLOCK
    out = lax.map(lambda a: fn(a[0], *a[1:]), (starts,) + tuple(split(t) for t in q_like))
    out = jnp.moveaxis(out, 0, 2)
    return out.reshape(out.shape[:2] + (n * Q_BLOCK,) + out.shape[4:])


def fox_attention(q, k, v, log_f):
    s_len = k.shape[2]
    cum = jnp.cumsum(log_f, axis=-1)
    k_pos = jnp.arange(s_len)
    scale = q.shape[-1] ** -0.5

    def block(start, qb, cb):
        q_pos = start + jnp.arange(Q_BLOCK)
        s = jnp.einsum('bhqd,bhkd->bhqk', qb, k).astype(jnp.float32) * scale
        s = s + (cb[..., :, None] - cum[..., None, :])
        s = jnp.where(k_pos[None, :] <= q_pos[:, None], s, -jnp.inf)
        p = jax.nn.softmax(s, axis=-1).astype(v.dtype)
        return jnp.einsum('bhqk,bhkd->bhqd', p, v)

    return _sweep_query_blocks(block, q, cum)


def diff_attention(q, k, v, lam, bias_by_dist):
    s_len = k.shape[2]
    k_pos = jnp.arange(s_len)
    scale = q.shape[-1] ** -0.5

    def block(start, qb):
        q_pos = start + jnp.arange(Q_BLOCK)
        dist = q_pos[:, None] - k_pos[None, :]
        bias = bias_by_dist[:, jnp.clip(dist, 0, s_len - 1)]
        s = jnp.einsum('bhqnd,bhknd->bhnqk', qb, k).astype(jnp.float32) * scale + bias[None, :, None]
        s = jnp.where(dist >= 0, s, -jnp.inf)
        p = jax.nn.softmax(s, axis=-1)
        a = p[:, :, 0] - lam * p[:, :, 1]
        return jnp.einsum('bhqk,bhkd->bhqd', a.astype(v.dtype), v)

    return _sweep_query_blocks(block, q)


def moba_attention(q, k, v, bias_by_dist):
    b, h, s_len, dh = k.shape
    nb = max(-(-s_len // MOBA_BLOCK), MOBA_TOPK)
    pad = nb * MOBA_BLOCK - s_len
    kp = jnp.pad(k, ((0, 0), (0, 0), (0, pad), (0, 0)))
    vp = jnp.pad(v, ((0, 0), (0, 0), (0, pad), (0, 0)))
    kb = kp.reshape(b, h, nb, MOBA_BLOCK, dh)
    vb = vp.reshape(b, h, nb, MOBA_BLOCK, dh)
    k_mean = jnp.mean(kb.astype(jnp.float32), axis=3).astype(k.dtype)
    blk = jnp.arange(nb)
    within = jnp.arange(MOBA_BLOCK)
    b_ix = jnp.arange(b)[:, None, None, None]
    h_ix = jnp.arange(h)[None, :, None, None]
    scale = dh ** -0.5
    n_sel = MOBA_TOPK * MOBA_BLOCK

    def block(start, qb):
        q_pos = start + jnp.arange(Q_BLOCK)
        own = start // MOBA_BLOCK
        gate = jnp.einsum('bhqd,bhnd->bhqn', qb, k_mean).astype(jnp.float32)
        gate = jnp.where(blk < own, gate, -jnp.inf)
        _, sel = lax.top_k(gate, MOBA_TOPK)
        valid = jnp.arange(MOBA_TOPK) < own
        k_sel = kb[b_ix, h_ix, sel]
        v_sel = vb[b_ix, h_ix, sel]
        sel_dist = q_pos[:, None, None] - (sel[..., None] * MOBA_BLOCK + within)
        sel_bias = bias_by_dist[h_ix[..., None], jnp.clip(sel_dist, 0, s_len - 1)]
        s_sel = jnp.einsum('bhqd,bhqnkd->bhqnk', qb, k_sel).astype(jnp.float32) * scale + sel_bias
        s_sel = jnp.where(valid[:, None], s_sel, -jnp.inf).reshape(b, h, Q_BLOCK, n_sel)
        own_start = own * MOBA_BLOCK
        k_own = lax.dynamic_slice_in_dim(kp, own_start, MOBA_BLOCK, axis=2)
        v_own = lax.dynamic_slice_in_dim(vp, own_start, MOBA_BLOCK, axis=2)
        own_dist = q_pos[:, None] - (own_start + within)[None, :]
        own_bias = bias_by_dist[:, jnp.clip(own_dist, 0, s_len - 1)]
        s_own = jnp.einsum('bhqd,bhkd->bhqk', qb, k_own).astype(jnp.float32) * scale + own_bias[None]
        s_own = jnp.where(own_dist >= 0, s_own, -jnp.inf)
        p = jax.nn.softmax(jnp.concatenate([s_sel, s_own], axis=-1), axis=-1).astype(v.dtype)
        p_sel = p[..., :n_sel].reshape(b, h, Q_BLOCK, MOBA_TOPK, MOBA_BLOCK)
        return (jnp.einsum('bhqnk,bhqnkd->bhqd', p_sel, v_sel)
                + jnp.einsum('bhqk,bhkd->bhqd', p[..., n_sel:], v_own))

    return _sweep_query_blocks(block, q)


def causal_attention(q, k, v):
    s_len = k.shape[2]
    k_pos = jnp.arange(s_len)
    scale = q.shape[-1] ** -0.5

    def block(start, qb):
        q_pos = start + jnp.arange(Q_BLOCK)
        s = jnp.einsum('bhqd,bhkd->bhqk', qb, k).astype(jnp.float32) * scale
        s = jnp.where(k_pos[None, :] <= q_pos[:, None], s, -jnp.inf)
        p = jax.nn.softmax(s, axis=-1).astype(v.dtype)
        return jnp.einsum('bhqk,bhkd->bhqd', p, v)

    return _sweep_query_blocks(block, q)


def setup_inputs(seed: int = 0) -> dict:
    key = jax.random.key(seed)
    ks = jax.random.split(key, 20)
    f32 = jnp.float32
    L = DEPTH

    def nrm(k, shape, scale):
        return jax.random.normal(k, shape, f32) * scale

    return {
        'x': nrm(ks[0], (BATCH, SEQ, D_MODEL), 1.0),
        'w_in': nrm(ks[1], (L, D_MODEL, N_IN), D_MODEL ** -0.5),
        'b_forget': FORGET_BIAS_INIT + nrm(ks[2], (L, GROUP_HEADS), 0.1),
        'diff_lambda': nrm(ks[3], (L, 4, DIFF_QK_DIM), 0.1),
        'diff_subln': 1.0 + nrm(ks[4], (L, HEAD_DIM), 0.02),
        'mla_q_norm': 1.0 + nrm(ks[5], (L, MLA_Q_LORA), 0.02),
        'mla_kv_norm': 1.0 + nrm(ks[6], (L, MLA_KV_LORA), 0.02),
        'mla_w_uq': nrm(ks[7], (L, MLA_Q_LORA, GROUP_HEADS * (MLA_NOPE + MLA_ROPE)), MLA_Q_LORA ** -0.5),
        'mla_w_ukv': nrm(ks[8], (L, MLA_KV_LORA, GROUP_HEADS * (MLA_NOPE + MLA_V)), MLA_KV_LORA ** -0.5),
        'rel_bias': nrm(ks[9], (REL_BUCKETS, N_REL_HEADS), 0.2),
        'w_o': nrm(ks[10], (L, MIX_WIDTH, D_MODEL), MIX_WIDTH ** -0.5 * DEEPNORM_BETA),
        'ln1_g': 1.0 + nrm(ks[11], (L, D_MODEL), 0.02),
        'ln1_b': nrm(ks[12], (L, D_MODEL), 0.02),
        'w_up': nrm(ks[13], (L, D_MODEL, 2 * D_FF), D_MODEL ** -0.5),
        'conv_w': nrm(ks[14], (L, CONV_WIDTH, 2 * D_FF), CONV_WIDTH ** -0.5),
        'conv_b': nrm(ks[15], (L, 2 * D_FF), 0.02),
        'w_down': nrm(ks[16], (L, D_FF, D_MODEL), D_FF ** -0.5 * DEEPNORM_BETA),
        'ln2_g': 1.0 + nrm(ks[17], (L, D_MODEL), 0.02),
        'ln2_b': nrm(ks[18], (L, D_MODEL), 0.02),
    }


def reference(x, w_in, b_forget, diff_lambda, diff_subln, mla_q_norm, mla_kv_norm, mla_w_uq,
              mla_w_ukv, rel_bias, w_o, ln1_g, ln1_b, w_up, conv_w, conv_b, w_down, ln2_g, ln2_b):
    b, s_len, _ = x.shape
    G = GROUP_HEADS
    pos = jnp.arange(s_len)
    bias_by_dist = rel_bias[_rel_bucket(pos)].T.astype(jnp.float32)
    bias_diff, bias_moba = bias_by_dist[:G], bias_by_dist[G:]
    cos, sin = _rope_tables(s_len)

    for l in range(DEPTH):
        h = x @ w_in[l]
        fox_h, diff_h, moba_h, mla_h = jnp.split(h, list(IN_SPLITS), axis=-1)

        fq, fk, fv, ff = jnp.split(fox_h, [GROUP_WIDTH, 2 * GROUP_WIDTH, 3 * GROUP_WIDTH], axis=-1)
        log_f = jax.nn.log_sigmoid(ff.astype(jnp.float32) + b_forget[l].astype(jnp.float32)).transpose(0, 2, 1)
        fox_o = fox_attention(_heads(fq, G), _heads(fk, G), _heads(fv, G), log_f)

        lam_init = 0.8 - 0.6 * math.exp(-0.3 * l)
        lq1, lk1, lq2, lk2 = diff_lambda[l].astype(jnp.float32)
        lam = jnp.exp(jnp.sum(lq1 * lk1)) - jnp.exp(jnp.sum(lq2 * lk2)) + lam_init
        dq, dk, dv = jnp.split(diff_h, [GROUP_WIDTH, 2 * GROUP_WIDTH], axis=-1)
        dq = _heads(dq, G).reshape(b, G, s_len, 2, DIFF_QK_DIM)
        dk = _heads(dk, G).reshape(b, G, s_len, 2, DIFF_QK_DIM)
        diff_o = diff_attention(dq, dk, _heads(dv, G), lam, bias_diff)
        diff_o = _rmsnorm(diff_o, diff_subln[l], DIFF_SUBLN_EPS) * (1.0 - lam_init)

        mq, mk, mv = jnp.split(moba_h, [GROUP_WIDTH, 2 * GROUP_WIDTH], axis=-1)
        moba_o = moba_attention(_heads(mq, G), _heads(mk, G), _heads(mv, G), bias_moba)

        cq, ckv, kr = jnp.split(mla_h, [MLA_Q_LORA, MLA_Q_LORA + MLA_KV_LORA], axis=-1)
        lq = _heads(_rmsnorm(cq, mla_q_norm[l]) @ mla_w_uq[l], G)
        lkv = _heads(_rmsnorm(ckv, mla_kv_norm[l]) @ mla_w_ukv[l], G)
        q_nope, q_rope = jnp.split(lq, [MLA_NOPE], axis=-1)
        k_nope, lv = jnp.split(lkv, [MLA_NOPE], axis=-1)
        k_rope = jnp.broadcast_to(_rope(kr[:, None], cos, sin), q_rope.shape)
        mla_o = causal_attention(jnp.concatenate([q_nope, _rope(q_rope, cos, sin)], axis=-1),
                                 jnp.concatenate([k_nope, k_rope], axis=-1), lv)

        mix = jnp.concatenate([_merge(fox_o), _merge(diff_o), _merge(moba_o), _merge(mla_o)], axis=-1) @ w_o[l]
        x = _layernorm(DEEPNORM_ALPHA * x + mix, ln1_g[l], ln1_b[l])

        u = x @ w_up[l]
        u_pad = jnp.pad(u, ((0, 0), (CONV_WIDTH - 1, 0), (0, 0)))
        cw = conv_w[l]
        u = sum(cw[j] * u_pad[:, j:j + s_len] for j in range(CONV_WIDTH)) + conv_b[l]
        gate, val = jnp.split(u, 2, axis=-1)
        y = (jax.nn.silu(gate) * val) @ w_down[l]
        x = _layernorm(DEEPNORM_ALPHA * x + y, ln2_g[l], ln2_b[l])

    return x
```

```python
import functools
import math

import jax
import jax.numpy as jnp
from jax import lax
from jax.experimental import pallas as pl
from jax.experimental.pallas import tpu as pltpu

F32 = jnp.float32
BF16 = jnp.bfloat16

HEAD_DIM = 64
GROUP_HEADS = 4
GROUP_WIDTH = GROUP_HEADS * HEAD_DIM
LANES = 128
DIFF_QK_DIM = HEAD_DIM // 2
DIFF_SUBLN_EPS = 1e-5
MOBA_BLOCK = 256
MOBA_TOPK = 3
MLA_Q_LORA = 256
MLA_KV_LORA = 128
MLA_NOPE = 64
MLA_ROPE = 32
ROPE_THETA = 10000.0
REL_BUCKETS = 32
REL_MAX_DIST = 128
D_FF = 2752
FF_CHUNK = 256
D_FF_PAD = -(-D_FF // FF_CHUNK) * FF_CHUNK
CONV_WIDTH = 3
CONV_HALO = 16
LN_EPS = 1e-5
RMS_EPS = 1e-6
NEG = -1e30

BIAS_TILE = 256
DENSE_TILE = 512
VMEM_LIMIT = 56 * 1024 * 1024

_NT = (((1,), (1,)), ((), ()))


def _params(*sem):
    return pltpu.CompilerParams(dimension_semantics=sem, vmem_limit_bytes=VMEM_LIMIT)


def _mm_kernel(x_ref, w_ref, o_ref):
    o_ref[...] = jnp.dot(x_ref[...], w_ref[...], preferred_element_type=F32).astype(o_ref.dtype)


def _matmul(x, w, out_dtype, tm, tn):
    m, k = x.shape
    n = w.shape[1]
    return pl.pallas_call(
        _mm_kernel,
        grid=(m // tm, n // tn),
        in_specs=[pl.BlockSpec((tm, k), lambda i, j: (i, 0)),
                  pl.BlockSpec((k, tn), lambda i, j: (0, j))],
        out_specs=pl.BlockSpec((tm, tn), lambda i, j: (i, j)),
        out_shape=jax.ShapeDtypeStruct((m, n), out_dtype),
        compiler_params=_params("arbitrary", "arbitrary"),
        name="in_proj",
    )(x, w)


def _gate_kernel(w_ref, x_ref, o_ref):
    o_ref[0] = lax.dot_general(w_ref[...], x_ref[...], _NT, preferred_element_type=F32)


def _gate_logits(w_gt, xb, batch, seq, ts=1024):
    ns = seq // ts
    return pl.pallas_call(
        _gate_kernel,
        grid=(batch, ns),
        in_specs=[pl.BlockSpec(w_gt.shape, lambda b, i: (0, 0)),
                  pl.BlockSpec((ts, xb.shape[1]), lambda b, i: (b * ns + i, 0))],
        out_specs=pl.BlockSpec((1, 8, ts), lambda b, i: (b, 0, i)),
        out_shape=jax.ShapeDtypeStruct((batch, 8, seq), F32),
        compiler_params=_params("arbitrary", "arbitrary"),
        name="fox_gate",
    )(w_gt, xb)


def _cum_kernel(g_ref, b_ref, o_ref):
    x = g_ref[0] + b_ref[...]
    x = jnp.minimum(x, 0.0) - jnp.log(1.0 + jnp.exp(-jnp.abs(x)))
    n = x.shape[1]
    lane = lax.broadcasted_iota(jnp.int32, x.shape, 1)
    k = 1
    while k < n:
        x = x + jnp.where(lane >= k, pltpu.roll(x, k, 1), 0.0)
        k *= 2
    o_ref[0] = x


def _fox_cumlog(logits, b_col):
    batch, _, seq = logits.shape
    return pl.pallas_call(
        _cum_kernel,
        grid=(batch,),
        in_specs=[pl.BlockSpec((1, 8, seq), lambda b: (b, 0, 0)),
                  pl.BlockSpec((8, 1), lambda b: (0, 0))],
        out_specs=pl.BlockSpec((1, 8, seq), lambda b: (b, 0, 0)),
        out_shape=jax.ShapeDtypeStruct(logits.shape, F32),
        compiler_params=_params("arbitrary"),
        name="fox_cumlog",
    )(logits, b_col)


def _bias_kernel(rb_ref, d_ref, p_ref, *, t):
    h = pl.program_id(0)
    i = lax.broadcasted_iota(jnp.int32, (t, t), 0)
    j = lax.broadcasted_iota(jnp.int32, (t, t), 1)
    max_exact = REL_BUCKETS // 2
    far = rb_ref[(REL_BUCKETS - 1) * 8 + h]
    for out_ref, off in ((d_ref, 0), (p_ref, t)):
        dist = i - j + off
        d_large = jnp.maximum(dist, max_exact).astype(F32)
        large = max_exact + (jnp.log(d_large / max_exact) / math.log(REL_MAX_DIST / max_exact)
                             * (REL_BUCKETS - max_exact)).astype(jnp.int32)
        large = jnp.minimum(large, REL_BUCKETS - 1)
        bucket = jnp.where(dist < max_exact, dist, large)
        acc = jnp.zeros((t, t), F32)
        for bkt in range(REL_BUCKETS - 1):
            acc = jnp.where(bucket == bkt, rb_ref[bkt * 8 + h] - far, acc)
        if off == 0:
            acc = jnp.where(dist >= 0, acc, NEG)
        out_ref[0] = acc


def _bias_tiles(rel_bias, t):
    n_heads = rel_bias.shape[1]
    shape = jax.ShapeDtypeStruct((n_heads, t, t), F32)
    spec = pl.BlockSpec((1, t, t), lambda h: (h, 0, 0))
    return pl.pallas_call(
        functools.partial(_bias_kernel, t=t),
        grid=(n_heads,),
        in_specs=[pl.BlockSpec(memory_space=pltpu.SMEM)],
        out_specs=(spec, spec),
        out_shape=(shape, shape),
        compiler_params=_params("arbitrary"),
        name="rel_bias_tiles",
    )(rel_bias.reshape(-1))


def _online_update(s, v, carry):
    m, l, acc = carry
    m_new = jnp.maximum(m, jnp.max(s, axis=-1, keepdims=True))
    p = jnp.exp(s - m_new)
    alpha = jnp.exp(m - m_new)
    l = alpha * l + jnp.sum(p, axis=-1, keepdims=True)
    acc = alpha * acc + jnp.dot(p.astype(BF16), v, preferred_element_type=F32)
    return m_new, l, acc


def _init_carry(t):
    return (jnp.full((t, 1), NEG, F32), jnp.zeros((t, 1), F32), jnp.zeros((t, LANES), F32))


def _lane_select(x, lo, hi):
    lane = lax.broadcasted_iota(jnp.int32, x.shape, 1)
    return jnp.where((lane >= lo) & (lane < hi), x, jnp.zeros_like(x))


def _merge_pair(a, b):
    lane = lax.broadcasted_iota(jnp.int32, a.shape, 1)
    return jnp.where(lane < HEAD_DIM, a, b)


def _causal(t):
    return (lax.broadcasted_iota(jnp.int32, (t, t), 0) >= lax.broadcasted_iota(jnp.int32, (t, t), 1))


def _fox_kernel(q_ref, k_ref, v_ref, c_ref, o_ref, *, t):
    qi = pl.program_id(1)
    causal = _causal(t)
    pairs = []
    for p in range(GROUP_HEADS // 2):
        sl = slice(p * LANES, (p + 1) * LANES)
        q_pair = q_ref[:, sl]
        halves = []
        for e in range(2):
            h = 2 * p + e
            qm = _lane_select(q_pair, e * HEAD_DIM, (e + 1) * HEAD_DIM)

            def scores(j, qm=qm, h=h, sl=sl):
                start = pl.multiple_of(j * t, t)
                k = k_ref[pl.ds(start, t), sl]
                s = lax.dot_general(qm, k, _NT, preferred_element_type=F32)
                return s - c_ref[0, h:h + 1, pl.ds(start, t)], v_ref[pl.ds(start, t), sl]

            def body(j, carry, scores=scores):
                s, v = scores(j)
                return _online_update(s, v, carry)

            carry = lax.fori_loop(0, qi, body, _init_carry(t))
            s, v = scores(qi)
            _, l, acc = _online_update(jnp.where(causal, s, NEG), v, carry)
            halves.append(acc / l)
        pairs.append(_merge_pair(*halves))
    o_ref[...] = jnp.concatenate(pairs, axis=1).astype(o_ref.dtype)


def _fox_attention(h_bf, cum, batch, seq, col0, t=DENSE_TILE):
    nq = seq // t
    w = GROUP_WIDTH
    cb = col0 // w
    return pl.pallas_call(
        functools.partial(_fox_kernel, t=t),
        grid=(batch, nq),
        in_specs=[pl.BlockSpec((t, w), lambda b, i: (b * nq + i, cb)),
                  pl.BlockSpec((seq, w), lambda b, i: (b, cb + 1)),
                  pl.BlockSpec((seq, w), lambda b, i: (b, cb + 2)),
                  pl.BlockSpec((1, 8, seq), lambda b, i: (b, 0, 0))],
        out_specs=pl.BlockSpec((t, w), lambda b, i: (b * nq + i, 0)),
        out_shape=jax.ShapeDtypeStruct((batch * seq, w), BF16),
        compiler_params=_params("arbitrary", "arbitrary"),
        name="fox_attn",
    )(h_bf, h_bf, h_bf, cum)


def _diff_kernel(q_ref, k_ref, v_ref, bd_ref, bp_ref, lam_ref, lc_ref, g_ref, o_ref, *, t):
    qi = pl.program_id(1)
    lam_p = lam_ref[...]
    lam_init = lc_ref[0:1, 0:1]
    lam = (jnp.exp(jnp.sum(lam_p[0:1] * lam_p[1:2], axis=-1, keepdims=True))
           - jnp.exp(jnp.sum(lam_p[2:3] * lam_p[3:4], axis=-1, keepdims=True)) + lam_init)
    pairs = []
    for p in range(GROUP_HEADS // 2):
        sl = slice(p * LANES, (p + 1) * LANES)
        q_pair = q_ref[:, sl]
        halves = []
        for e in range(2):
            h = 2 * p + e
            lo = e * HEAD_DIM
            qms = (_lane_select(q_pair, lo, lo + DIFF_QK_DIM),
                   _lane_select(q_pair, lo + DIFF_QK_DIM, lo + HEAD_DIM))

            def step(j, carry, bias, qms=qms, sl=sl):
                start = pl.multiple_of(j * t, t)
                k = k_ref[pl.ds(start, t), sl]
                v = v_ref[pl.ds(start, t), sl]
                out = []
                for qm, c in zip(qms, carry):
                    s = lax.dot_general(qm, k, _NT, preferred_element_type=F32)
                    if bias is not None:
                        s = s + bias
                    out.append(_online_update(s, v, c))
                return tuple(out)

            carry = (_init_carry(t), _init_carry(t))
            carry = lax.fori_loop(0, jnp.maximum(qi - 1, 0),
                                  lambda j, c, step=step: step(j, c, None), carry)
            carry = lax.cond(qi >= 1,
                             lambda c, step=step, h=h: step(qi - 1, c, bp_ref[h]),
                             lambda c: c, carry)
            (_, l1, a1), (_, l2, a2) = step(qi, carry, bd_ref[h])
            o = a1 / l1 - lam * (a2 / l2)
            o_h = _lane_select(o, lo, lo + HEAD_DIM)
            ms = jnp.sum(o_h * o_h, axis=-1, keepdims=True) * (1.0 / HEAD_DIM)
            halves.append(o_h * lax.rsqrt(ms + DIFF_SUBLN_EPS))
        pairs.append(_merge_pair(*halves))
    o = jnp.concatenate(pairs, axis=1) * g_ref[...] * lc_ref[1:2, 0:1]
    o_ref[...] = o.astype(o_ref.dtype)


def _diff_attention(h_bf, bias_d, bias_p, lam_p, lam_c, gain, batch, seq, col0, t=BIAS_TILE):
    nq = seq // t
    w = GROUP_WIDTH
    cb = col0 // w
    bias_spec = pl.BlockSpec((GROUP_HEADS, t, t), lambda b, i: (0, 0, 0))
    return pl.pallas_call(
        functools.partial(_diff_kernel, t=t),
        grid=(batch, nq),
        in_specs=[pl.BlockSpec((t, w), lambda b, i: (b * nq + i, cb)),
                  pl.BlockSpec((seq, w), lambda b, i: (b, cb + 1)),
                  pl.BlockSpec((seq, w), lambda b, i: (b, cb + 2)),
                  bias_spec, bias_spec,
                  pl.BlockSpec(lam_p.shape, lambda b, i: (0, 0)),
                  pl.BlockSpec(lam_c.shape, lambda b, i: (0, 0)),
                  pl.BlockSpec(gain.shape, lambda b, i: (0, 0))],
        out_specs=pl.BlockSpec((t, w), lambda b, i: (b * nq + i, 0)),
        out_shape=jax.ShapeDtypeStruct((batch * seq, w), BF16),
        compiler_params=_params("arbitrary", "arbitrary"),
        name="diff_attn",
    )(h_bf, h_bf, h_bf, bias_d, bias_p, lam_p, lam_c, gain)


def _moba_kernel(q_ref, k_ref, v_ref, bd_ref, bp_ref, o_ref, kmean_ref, *, t, n_blocks):
    qi = pl.program_id(1)

    @pl.when(qi == 0)
    def _():
        kmean_ref[...] = jnp.zeros_like(kmean_ref)
        for blk in range(n_blocks):
            kb = k_ref[blk * t:(blk + 1) * t, :].astype(F32)
            kmean_ref[blk:blk + 1, :] = jnp.sum(kb, axis=0, keepdims=True) * (1.0 / t)

    blk_i = lax.broadcasted_iota(jnp.int32, (t, LANES), 1)
    blk_f = blk_i.astype(F32)
    pairs = []
    for p in range(GROUP_HEADS // 2):
        sl = slice(p * LANES, (p + 1) * LANES)
        q_pair = q_ref[:, sl]
        halves = []
        for e in range(2):
            h = 2 * p + e
            qm = _lane_select(q_pair, e * HEAD_DIM, (e + 1) * HEAD_DIM)
            gate = lax.dot_general(qm.astype(F32), kmean_ref[:, sl], _NT,
                                   precision=lax.Precision.HIGHEST, preferred_element_type=F32)
            g = jnp.where(blk_i < qi, gate, NEG)
            keep = jnp.zeros((t, LANES), F32)
            for _ in range(MOBA_TOPK):
                mx = jnp.max(g, axis=-1, keepdims=True)
                first = jnp.min(jnp.where(g == mx, blk_f, 1e9), axis=-1, keepdims=True)
                pick = jnp.where(mx > 0.5 * NEG, jnp.where(blk_f == first, 1.0, 0.0), 0.0)
                keep = jnp.maximum(keep, pick)
                g = jnp.where(pick > 0.0, NEG, g)
            q_cat = jnp.concatenate([qm, (1.0 - keep).astype(BF16)], axis=1)

            def step(j, carry, bias, q_cat=q_cat, sl=sl):
                start = pl.multiple_of(j * t, t)
                k = k_ref[pl.ds(start, t), sl]
                reject = jnp.where(blk_i == j, NEG, 0.0).astype(BF16)
                s = lax.dot_general(q_cat, jnp.concatenate([k, reject], axis=1), _NT,
                                    preferred_element_type=F32)
                if bias is not None:
                    s = s + bias
                return _online_update(s, v_ref[pl.ds(start, t), sl], carry)

            carry = lax.fori_loop(0, jnp.maximum(qi - 1, 0),
                                  lambda j, c, step=step: step(j, c, None), _init_carry(t))
            carry = lax.cond(qi >= 1,
                             lambda c, step=step, h=h: step(qi - 1, c, bp_ref[h]),
                             lambda c: c, carry)
            start = pl.multiple_of(qi * t, t)
            s = lax.dot_general(qm, k_ref[pl.ds(start, t), sl], _NT, preferred_element_type=F32)
            _, l, acc = _online_update(s + bd_ref[h], v_ref[pl.ds(start, t), sl], carry)
            halves.append(acc / l)
        pairs.append(_merge_pair(*halves))
    o_ref[...] = jnp.concatenate(pairs, axis=1).astype(o_ref.dtype)


def _moba_attention(h_bf, bias_d, bias_p, batch, seq, col0, t=MOBA_BLOCK):
    nq = seq // t
    w = GROUP_WIDTH
    cb = col0 // w
    bias_spec = pl.BlockSpec((GROUP_HEADS, t, t), lambda b, i: (1, 0, 0))
    return pl.pallas_call(
        functools.partial(_moba_kernel, t=t, n_blocks=nq),
        grid=(batch, nq),
        in_specs=[pl.BlockSpec((t, w), lambda b, i: (b * nq + i, cb)),
                  pl.BlockSpec((seq, w), lambda b, i: (b, cb + 1)),
                  pl.BlockSpec((seq, w), lambda b, i: (b, cb + 2)),
                  bias_spec, bias_spec],
        out_specs=pl.BlockSpec((t, w), lambda b, i: (b * nq + i, 0)),
        out_shape=jax.ShapeDtypeStruct((batch * seq, w), BF16),
        scratch_shapes=[pltpu.VMEM((LANES, w), F32)],
        compiler_params=_params("arbitrary", "arbitrary"),
        name="moba_attn",
    )(h_bf, h_bf, h_bf, bias_d, bias_p)


def _mla_prep_kernel(c_ref, gq_ref, gkv_ref, wqa_ref, wqb_ref, wk_ref, wv_ref, e_ref,
                     tq_ref, tk_ref, q_ref, k_ref, v_ref):
    c = c_ref[...]
    cq = c[:, :MLA_Q_LORA]
    ckv = c[:, MLA_Q_LORA:MLA_Q_LORA + MLA_KV_LORA]
    kr = c[:, MLA_Q_LORA + MLA_KV_LORA:]
    cq = cq * lax.rsqrt(jnp.mean(cq * cq, axis=-1, keepdims=True) + RMS_EPS) * gq_ref[...]
    ckv = ckv * lax.rsqrt(jnp.mean(ckv * ckv, axis=-1, keepdims=True) + RMS_EPS) * gkv_ref[...]
    cq = cq.astype(BF16)
    ckv = ckv.astype(BF16)
    tq = tq_ref[...]
    cos_q = jnp.concatenate([tq[:, :LANES]] * GROUP_HEADS, axis=1)
    sin_q = jnp.concatenate([tq[:, LANES:]] * GROUP_HEADS, axis=1)
    q = (jnp.dot(cq, wqa_ref[...], preferred_element_type=F32) * cos_q
         + jnp.dot(cq, wqb_ref[...], preferred_element_type=F32) * sin_q)
    q_ref[...] = (q * (MLA_NOPE + MLA_ROPE) ** -0.5).astype(q_ref.dtype)
    k_rope = (kr * tk_ref[...]).astype(BF16)
    k = (jnp.dot(ckv, wk_ref[...], preferred_element_type=F32)
         + jnp.dot(k_rope, e_ref[...], preferred_element_type=F32))
    k_ref[...] = k.astype(k_ref.dtype)
    v_ref[...] = jnp.dot(ckv, wv_ref[...], preferred_element_type=F32).astype(v_ref.dtype)


def _mla_prep(c_lat, gq, gkv, wqa, wqb, wk, wv, e_mat, tab_q, tab_k, seq, tm=512):
    m = c_lat.shape[0]
    ns = seq // tm
    full = lambda a: pl.BlockSpec(a.shape, lambda i: (0, 0))
    slot_w = GROUP_HEADS * LANES
    return pl.pallas_call(
        _mla_prep_kernel,
        grid=(m // tm,),
        in_specs=[pl.BlockSpec((tm, c_lat.shape[1]), lambda i: (i, 0)),
                  full(gq), full(gkv), full(wqa), full(wqb), full(wk), full(wv), full(e_mat),
                  pl.BlockSpec((tm, tab_q.shape[1]), lambda i: (i % ns, 0)),
                  pl.BlockSpec((tm, tab_k.shape[1]), lambda i: (i % ns, 0))],
        out_specs=(pl.BlockSpec((tm, slot_w), lambda i: (i, 0)),
                   pl.BlockSpec((tm, slot_w), lambda i: (i, 0)),
                   pl.BlockSpec((tm, GROUP_WIDTH), lambda i: (i, 0))),
        out_shape=(jax.ShapeDtypeStruct((m, slot_w), BF16),
                   jax.ShapeDtypeStruct((m, slot_w), BF16),
                   jax.ShapeDtypeStruct((m, GROUP_WIDTH), BF16)),
        compiler_params=_params("arbitrary"),
        name="mla_prep",
    )(c_lat, gq, gkv, wqa, wqb, wk, wv, e_mat, tab_q, tab_k)


def _mla_kernel(q_ref, k_ref, v_ref, o_ref, *, t):
    qi = pl.program_id(1)
    causal = _causal(t)
    pairs = []
    for p in range(GROUP_HEADS // 2):
        vsl = slice(p * LANES, (p + 1) * LANES)
        halves = []
        for e in range(2):
            h = 2 * p + e
            hsl = slice(h * LANES, (h + 1) * LANES)
            q = q_ref[:, hsl]

            def scores(j, q=q, hsl=hsl, vsl=vsl):
                start = pl.multiple_of(j * t, t)
                s = lax.dot_general(q, k_ref[pl.ds(start, t), hsl], _NT, preferred_element_type=F32)
                return s, v_ref[pl.ds(start, t), vsl]

            def body(j, carry, scores=scores):
                s, v = scores(j)
                return _online_update(s, v, carry)

            carry = lax.fori_loop(0, qi, body, _init_carry(t))
            s, v = scores(qi)
            _, l, acc = _online_update(jnp.where(causal, s, NEG), v, carry)
            halves.append(acc / l)
        pairs.append(_merge_pair(*halves))
    o_ref[...] = jnp.concatenate(pairs, axis=1).astype(o_ref.dtype)


def _mla_attention(q, k, v, batch, seq, t=DENSE_TILE):
    nq = seq // t
    slot_w = q.shape[1]
    return pl.pallas_call(
        functools.partial(_mla_kernel, t=t),
        grid=(batch, nq),
        in_specs=[pl.BlockSpec((t, slot_w), lambda b, i: (b * nq + i, 0)),
                  pl.BlockSpec((seq, slot_w), lambda b, i: (b, 0)),
                  pl.BlockSpec((seq, GROUP_WIDTH), lambda b, i: (b, 0))],
        out_specs=pl.BlockSpec((t, GROUP_WIDTH), lambda b, i: (b * nq + i, 0)),
        out_shape=jax.ShapeDtypeStruct((batch * seq, GROUP_WIDTH), BF16),
        compiler_params=_params("arbitrary", "arbitrary"),
        name="mla_attn",
    )(q, k, v)


def _layernorm(t, g, b):
    mu = jnp.mean(t, axis=-1, keepdims=True)
    d = t - mu
    var = jnp.mean(d * d, axis=-1, keepdims=True)
    return d * lax.rsqrt(var + LN_EPS) * g + b


def _out_kernel(o1_ref, o2_ref, o3_ref, o4_ref, w_ref, x_ref, g_ref, b_ref, of_ref, ob_ref, *, alpha):
    mix = None
    for n, o_ref in enumerate((o1_ref, o2_ref, o3_ref, o4_ref)):
        part = jnp.dot(o_ref[...], w_ref[n * GROUP_WIDTH:(n + 1) * GROUP_WIDTH, :],
                       preferred_element_type=F32)
        mix = part if mix is None else mix + part
    y = _layernorm(alpha * x_ref[...] + mix, g_ref[...], b_ref[...])
    of_ref[...] = y
    ob_ref[...] = y.astype(ob_ref.dtype)


def _out_proj(outs, w_o, x, g, b, alpha, tm=512):
    m, d = x.shape
    o_spec = pl.BlockSpec((tm, GROUP_WIDTH), lambda i: (i, 0))
    row_spec = pl.BlockSpec((tm, d), lambda i: (i, 0))
    vec_spec = pl.BlockSpec((1, d), lambda i: (0, 0))
    return pl.pallas_call(
        functools.partial(_out_kernel, alpha=alpha),
        grid=(m // tm,),
        in_specs=[o_spec, o_spec, o_spec, o_spec,
                  pl.BlockSpec(w_o.shape, lambda i: (0, 0)), row_spec, vec_spec, vec_spec],
        out_specs=(row_spec, row_spec),
        out_shape=(jax.ShapeDtypeStruct((m, d), F32), jax.ShapeDtypeStruct((m, d), BF16)),
        compiler_params=_params("arbitrary"),
        name="out_proj_ln",
    )(*outs, w_o, x, g, b)


def _ffn_kernel(xb_ref, xh_ref, xf_ref, wg_ref, wv_ref, cg_ref, cv_ref, wd_ref, g_ref, b_ref,
                of_ref, ob_ref, acc_ref, *, alpha, tiles_per_seq, n_chunks):
    i = pl.program_id(0)
    c = pl.program_id(1)
    halo = xh_ref[...]
    halo = jnp.where(i % tiles_per_seq == 0, jnp.zeros_like(halo), halo)
    xe = jnp.concatenate([halo, xb_ref[...]], axis=0)

    def branch(w_ref, cp_ref):
        u = jnp.dot(xe, w_ref[...], preferred_element_type=F32)
        cp = cp_ref[...]
        y = cp[2:3] * u + cp[1:2] * pltpu.roll(u, 1, 0) + cp[0:1] * pltpu.roll(u, 2, 0) + cp[3:4]
        return y[CONV_HALO:]

    gate = branch(wg_ref, cg_ref)
    val = branch(wv_ref, cv_ref)
    act = (gate * jax.nn.sigmoid(gate) * val).astype(BF16)
    part = jnp.dot(act, wd_ref[...], preferred_element_type=F32)

    @pl.when(c == 0)
    def _():
        acc_ref[...] = part

    @pl.when(c > 0)
    def _():
        acc_ref[...] += part

    @pl.when(c == n_chunks - 1)
    def _():
        y = _layernorm(alpha * xf_ref[...] + acc_ref[...], g_ref[...], b_ref[...])
        of_ref[...] = y
        ob_ref[...] = y.astype(ob_ref.dtype)


def _ffn(xb, xf, w_up, conv_p, w_down, g, b, alpha, seq, tm=512):
    m, d = xf.shape
    n_chunks = D_FF_PAD // FF_CHUNK
    halo_blocks = tm // CONV_HALO
    row_spec = pl.BlockSpec((tm, d), lambda i, c: (i, 0))
    vec_spec = pl.BlockSpec((1, d), lambda i, c: (0, 0))
    return pl.pallas_call(
        functools.partial(_ffn_kernel, alpha=alpha, tiles_per_seq=seq // tm, n_chunks=n_chunks),
        grid=(m // tm, n_chunks),
        in_specs=[row_spec,
                  pl.BlockSpec((CONV_HALO, d), lambda i, c: (jnp.maximum(i * halo_blocks - 1, 0), 0)),
                  row_spec,
                  pl.BlockSpec((d, FF_CHUNK), lambda i, c: (0, c)),
                  pl.BlockSpec((d, FF_CHUNK), lambda i, c: (0, n_chunks + c)),
                  pl.BlockSpec((8, FF_CHUNK), lambda i, c: (0, c)),
                  pl.BlockSpec((8, FF_CHUNK), lambda i, c: (0, n_chunks + c)),
                  pl.BlockSpec((FF_CHUNK, d), lambda i, c: (c, 0)),
                  vec_spec, vec_spec],
        out_specs=(row_spec, row_spec),
        out_shape=(jax.ShapeDtypeStruct((m, d), F32), jax.ShapeDtypeStruct((m, d), BF16)),
        scratch_shapes=[pltpu.VMEM((tm, d), F32)],
        compiler_params=_params("arbitrary", "arbitrary"),
        name="conv_ffn_ln",
    )(xb, xb, xf, w_up, w_up, conv_p, conv_p, w_down, g, b)


def _swap_halves(w):
    half = w.shape[-1] // 2
    return jnp.concatenate([-w[..., half:], w[..., :half]], axis=-1)


def _prep_weights(w_in, mla_w_uq, mla_w_ukv, w_up, conv_w, conv_b, w_down):
    depth, d, _ = w_in.shape
    gw = GROUP_WIDTH
    fox0 = 0
    diff0 = 3 * gw + GROUP_HEADS
    moba0 = diff0 + 3 * gw
    mla0 = moba0 + 3 * gw
    sm = HEAD_DIM ** -0.5
    sd = DIFF_QK_DIM ** -0.5
    w_qkv = jnp.concatenate([
        w_in[:, :, fox0:fox0 + gw] * sm, w_in[:, :, fox0 + gw:fox0 + 3 * gw],
        w_in[:, :, diff0:diff0 + gw] * sd, w_in[:, :, diff0 + gw:diff0 + 3 * gw],
        w_in[:, :, moba0:moba0 + gw] * sm, w_in[:, :, moba0 + gw:moba0 + 3 * gw]], axis=-1).astype(BF16)
    w_gt = jnp.pad(jnp.swapaxes(w_in[:, :, 3 * gw:3 * gw + GROUP_HEADS], 1, 2),
                   ((0, 0), (0, 8 - GROUP_HEADS), (0, 0))).astype(BF16)
    kr0 = mla0 + MLA_Q_LORA + MLA_KV_LORA
    w_kr = w_in[:, :, kr0:kr0 + MLA_ROPE]
    w_lat = jnp.concatenate([w_in[:, :, mla0:kr0], w_kr, _swap_halves(w_kr),
                             jnp.zeros((depth, d, LANES - 2 * MLA_ROPE), F32)], axis=-1).astype(BF16)

    uq = mla_w_uq.reshape(depth, MLA_Q_LORA, GROUP_HEADS, MLA_NOPE + MLA_ROPE)
    pad = LANES - MLA_NOPE - MLA_ROPE
    wqa = jnp.pad(uq, ((0, 0), (0, 0), (0, 0), (0, pad)))
    wqb = jnp.concatenate([jnp.zeros_like(uq[..., :MLA_NOPE]), _swap_halves(uq[..., MLA_NOPE:]),
                           jnp.zeros(uq.shape[:-1] + (pad,), F32)], axis=-1)
    ukv = mla_w_ukv.reshape(depth, MLA_KV_LORA, GROUP_HEADS, MLA_NOPE + HEAD_DIM)
    wk = jnp.pad(ukv[..., :MLA_NOPE], ((0, 0), (0, 0), (0, 0), (0, LANES - MLA_NOPE)))
    wv = ukv[..., MLA_NOPE:]
    flat = lambda a: a.reshape(depth, a.shape[1], -1).astype(BF16)

    ffp = D_FF_PAD - D_FF
    padc = lambda a: jnp.pad(a, ((0, 0), (0, 0), (0, ffp)))
    w_up_p = jnp.concatenate([padc(w_up[:, :, :D_FF]), padc(w_up[:, :, D_FF:])], axis=-1).astype(BF16)
    conv = jnp.concatenate([conv_w, conv_b[:, None, :],
                            jnp.zeros((depth, 8 - CONV_WIDTH - 1, 2 * D_FF), F32)], axis=1)
    conv_p = jnp.concatenate([padc(conv[:, :, :D_FF]), padc(conv[:, :, D_FF:])], axis=-1)
    w_down_p = jnp.pad(w_down, ((0, 0), (0, ffp), (0, 0))).astype(BF16)
    return w_qkv, w_gt, w_lat, flat(wqa), flat(wqb), flat(wk), flat(wv), w_up_p, conv_p, w_down_p


def _rope_constants(seq):
    inv = ROPE_THETA ** (-jnp.arange(0, MLA_ROPE // 2, dtype=F32) * 2.0 / MLA_ROPE)
    ang = jnp.arange(seq, dtype=F32)[:, None] * inv[None, :]
    cos, sin = jnp.cos(ang), jnp.sin(ang)
    cos2 = jnp.concatenate([cos, cos], axis=1)
    sin2 = jnp.concatenate([sin, sin], axis=1)
    pad = jnp.zeros((seq, LANES - MLA_NOPE - MLA_ROPE), F32)
    tab_q = jnp.concatenate([jnp.ones((seq, MLA_NOPE), F32), cos2, pad,
                             jnp.zeros((seq, MLA_NOPE), F32), sin2, pad], axis=1)
    tab_k = jnp.concatenate([cos2, sin2, jnp.zeros((seq, LANES - 2 * MLA_ROPE), F32)], axis=1)
    r = jnp.arange(LANES)[:, None]
    c = jnp.arange(GROUP_HEADS * LANES)[None, :]
    e_mat = ((r < 2 * MLA_ROPE) & (c % LANES == MLA_NOPE + r % MLA_ROPE)).astype(BF16)
    return tab_q, tab_k, e_mat


def kernel(x, w_in, b_forget, diff_lambda, diff_subln, mla_q_norm, mla_kv_norm, mla_w_uq, mla_w_ukv,
           rel_bias, w_o, ln1_g, ln1_b, w_up, conv_w, conv_b, w_down, ln2_g, ln2_b):
    batch, seq, d = x.shape
    depth = w_in.shape[0]
    alpha = (2 * depth) ** 0.25
    assert seq % DENSE_TILE == 0 and seq % MOBA_BLOCK == 0 and seq // MOBA_BLOCK <= LANES

    (w_qkv, w_gt, w_lat, wqa, wqb, wk, wv, w_up_p, conv_p, w_down_p) = _prep_weights(
        w_in, mla_w_uq, mla_w_ukv, w_up, conv_w, conv_b, w_down)
    w_o_b = w_o.astype(BF16)
    tab_q, tab_k, e_mat = _rope_constants(seq)
    bias_d, bias_p = _bias_tiles(rel_bias, BIAS_TILE)
    b_col = jnp.pad(b_forget, ((0, 0), (0, 8 - GROUP_HEADS)))[:, :, None]
    gain = jnp.tile(diff_subln, (1, GROUP_HEADS))[:, None, :]

    xf = x.reshape(batch * seq, d)
    xb = xf.astype(BF16)
    for l in range(depth):
        h_bf = _matmul(xb, w_qkv[l], BF16, tm=512, tn=3 * GROUP_WIDTH)
        c_lat = _matmul(xb, w_lat[l], F32, tm=512, tn=w_lat.shape[-1])
        cum = _fox_cumlog(_gate_logits(w_gt[l], xb, batch, seq), b_col[l])
        fox_o = _fox_attention(h_bf, cum, batch, seq, 0)
        lam_init = 0.8 - 0.6 * math.exp(-0.3 * l)
        lam_c = jnp.concatenate([jnp.full((1, LANES), lam_init, F32),
                                 jnp.full((1, LANES), 1.0 - lam_init, F32),
                                 jnp.zeros((6, LANES), F32)], axis=0)
        diff_o = _diff_attention(h_bf, bias_d, bias_p, diff_lambda[l], lam_c, gain[l],
                                 batch, seq, 3 * GROUP_WIDTH)
        moba_o = _moba_attention(h_bf, bias_d, bias_p, batch, seq, 6 * GROUP_WIDTH)
        q_m, k_m, v_m = _mla_prep(c_lat, mla_q_norm[l][None], mla_kv_norm[l][None],
                                  wqa[l], wqb[l], wk[l], wv[l], e_mat, tab_q, tab_k, seq)
        mla_o = _mla_attention(q_m, k_m, v_m, batch, seq)
        xf, xb = _out_proj((fox_o, diff_o, moba_o, mla_o), w_o_b[l], xf, ln1_g[l][None], ln1_b[l][None], alpha)
        xf, xb = _ffn(xb, xf, w_up_p[l], conv_p[l], w_down_p[l], ln2_g[l][None], ln2_b[l][None], alpha, seq)
    return xf.reshape(batch, seq, d)
```

```python
import functools
import math

import jax
import jax.numpy as jnp
from jax import lax
from jax.experimental import pallas as pl
from jax.experimental.pallas import tpu as pltpu

F32 = jnp.float32
BF16 = jnp.bfloat16

HEAD_DIM = 64
GROUP_HEADS = 4
GROUP_WIDTH = GROUP_HEADS * HEAD_DIM
LANES = 128
DIFF_QK_DIM = HEAD_DIM // 2
DIFF_SUBLN_EPS = 1e-5
MOBA_BLOCK = 256
MOBA_TOPK = 3
MLA_Q_LORA = 256
MLA_KV_LORA = 128
MLA_NOPE = 64
MLA_ROPE = 32
ROPE_THETA = 10000.0
REL_BUCKETS = 32
REL_MAX_DIST = 128
D_FF = 2752
FF_CHUNK = 256
D_FF_PAD = -(-D_FF // FF_CHUNK) * FF_CHUNK
CONV_WIDTH = 3
CONV_HALO = 16
LN_EPS = 1e-5
RMS_EPS = 1e-6
NEG = -1e30

BIAS_BLOCK = 128
ATTN_TILE = 512
VMEM_LIMIT = 56 * 1024 * 1024

_NT = (((1,), (1,)), ((), ()))


def _params(*sem):
    return pltpu.CompilerParams(dimension_semantics=sem, vmem_limit_bytes=VMEM_LIMIT)


def _mm_kernel(x_ref, w_ref, o_ref):
    o_ref[...] = jnp.dot(x_ref[...], w_ref[...], preferred_element_type=F32).astype(o_ref.dtype)


def _matmul(x, w, out_dtype, tm, tn):
    m, k = x.shape
    n = w.shape[1]
    return pl.pallas_call(
        _mm_kernel,
        grid=(m // tm, n // tn),
        in_specs=[pl.BlockSpec((tm, k), lambda i, j: (i, 0)),
                  pl.BlockSpec((k, tn), lambda i, j: (0, j))],
        out_specs=pl.BlockSpec((tm, tn), lambda i, j: (i, j)),
        out_shape=jax.ShapeDtypeStruct((m, n), out_dtype),
        compiler_params=_params("arbitrary", "arbitrary"),
        name="in_proj",
    )(x, w)


def _mm_nt_kernel(w_ref, x_ref, o_ref):
    o_ref[...] = lax.dot_general(w_ref[...], x_ref[...], _NT, preferred_element_type=F32).astype(o_ref.dtype)


def _matmul_nt(w_t, x, out_dtype, tm):
    n, k = w_t.shape
    m = x.shape[0]
    return pl.pallas_call(
        _mm_nt_kernel,
        grid=(m // tm,),
        in_specs=[pl.BlockSpec((n, k), lambda i: (0, 0)),
                  pl.BlockSpec((tm, k), lambda i: (i, 0))],
        out_specs=pl.BlockSpec((n, tm), lambda i: (0, i)),
        out_shape=jax.ShapeDtypeStruct((n, m), out_dtype),
        compiler_params=_params("arbitrary"),
        name="in_proj_t",
    )(w_t, x)


def _cum_kernel(g_ref, b_ref, o_ref):
    x = g_ref[...] + b_ref[...]
    x = jnp.minimum(x, 0.0) - jnp.log(1.0 + jnp.exp(-jnp.abs(x)))
    n = x.shape[1]
    lane = lax.broadcasted_iota(jnp.int32, x.shape, 1)
    k = 1
    while k < n:
        x = x + jnp.where(lane >= k, pltpu.roll(x, k, 1), 0.0)
        k *= 2
    o_ref[...] = x


def _fox_cumlog(logits_t, b_col, batch, seq):
    return pl.pallas_call(
        _cum_kernel,
        grid=(batch,),
        in_specs=[pl.BlockSpec((8, seq), lambda b: (0, b)),
                  pl.BlockSpec((8, 1), lambda b: (0, 0))],
        out_specs=pl.BlockSpec((8, seq), lambda b: (0, b)),
        out_shape=jax.ShapeDtypeStruct(logits_t.shape, F32),
        compiler_params=_params("arbitrary"),
        name="fox_cumlog",
    )(logits_t, b_col)


def _bias_kernel(rb_ref, d_ref, p_ref, *, t):
    h = pl.program_id(0)
    i = lax.broadcasted_iota(jnp.int32, (t, t), 0)
    j = lax.broadcasted_iota(jnp.int32, (t, t), 1)
    max_exact = REL_BUCKETS // 2
    far = rb_ref[(REL_BUCKETS - 1) * 8 + h]
    for out_ref, off in ((d_ref, 0), (p_ref, t)):
        dist = j - i + off
        d_large = jnp.maximum(dist, max_exact).astype(F32)
        large = max_exact + (jnp.log(d_large / max_exact) / math.log(REL_MAX_DIST / max_exact)
                             * (REL_BUCKETS - max_exact)).astype(jnp.int32)
        large = jnp.minimum(large, REL_BUCKETS - 1)
        bucket = jnp.where(dist < max_exact, dist, large)
        acc = jnp.zeros((t, t), F32)
        for bkt in range(REL_BUCKETS - 1):
            acc = jnp.where(bucket == bkt, rb_ref[bkt * 8 + h] - far, acc)
        if off == 0:
            acc = jnp.where(dist >= 0, acc, NEG)
        out_ref[0] = acc


def _bias_blocks(rel_bias, t):
    n_heads = rel_bias.shape[1]
    shape = jax.ShapeDtypeStruct((n_heads, t, t), F32)
    spec = pl.BlockSpec((1, t, t), lambda h: (h, 0, 0))
    return pl.pallas_call(
        functools.partial(_bias_kernel, t=t),
        grid=(n_heads,),
        in_specs=[pl.BlockSpec(memory_space=pltpu.SMEM)],
        out_specs=(spec, spec),
        out_shape=(shape, shape),
        compiler_params=_params("arbitrary"),
        name="rel_bias_blocks",
    )(rel_bias.reshape(-1))


def _online_update(s_t, v_t, carry):
    m, l, acc = carry
    m_new = jnp.maximum(m, jnp.max(s_t, axis=0, keepdims=True))
    p = jnp.exp(s_t - m_new)
    alpha = jnp.exp(m - m_new)
    l = alpha * l + jnp.sum(p, axis=0, keepdims=True)
    acc = alpha * acc + jnp.dot(v_t, p.astype(BF16), preferred_element_type=F32)
    return m_new, l, acc


def _init_carry(t):
    return (jnp.full((1, t), NEG, F32), jnp.zeros((1, t), F32), jnp.zeros((LANES, t), F32))


def _lane_select(x, lo, hi):
    lane = lax.broadcasted_iota(jnp.int32, x.shape, 1)
    return jnp.where((lane >= lo) & (lane < hi), x, jnp.zeros_like(x))


def _merge_pair(a, b):
    row = lax.broadcasted_iota(jnp.int32, a.shape, 0)
    return jnp.where(row < HEAD_DIM, a, b)


def _causal_t(t):
    return (lax.broadcasted_iota(jnp.int32, (t, t), 0) <= lax.broadcasted_iota(jnp.int32, (t, t), 1))


def _band_bias(s_t, d_t, p_t, diag):
    n = s_t.shape[0] // BIAS_BLOCK
    rows = []
    for b in range(n):
        blocks = []
        for a in range(n):
            blk = s_t[b * BIAS_BLOCK:(b + 1) * BIAS_BLOCK, a * BIAS_BLOCK:(a + 1) * BIAS_BLOCK]
            if diag and b > a:
                blk = jnp.full_like(blk, NEG)
            elif diag and b == a:
                blk = blk + d_t
            elif (diag and b == a - 1) or (not diag and b == n - 1 and a == 0):
                blk = blk + p_t
            blocks.append(blk)
        rows.append(jnp.concatenate(blocks, axis=1))
    return jnp.concatenate(rows, axis=0)


def _sweep(step, qi, carries, near_bias):
    if near_bias:
        carries = lax.fori_loop(0, jnp.maximum(qi - 1, 0), lambda j, c: step(j, c, "far"), carries)
        carries = lax.cond(qi >= 1, lambda c: step(qi - 1, c, "near"), lambda c: c, carries)
    else:
        carries = lax.fori_loop(0, qi, lambda j, c: step(j, c, "far"), carries)
    return step(qi, carries, "diag")


def _attn_specs(t, seq, nq, cb, vrow):
    w = GROUP_WIDTH
    return [pl.BlockSpec((t, w), lambda b, i: (b * nq + i, cb)),
            pl.BlockSpec((seq, w), lambda b, i: (b, cb + 1)),
            pl.BlockSpec((w, seq), lambda b, i: (vrow, b))]


def _attn_call(kernel_fn, name, batch, seq, t, in_specs, args, scratch=()):
    nq = seq // t
    return pl.pallas_call(
        kernel_fn,
        grid=(batch, nq),
        in_specs=in_specs,
        out_specs=pl.BlockSpec((t, GROUP_WIDTH), lambda b, i: (b * nq + i, 0)),
        out_shape=jax.ShapeDtypeStruct((batch * seq, GROUP_WIDTH), BF16),
        scratch_shapes=list(scratch),
        compiler_params=_params("arbitrary", "arbitrary"),
        name=name,
    )(*args)


def _head_queries(q_ref, width):
    out = []
    for lo in range(0, GROUP_WIDTH, width):
        p = lo // LANES
        out.append(_lane_select(q_ref[:, p * LANES:(p + 1) * LANES], lo - p * LANES, lo - p * LANES + width))
    return out


def _fox_kernel(q_ref, k_ref, vt_ref, c_ref, o_ref, *, t):
    qi = pl.program_id(1)
    causal = _causal_t(t)
    qms = _head_queries(q_ref, HEAD_DIM)

    def step(j, carries, kind):
        start = pl.multiple_of(j * t, t)
        c_cols = c_ref[pl.ds(start, t), :]
        out = []
        for h in range(GROUP_HEADS):
            sl = slice((h // 2) * LANES, (h // 2 + 1) * LANES)
            s_t = lax.dot_general(k_ref[pl.ds(start, t), sl], qms[h], _NT, preferred_element_type=F32)
            s_t = s_t - c_cols[:, h:h + 1]
            if kind == "diag":
                s_t = jnp.where(causal, s_t, NEG)
            out.append(_online_update(s_t, vt_ref[sl, pl.ds(start, t)], carries[h]))
        return tuple(out)

    carries = _sweep(step, qi, tuple(_init_carry(t) for _ in range(GROUP_HEADS)), False)
    outs = [acc / l for (_, l, acc) in carries]
    for p in range(GROUP_HEADS // 2):
        o_t = _merge_pair(outs[2 * p], outs[2 * p + 1])
        o_ref[:, p * LANES:(p + 1) * LANES] = o_t.T.astype(o_ref.dtype)


def _fox_attention(h_bf, v_t, cum_cols, batch, seq, cb, vrow, t=ATTN_TILE):
    nq = seq // t
    specs = _attn_specs(t, seq, nq, cb, vrow) + [pl.BlockSpec((seq, 8), lambda b, i: (b, 0))]
    return _attn_call(functools.partial(_fox_kernel, t=t), "fox_attn", batch, seq, t, specs,
                      (h_bf, h_bf, v_t, cum_cols))


def _diff_kernel(q_ref, k_ref, vt_ref, bd_ref, bp_ref, lam_ref, lc_ref, g_ref, o_ref, *, t):
    qi = pl.program_id(1)
    lam_p = lam_ref[...]
    lam = (jnp.exp(jnp.sum(lam_p[0:1] * lam_p[1:2], axis=-1, keepdims=True))
           - jnp.exp(jnp.sum(lam_p[2:3] * lam_p[3:4], axis=-1, keepdims=True)) + lc_ref[0:1, 0:1])
    qms = _head_queries(q_ref, DIFF_QK_DIM)

    def step(j, carries, kind):
        start = pl.multiple_of(j * t, t)
        out = []
        for c in range(2 * GROUP_HEADS):
            h = c // 2
            sl = slice((h // 2) * LANES, (h // 2 + 1) * LANES)
            s_t = lax.dot_general(k_ref[pl.ds(start, t), sl], qms[c], _NT, preferred_element_type=F32)
            if kind != "far":
                s_t = _band_bias(s_t, bd_ref[h], bp_ref[h], kind == "diag")
            out.append(_online_update(s_t, vt_ref[sl, pl.ds(start, t)], carries[c]))
        return tuple(out)

    carries = _sweep(step, qi, tuple(_init_carry(t) for _ in range(2 * GROUP_HEADS)), True)
    row = lax.broadcasted_iota(jnp.int32, (LANES, t), 0)
    outs = []
    for h in range(GROUP_HEADS):
        (_, l1, a1), (_, l2, a2) = carries[2 * h], carries[2 * h + 1]
        o = a1 / l1 - lam * (a2 / l2)
        lo = (h % 2) * HEAD_DIM
        o = jnp.where((row >= lo) & (row < lo + HEAD_DIM), o, 0.0)
        ms = jnp.sum(o * o, axis=0, keepdims=True) * (1.0 / HEAD_DIM)
        outs.append(o * lax.rsqrt(ms + DIFF_SUBLN_EPS))
    for p in range(GROUP_HEADS // 2):
        sl = slice(p * LANES, (p + 1) * LANES)
        o_t = _merge_pair(outs[2 * p], outs[2 * p + 1])
        o_ref[:, sl] = (o_t.T * g_ref[:, sl] * lc_ref[1:2, 0:1]).astype(o_ref.dtype)


def _diff_attention(h_bf, v_t, bias_d, bias_p, lam_p, lam_c, gain, batch, seq, cb, vrow, t=ATTN_TILE):
    nq = seq // t
    bias_spec = pl.BlockSpec((GROUP_HEADS, BIAS_BLOCK, BIAS_BLOCK), lambda b, i: (0, 0, 0))
    full = lambda a: pl.BlockSpec(a.shape, lambda b, i: (0, 0))
    specs = _attn_specs(t, seq, nq, cb, vrow) + [bias_spec, bias_spec, full(lam_p), full(lam_c), full(gain)]
    return _attn_call(functools.partial(_diff_kernel, t=t), "diff_attn", batch, seq, t, specs,
                      (h_bf, h_bf, v_t, bias_d, bias_p, lam_p, lam_c, gain))


def _moba_kernel(q_ref, k_ref, vt_ref, bd_ref, bp_ref, o_ref, kmean_ref, *, t, n_blocks):
    qi = pl.program_id(1)
    per_tile = t // MOBA_BLOCK

    @pl.when(qi == 0)
    def _():
        kmean_ref[...] = jnp.zeros_like(kmean_ref)
        for blk in range(n_blocks):
            kb = k_ref[blk * MOBA_BLOCK:(blk + 1) * MOBA_BLOCK, :].astype(F32)
            kmean_ref[blk:blk + 1, :] = jnp.sum(kb, axis=0, keepdims=True) * (1.0 / MOBA_BLOCK)

    blk_i = lax.broadcasted_iota(jnp.int32, (t, LANES), 1)
    blk_f = blk_i.astype(F32)
    row_i = lax.broadcasted_iota(jnp.int32, (t, LANES), 0)
    row_blk = sum((row_i >= n * MOBA_BLOCK).astype(jnp.int32) for n in range(1, per_tile))
    own = qi * per_tile + row_blk
    qms = _head_queries(q_ref, HEAD_DIM)
    q_cats = []
    for h in range(GROUP_HEADS):
        sl = slice((h // 2) * LANES, (h // 2 + 1) * LANES)
        gate = lax.dot_general(qms[h].astype(F32), kmean_ref[:, sl], _NT,
                               precision=lax.Precision.HIGHEST, preferred_element_type=F32)
        g = jnp.where(blk_i < own, gate, NEG)
        keep = jnp.where(blk_i == own, 1.0, 0.0)
        for _ in range(MOBA_TOPK):
            mx = jnp.max(g, axis=-1, keepdims=True)
            first = jnp.min(jnp.where(g == mx, blk_f, 1e9), axis=-1, keepdims=True)
            pick = jnp.where(mx > 0.5 * NEG, jnp.where(blk_f == first, 1.0, 0.0), 0.0)
            keep = jnp.maximum(keep, pick)
            g = jnp.where(pick > 0.0, NEG, g)
        q_cats.append(jnp.concatenate([qms[h], (1.0 - keep).astype(BF16)], axis=1))

    def step(j, carries, kind):
        start = pl.multiple_of(j * t, t)
        reject = jnp.where(blk_i == j * per_tile + row_blk, NEG, 0.0).astype(BF16)
        out = []
        for h in range(GROUP_HEADS):
            sl = slice((h // 2) * LANES, (h // 2 + 1) * LANES)
            k_cat = jnp.concatenate([k_ref[pl.ds(start, t), sl], reject], axis=1)
            s_t = lax.dot_general(k_cat, q_cats[h], _NT, preferred_element_type=F32)
            if kind != "far":
                s_t = _band_bias(s_t, bd_ref[h], bp_ref[h], kind == "diag")
            out.append(_online_update(s_t, vt_ref[sl, pl.ds(start, t)], carries[h]))
        return tuple(out)

    carries = _sweep(step, qi, tuple(_init_carry(t) for _ in range(GROUP_HEADS)), True)
    outs = [acc / l for (_, l, acc) in carries]
    for p in range(GROUP_HEADS // 2):
        o_t = _merge_pair(outs[2 * p], outs[2 * p + 1])
        o_ref[:, p * LANES:(p + 1) * LANES] = o_t.T.astype(o_ref.dtype)


def _moba_attention(h_bf, v_t, bias_d, bias_p, batch, seq, cb, vrow, t=ATTN_TILE):
    nq = seq // t
    bias_spec = pl.BlockSpec((GROUP_HEADS, BIAS_BLOCK, BIAS_BLOCK), lambda b, i: (1, 0, 0))
    specs = _attn_specs(t, seq, nq, cb, vrow) + [bias_spec, bias_spec]
    return _attn_call(functools.partial(_moba_kernel, t=t, n_blocks=seq // MOBA_BLOCK), "moba_attn",
                      batch, seq, t, specs, (h_bf, h_bf, v_t, bias_d, bias_p),
                      scratch=[pltpu.VMEM((LANES, GROUP_WIDTH), F32)])


def _mla_prep_kernel(c_ref, gq_ref, gkv_ref, wqa_ref, wqb_ref, wk_ref, wvt_ref, e_ref,
                     tq_ref, tk_ref, q_ref, k_ref, vt_ref):
    c = c_ref[...]
    cq = c[:, :MLA_Q_LORA]
    ckv = c[:, MLA_Q_LORA:MLA_Q_LORA + MLA_KV_LORA]
    kr = c[:, MLA_Q_LORA + MLA_KV_LORA:]
    cq = cq * lax.rsqrt(jnp.mean(cq * cq, axis=-1, keepdims=True) + RMS_EPS) * gq_ref[...]
    ckv = ckv * lax.rsqrt(jnp.mean(ckv * ckv, axis=-1, keepdims=True) + RMS_EPS) * gkv_ref[...]
    cq = cq.astype(BF16)
    ckv = ckv.astype(BF16)
    tq = tq_ref[...]
    cos_q = jnp.concatenate([tq[:, :LANES]] * GROUP_HEADS, axis=1)
    sin_q = jnp.concatenate([tq[:, LANES:]] * GROUP_HEADS, axis=1)
    q = (jnp.dot(cq, wqa_ref[...], preferred_element_type=F32) * cos_q
         + jnp.dot(cq, wqb_ref[...], preferred_element_type=F32) * sin_q)
    q_ref[...] = (q * (MLA_NOPE + MLA_ROPE) ** -0.5).astype(q_ref.dtype)
    k_rope = (kr * tk_ref[...]).astype(BF16)
    k = (jnp.dot(ckv, wk_ref[...], preferred_element_type=F32)
         + jnp.dot(k_rope, e_ref[...], preferred_element_type=F32))
    k_ref[...] = k.astype(k_ref.dtype)
    vt_ref[...] = lax.dot_general(wvt_ref[...], ckv, _NT, preferred_element_type=F32).astype(vt_ref.dtype)


def _mla_prep(c_lat, gq, gkv, wqa, wqb, wk, wvt, e_mat, tab_q, tab_k, seq, tm=512):
    m = c_lat.shape[0]
    ns = seq // tm
    full = lambda a: pl.BlockSpec(a.shape, lambda i: (0, 0))
    slot_w = GROUP_HEADS * LANES
    return pl.pallas_call(
        _mla_prep_kernel,
        grid=(m // tm,),
        in_specs=[pl.BlockSpec((tm, c_lat.shape[1]), lambda i: (i, 0)),
                  full(gq), full(gkv), full(wqa), full(wqb), full(wk), full(wvt), full(e_mat),
                  pl.BlockSpec((tm, tab_q.shape[1]), lambda i: (i % ns, 0)),
                  pl.BlockSpec((tm, tab_k.shape[1]), lambda i: (i % ns, 0))],
        out_specs=(pl.BlockSpec((tm, slot_w), lambda i: (i, 0)),
                   pl.BlockSpec((tm, slot_w), lambda i: (i, 0)),
                   pl.BlockSpec((GROUP_WIDTH, tm), lambda i: (0, i))),
        out_shape=(jax.ShapeDtypeStruct((m, slot_w), BF16),
                   jax.ShapeDtypeStruct((m, slot_w), BF16),
                   jax.ShapeDtypeStruct((GROUP_WIDTH, m), BF16)),
        compiler_params=_params("arbitrary"),
        name="mla_prep",
    )(c_lat, gq, gkv, wqa, wqb, wk, wvt, e_mat, tab_q, tab_k)


def _mla_kernel(q_ref, k_ref, vt_ref, o_ref, *, t):
    qi = pl.program_id(1)
    causal = _causal_t(t)

    def step(j, carries, kind):
        start = pl.multiple_of(j * t, t)
        out = []
        for h in range(GROUP_HEADS):
            hsl = slice(h * LANES, (h + 1) * LANES)
            vsl = slice((h // 2) * LANES, (h // 2 + 1) * LANES)
            s_t = lax.dot_general(k_ref[pl.ds(start, t), hsl], q_ref[:, hsl], _NT, preferred_element_type=F32)
            if kind == "diag":
                s_t = jnp.where(causal, s_t, NEG)
            out.append(_online_update(s_t, vt_ref[vsl, pl.ds(start, t)], carries[h]))
        return tuple(out)

    carries = _sweep(step, qi, tuple(_init_carry(t) for _ in range(GROUP_HEADS)), False)
    outs = [acc / l for (_, l, acc) in carries]
    for p in range(GROUP_HEADS // 2):
        o_t = _merge_pair(outs[2 * p], outs[2 * p + 1])
        o_ref[:, p * LANES:(p + 1) * LANES] = o_t.T.astype(o_ref.dtype)


def _mla_attention(q, k, v_t, batch, seq, t=ATTN_TILE):
    nq = seq // t
    slot_w = q.shape[1]
    specs = [pl.BlockSpec((t, slot_w), lambda b, i: (b * nq + i, 0)),
             pl.BlockSpec((seq, slot_w), lambda b, i: (b, 0)),
             pl.BlockSpec((GROUP_WIDTH, seq), lambda b, i: (0, b))]
    return _attn_call(functools.partial(_mla_kernel, t=t), "mla_attn", batch, seq, t, specs, (q, k, v_t))


def _layernorm(t, g, b):
    mu = jnp.mean(t, axis=-1, keepdims=True)
    d = t - mu
    var = jnp.mean(d * d, axis=-1, keepdims=True)
    return d * lax.rsqrt(var + LN_EPS) * g + b


def _out_kernel(o1_ref, o2_ref, o3_ref, o4_ref, w_ref, x_ref, g_ref, b_ref, of_ref, ob_ref, *, alpha):
    mix = None
    for n, o_ref in enumerate((o1_ref, o2_ref, o3_ref, o4_ref)):
        part = jnp.dot(o_ref[...], w_ref[n * GROUP_WIDTH:(n + 1) * GROUP_WIDTH, :],
                       preferred_element_type=F32)
        mix = part if mix is None else mix + part
    y = _layernorm(alpha * x_ref[...] + mix, g_ref[...], b_ref[...])
    of_ref[...] = y
    ob_ref[...] = y.astype(ob_ref.dtype)


def _out_proj(outs, w_o, x, g, b, alpha, tm=512):
    m, d = x.shape
    o_spec = pl.BlockSpec((tm, GROUP_WIDTH), lambda i: (i, 0))
    row_spec = pl.BlockSpec((tm, d), lambda i: (i, 0))
    vec_spec = pl.BlockSpec((1, d), lambda i: (0, 0))
    return pl.pallas_call(
        functools.partial(_out_kernel, alpha=alpha),
        grid=(m // tm,),
        in_specs=[o_spec, o_spec, o_spec, o_spec,
                  pl.BlockSpec(w_o.shape, lambda i: (0, 0)), row_spec, vec_spec, vec_spec],
        out_specs=(row_spec, row_spec),
        out_shape=(jax.ShapeDtypeStruct((m, d), F32), jax.ShapeDtypeStruct((m, d), BF16)),
        compiler_params=_params("arbitrary"),
        name="out_proj_ln",
    )(*outs, w_o, x, g, b)


def _ffn_kernel(xb_ref, xh_ref, xf_ref, wg_ref, wv_ref, cg_ref, cv_ref, wd_ref, g_ref, b_ref,
                of_ref, ob_ref, acc_ref, *, alpha, tiles_per_seq, n_chunks):
    i = pl.program_id(0)
    c = pl.program_id(1)
    halo = xh_ref[...]
    halo = jnp.where(i % tiles_per_seq == 0, jnp.zeros_like(halo), halo)
    xe = jnp.concatenate([halo, xb_ref[...]], axis=0)

    def branch(w_ref, cp_ref):
        u = jnp.dot(xe, w_ref[...], preferred_element_type=F32)
        cp = cp_ref[...]
        y = cp[2:3] * u + cp[1:2] * pltpu.roll(u, 1, 0) + cp[0:1] * pltpu.roll(u, 2, 0) + cp[3:4]
        return y[CONV_HALO:]

    gate = branch(wg_ref, cg_ref)
    val = branch(wv_ref, cv_ref)
    act = (gate * jax.nn.sigmoid(gate) * val).astype(BF16)
    part = jnp.dot(act, wd_ref[...], preferred_element_type=F32)

    @pl.when(c == 0)
    def _():
        acc_ref[...] = part

    @pl.when(c > 0)
    def _():
        acc_ref[...] += part

    @pl.when(c == n_chunks - 1)
    def _():
        y = _layernorm(alpha * xf_ref[...] + acc_ref[...], g_ref[...], b_ref[...])
        of_ref[...] = y
        ob_ref[...] = y.astype(ob_ref.dtype)


def _ffn(xb, xf, w_up, conv_p, w_down, g, b, alpha, seq, tm=512):
    m, d = xf.shape
    n_chunks = D_FF_PAD // FF_CHUNK
    halo_blocks = tm // CONV_HALO
    row_spec = pl.BlockSpec((tm, d), lambda i, c: (i, 0))
    vec_spec = pl.BlockSpec((1, d), lambda i, c: (0, 0))
    return pl.pallas_call(
        functools.partial(_ffn_kernel, alpha=alpha, tiles_per_seq=seq // tm, n_chunks=n_chunks),
        grid=(m // tm, n_chunks),
        in_specs=[row_spec,
                  pl.BlockSpec((CONV_HALO, d), lambda i, c: (jnp.maximum(i * halo_blocks - 1, 0), 0)),
                  row_spec,
                  pl.BlockSpec((d, FF_CHUNK), lambda i, c: (0, c)),
                  pl.BlockSpec((d, FF_CHUNK), lambda i, c: (0, n_chunks + c)),
                  pl.BlockSpec((8, FF_CHUNK), lambda i, c: (0, c)),
                  pl.BlockSpec((8, FF_CHUNK), lambda i, c: (0, n_chunks + c)),
                  pl.BlockSpec((FF_CHUNK, d), lambda i, c: (c, 0)),
                  vec_spec, vec_spec],
        out_specs=(row_spec, row_spec),
        out_shape=(jax.ShapeDtypeStruct((m, d), F32), jax.ShapeDtypeStruct((m, d), BF16)),
        scratch_shapes=[pltpu.VMEM((tm, d), F32)],
        compiler_params=_params("arbitrary", "arbitrary"),
        name="conv_ffn_ln",
    )(xb, xb, xf, w_up, w_up, conv_p, conv_p, w_down, g, b)


def _swap_halves(w):
    half = w.shape[-1] // 2
    return jnp.concatenate([-w[..., half:], w[..., :half]], axis=-1)


def _prep_weights(w_in, mla_w_uq, mla_w_ukv, w_up, conv_w, conv_b, w_down):
    depth, d, _ = w_in.shape
    gw = GROUP_WIDTH
    fox0 = 0
    diff0 = 3 * gw + GROUP_HEADS
    moba0 = diff0 + 3 * gw
    mla0 = moba0 + 3 * gw
    sm = HEAD_DIM ** -0.5
    sd = DIFF_QK_DIM ** -0.5
    w_qk = jnp.concatenate([
        w_in[:, :, fox0:fox0 + gw] * sm, w_in[:, :, fox0 + gw:fox0 + 2 * gw],
        w_in[:, :, diff0:diff0 + gw] * sd, w_in[:, :, diff0 + gw:diff0 + 2 * gw],
        w_in[:, :, moba0:moba0 + gw] * sm, w_in[:, :, moba0 + gw:moba0 + 2 * gw]], axis=-1).astype(BF16)
    w_t = jnp.swapaxes(jnp.concatenate([
        w_in[:, :, fox0 + 2 * gw:fox0 + 3 * gw], w_in[:, :, diff0 + 2 * gw:diff0 + 3 * gw],
        w_in[:, :, moba0 + 2 * gw:moba0 + 3 * gw], w_in[:, :, 3 * gw:3 * gw + GROUP_HEADS],
        jnp.zeros((depth, d, 8 - GROUP_HEADS), F32)], axis=-1), 1, 2)
    w_vt, w_gt = w_t[:, :3 * gw].astype(BF16), w_t[:, 3 * gw:].astype(BF16)
    kr0 = mla0 + MLA_Q_LORA + MLA_KV_LORA
    w_kr = w_in[:, :, kr0:kr0 + MLA_ROPE]
    w_lat = jnp.concatenate([w_in[:, :, mla0:kr0], w_kr, _swap_halves(w_kr),
                             jnp.zeros((depth, d, LANES - 2 * MLA_ROPE), F32)], axis=-1).astype(BF16)

    uq = mla_w_uq.reshape(depth, MLA_Q_LORA, GROUP_HEADS, MLA_NOPE + MLA_ROPE)
    pad = LANES - MLA_NOPE - MLA_ROPE
    wqa = jnp.pad(uq, ((0, 0), (0, 0), (0, 0), (0, pad)))
    wqb = jnp.concatenate([jnp.zeros_like(uq[..., :MLA_NOPE]), _swap_halves(uq[..., MLA_NOPE:]),
                           jnp.zeros(uq.shape[:-1] + (pad,), F32)], axis=-1)
    ukv = mla_w_ukv.reshape(depth, MLA_KV_LORA, GROUP_HEADS, MLA_NOPE + HEAD_DIM)
    wk = jnp.pad(ukv[..., :MLA_NOPE], ((0, 0), (0, 0), (0, 0), (0, LANES - MLA_NOPE)))
    flat = lambda a: a.reshape(depth, a.shape[1], -1).astype(BF16)
    wvt = jnp.swapaxes(flat(ukv[..., MLA_NOPE:]), 1, 2)

    ffp = D_FF_PAD - D_FF
    padc = lambda a: jnp.pad(a, ((0, 0), (0, 0), (0, ffp)))
    w_up_p = jnp.concatenate([padc(w_up[:, :, :D_FF]), padc(w_up[:, :, D_FF:])], axis=-1).astype(BF16)
    conv = jnp.concatenate([conv_w, conv_b[:, None, :],
                            jnp.zeros((depth, 8 - CONV_WIDTH - 1, 2 * D_FF), F32)], axis=1)
    conv_p = jnp.concatenate([padc(conv[:, :, :D_FF]), padc(conv[:, :, D_FF:])], axis=-1)
    w_down_p = jnp.pad(w_down, ((0, 0), (0, ffp), (0, 0))).astype(BF16)
    return w_qk, w_vt, w_gt, w_lat, flat(wqa), flat(wqb), flat(wk), wvt, w_up_p, conv_p, w_down_p


def _rope_constants(seq):
    inv = ROPE_THETA ** (-jnp.arange(0, MLA_ROPE // 2, dtype=F32) * 2.0 / MLA_ROPE)
    ang = jnp.arange(seq, dtype=F32)[:, None] * inv[None, :]
    cos, sin = jnp.cos(ang), jnp.sin(ang)
    cos2 = jnp.concatenate([cos, cos], axis=1)
    sin2 = jnp.concatenate([sin, sin], axis=1)
    pad = jnp.zeros((seq, LANES - MLA_NOPE - MLA_ROPE), F32)
    tab_q = jnp.concatenate([jnp.ones((seq, MLA_NOPE), F32), cos2, pad,
                             jnp.zeros((seq, MLA_NOPE), F32), sin2, pad], axis=1)
    tab_k = jnp.concatenate([cos2, sin2, jnp.zeros((seq, LANES - 2 * MLA_ROPE), F32)], axis=1)
    r = jnp.arange(LANES)[:, None]
    c = jnp.arange(GROUP_HEADS * LANES)[None, :]
    e_mat = ((r < 2 * MLA_ROPE) & (c % LANES == MLA_NOPE + r % MLA_ROPE)).astype(BF16)
    return tab_q, tab_k, e_mat


def kernel(x, w_in, b_forget, diff_lambda, diff_subln, mla_q_norm, mla_kv_norm, mla_w_uq, mla_w_ukv,
           rel_bias, w_o, ln1_g, ln1_b, w_up, conv_w, conv_b, w_down, ln2_g, ln2_b):
    batch, seq, d = x.shape
    depth = w_in.shape[0]
    alpha = (2 * depth) ** 0.25
    assert seq % ATTN_TILE == 0 and ATTN_TILE % MOBA_BLOCK == 0 and seq // MOBA_BLOCK <= LANES

    (w_qk, w_vt, w_gt, w_lat, wqa, wqb, wk, wvt, w_up_p, conv_p, w_down_p) = _prep_weights(
        w_in, mla_w_uq, mla_w_ukv, w_up, conv_w, conv_b, w_down)
    w_o_b = w_o.astype(BF16)
    tab_q, tab_k, e_mat = _rope_constants(seq)
    bias_d, bias_p = _bias_blocks(rel_bias, BIAS_BLOCK)
    b_col = jnp.pad(b_forget, ((0, 0), (0, 8 - GROUP_HEADS)))[:, :, None]
    gain = jnp.tile(diff_subln, (1, GROUP_HEADS))[:, None, :]

    xf = x.reshape(batch * seq, d)
    xb = xf.astype(BF16)
    for l in range(depth):
        h_bf = _matmul(xb, w_qk[l], BF16, tm=512, tn=3 * GROUP_WIDTH)
        v_t = _matmul_nt(w_vt[l], xb, BF16, tm=512)
        c_lat = _matmul(xb, w_lat[l], F32, tm=512, tn=w_lat.shape[-1])
        cum = _fox_cumlog(_matmul_nt(w_gt[l], xb, F32, tm=1024), b_col[l], batch, seq)
        fox_o = _fox_attention(h_bf, v_t, cum.T, batch, seq, 0, 0)
        lam_init = 0.8 - 0.6 * math.exp(-0.3 * l)
        lam_c = jnp.concatenate([jnp.full((1, LANES), lam_init, F32),
                                 jnp.full((1, LANES), 1.0 - lam_init, F32),
                                 jnp.zeros((6, LANES), F32)], axis=0)
        diff_o = _diff_attention(h_bf, v_t, bias_d, bias_p, diff_lambda[l], lam_c, gain[l], batch, seq, 2, 1)
        moba_o = _moba_attention(h_bf, v_t, bias_d, bias_p, batch, seq, 4, 2)
        q_m, k_m, v_m = _mla_prep(c_lat, mla_q_norm[l][None], mla_kv_norm[l][None],
                                  wqa[l], wqb[l], wk[l], wvt[l], e_mat, tab_q, tab_k, seq)
        mla_o = _mla_attention(q_m, k_m, v_m, batch, seq)
        xf, xb = _out_proj((fox_o, diff_o, moba_o, mla_o), w_o_b[l], xf, ln1_g[l][None], ln1_b[l][None], alpha)
        xf, xb = _ffn(xb, xf, w_up_p[l], conv_p[l], w_down_p[l], ln2_g[l][None], ln2_b[l][None], alpha, seq)
    return xf.reshape(batch, seq, d)
```

```python
import functools
import math

import jax
import jax.numpy as jnp
from jax import lax
from jax.experimental import pallas as pl
from jax.experimental.pallas import tpu as pltpu

F32 = jnp.float32
BF16 = jnp.bfloat16

HEAD_DIM = 64
GROUP_HEADS = 4
GROUP_WIDTH = GROUP_HEADS * HEAD_DIM
LANES = 128
DIFF_QK_DIM = HEAD_DIM // 2
DIFF_SUBLN_EPS = 1e-5
MOBA_BLOCK = 256
MOBA_TOPK = 3
MLA_Q_LORA = 256
MLA_KV_LORA = 128
MLA_NOPE = 64
MLA_ROPE = 32
ROPE_THETA = 10000.0
REL_BUCKETS = 32
REL_MAX_DIST = 128
D_FF = 2752
FF_CHUNK = 256
D_FF_PAD = -(-D_FF // FF_CHUNK) * FF_CHUNK
CONV_WIDTH = 3
CONV_HALO = 16
LN_EPS = 1e-5
RMS_EPS = 1e-6
NEG = -1e30
LOG2E = 1.4426950408889634

BIAS_BLOCK = 128
ATTN_TILE = 512
SCORE_LOOKAHEAD = 2
VMEM_LIMIT = 56 * 1024 * 1024

_NT = (((1,), (1,)), ((), ()))


def _params(*sem):
    return pltpu.CompilerParams(dimension_semantics=sem, vmem_limit_bytes=VMEM_LIMIT)


def _mm_kernel(x_ref, w_ref, o_ref):
    o_ref[...] = jnp.dot(x_ref[...], w_ref[...], preferred_element_type=F32).astype(o_ref.dtype)


def _matmul(x, w, out_dtype, tm, tn):
    m, k = x.shape
    n = w.shape[1]
    return pl.pallas_call(
        _mm_kernel,
        grid=(m // tm, n // tn),
        in_specs=[pl.BlockSpec((tm, k), lambda i, j: (i, 0)),
                  pl.BlockSpec((k, tn), lambda i, j: (0, j))],
        out_specs=pl.BlockSpec((tm, tn), lambda i, j: (i, j)),
        out_shape=jax.ShapeDtypeStruct((m, n), out_dtype),
        compiler_params=_params("arbitrary", "arbitrary"),
        name="in_proj",
    )(x, w)


def _mm_nt_kernel(w_ref, x_ref, o_ref):
    o_ref[...] = lax.dot_general(w_ref[...], x_ref[...], _NT, preferred_element_type=F32).astype(o_ref.dtype)


def _matmul_nt(w_t, x, out_dtype, tm):
    n, k = w_t.shape
    m = x.shape[0]
    return pl.pallas_call(
        _mm_nt_kernel,
        grid=(m // tm,),
        in_specs=[pl.BlockSpec((n, k), lambda i: (0, 0)),
                  pl.BlockSpec((tm, k), lambda i: (i, 0))],
        out_specs=pl.BlockSpec((n, tm), lambda i: (0, i)),
        out_shape=jax.ShapeDtypeStruct((n, m), out_dtype),
        compiler_params=_params("arbitrary"),
        name="in_proj_t",
    )(w_t, x)


def _cum_kernel(g_ref, b_ref, o_ref):
    x = g_ref[...] + b_ref[...]
    x = jnp.minimum(x, 0.0) - jnp.log(1.0 + jnp.exp(-jnp.abs(x)))
    n = x.shape[1]
    lane = lax.broadcasted_iota(jnp.int32, x.shape, 1)
    k = 1
    while k < n:
        x = x + jnp.where(lane >= k, pltpu.roll(x, k, 1), 0.0)
        k *= 2
    o_ref[...] = x * LOG2E


def _fox_cumlog(logits_t, b_col, batch, seq):
    return pl.pallas_call(
        _cum_kernel,
        grid=(batch,),
        in_specs=[pl.BlockSpec((8, seq), lambda b: (0, b)),
                  pl.BlockSpec((8, 1), lambda b: (0, 0))],
        out_specs=pl.BlockSpec((8, seq), lambda b: (0, b)),
        out_shape=jax.ShapeDtypeStruct(logits_t.shape, F32),
        compiler_params=_params("arbitrary"),
        name="fox_cumlog",
    )(logits_t, b_col)


def _bias_kernel(rb_ref, d_ref, p_ref, *, t):
    h = pl.program_id(0)
    i = lax.broadcasted_iota(jnp.int32, (t, t), 0)
    j = lax.broadcasted_iota(jnp.int32, (t, t), 1)
    max_exact = REL_BUCKETS // 2
    far = rb_ref[(REL_BUCKETS - 1) * 8 + h]
    for out_ref, off in ((d_ref, 0), (p_ref, t)):
        dist = j - i + off
        d_large = jnp.maximum(dist, max_exact).astype(F32)
        large = max_exact + (jnp.log(d_large / max_exact) / math.log(REL_MAX_DIST / max_exact)
                             * (REL_BUCKETS - max_exact)).astype(jnp.int32)
        large = jnp.minimum(large, REL_BUCKETS - 1)
        bucket = jnp.where(dist < max_exact, dist, large)
        acc = jnp.zeros((t, t), F32)
        for bkt in range(REL_BUCKETS - 1):
            acc = jnp.where(bucket == bkt, (rb_ref[bkt * 8 + h] - far) * LOG2E, acc)
        if off == 0:
            acc = jnp.where(dist >= 0, acc, NEG)
        out_ref[0] = acc


def _bias_blocks(rel_bias, t):
    n_heads = rel_bias.shape[1]
    shape = jax.ShapeDtypeStruct((n_heads, t, t), F32)
    spec = pl.BlockSpec((1, t, t), lambda h: (h, 0, 0))
    return pl.pallas_call(
        functools.partial(_bias_kernel, t=t),
        grid=(n_heads,),
        in_specs=[pl.BlockSpec(memory_space=pltpu.SMEM)],
        out_specs=(spec, spec),
        out_shape=(shape, shape),
        compiler_params=_params("arbitrary"),
        name="rel_bias_blocks",
    )(rel_bias.reshape(-1))


def _online_update(s_t, v_t, carry):
    m, l, acc = carry
    m_new = jnp.maximum(m, jnp.max(s_t, axis=0, keepdims=True))
    p = jnp.exp2(s_t - m_new)
    alpha = jnp.exp2(m - m_new)
    l = alpha * l + jnp.sum(p, axis=0, keepdims=True)
    acc = alpha * acc + jnp.dot(v_t, p.astype(BF16), preferred_element_type=F32)
    return m_new, l, acc


def _sweep(n, t, qi, raw, fix, values, near_bias, s_ref):
    la = s_ref.shape[0]

    def step(j, carries, kind):
        ahead = {}
        out = []
        for c in range(n):
            if c + la < n:
                ahead[c + la] = raw(c + la, j)
            elif kind != "diag":
                s_ref[c + la - n] = raw(c + la - n, j + 1)
            s_t = s_ref[c] if c < la else ahead.pop(c)
            out.append(_online_update(fix(c, s_t, j, kind), values(c, j), carries[c]))
        return tuple(out)

    for c in range(la):
        s_ref[c] = raw(c, 0)
    carries = tuple(_init_carry(t) for _ in range(n))
    if near_bias:
        carries = lax.fori_loop(0, jnp.maximum(qi - 1, 0), lambda j, s: step(j, s, "far"), carries)
        carries = lax.cond(qi >= 1, lambda s: step(qi - 1, s, "near"), lambda s: s, carries)
    else:
        carries = lax.fori_loop(0, qi, lambda j, s: step(j, s, "far"), carries)
    return step(qi, carries, "diag")


def _init_carry(t):
    return (jnp.full((1, t), NEG, F32), jnp.zeros((1, t), F32), jnp.zeros((LANES, t), F32))


def _lane_select(x, lo, hi):
    lane = lax.broadcasted_iota(jnp.int32, x.shape, 1)
    return jnp.where((lane >= lo) & (lane < hi), x, jnp.zeros_like(x))


def _merge_pair(a, b):
    row = lax.broadcasted_iota(jnp.int32, a.shape, 0)
    return jnp.where(row < HEAD_DIM, a, b)


def _causal_t(t):
    return (lax.broadcasted_iota(jnp.int32, (t, t), 0) <= lax.broadcasted_iota(jnp.int32, (t, t), 1))


def _band_bias(s_t, d_t, p_t, diag):
    n = s_t.shape[0] // BIAS_BLOCK
    rows = []
    for b in range(n):
        blocks = []
        for a in range(n):
            blk = s_t[b * BIAS_BLOCK:(b + 1) * BIAS_BLOCK, a * BIAS_BLOCK:(a + 1) * BIAS_BLOCK]
            if diag and b > a:
                blk = jnp.full_like(blk, NEG)
            elif diag and b == a:
                blk = blk + d_t
            elif (diag and b == a - 1) or (not diag and b == n - 1 and a == 0):
                blk = blk + p_t
            blocks.append(blk)
        rows.append(jnp.concatenate(blocks, axis=1))
    return jnp.concatenate(rows, axis=0)


def _attn_specs(t, seq, nq, cb, vrow):
    w = GROUP_WIDTH
    return [pl.BlockSpec((t, w), lambda b, i: (b * nq + i, cb)),
            pl.BlockSpec((seq, w), lambda b, i: (b, cb + 1)),
            pl.BlockSpec((w, seq), lambda b, i: (vrow, b))]


def _attn_call(kernel_fn, name, batch, seq, t, in_specs, args, scratch=()):
    nq = seq // t
    return pl.pallas_call(
        kernel_fn,
        grid=(batch, nq),
        in_specs=in_specs,
        out_specs=pl.BlockSpec((t, GROUP_WIDTH), lambda b, i: (b * nq + i, 0)),
        out_shape=jax.ShapeDtypeStruct((batch * seq, GROUP_WIDTH), BF16),
        scratch_shapes=[pltpu.VMEM((SCORE_LOOKAHEAD, t, t), F32)] + list(scratch),
        compiler_params=_params("arbitrary", "arbitrary"),
        name=name,
    )(*args)


def _head_queries(q_ref, width):
    out = []
    for lo in range(0, GROUP_WIDTH, width):
        p = lo // LANES
        out.append(_lane_select(q_ref[:, p * LANES:(p + 1) * LANES], lo - p * LANES, lo - p * LANES + width))
    return out


def _fox_kernel(q_ref, k_ref, vt_ref, c_ref, o_ref, s_ref, *, t):
    qi = pl.program_id(1)
    causal = _causal_t(t)
    qms = _head_queries(q_ref, HEAD_DIM)

    def raw(h, j):
        sl = slice((h // 2) * LANES, (h // 2 + 1) * LANES)
        return lax.dot_general(k_ref[pl.ds(pl.multiple_of(j * t, t), t), sl], qms[h], _NT,
                               preferred_element_type=F32)

    def fix(h, s_t, j, kind):
        s_t = s_t - c_ref[pl.ds(pl.multiple_of(j * t, t), t), h:h + 1]
        return jnp.where(causal, s_t, NEG) if kind == "diag" else s_t

    def values(h, j):
        return vt_ref[(h // 2) * LANES:(h // 2 + 1) * LANES, pl.ds(pl.multiple_of(j * t, t), t)]

    carries = _sweep(GROUP_HEADS, t, qi, raw, fix, values, False, s_ref)
    outs = [acc / l for (_, l, acc) in carries]
    for p in range(GROUP_HEADS // 2):
        o_t = _merge_pair(outs[2 * p], outs[2 * p + 1])
        o_ref[:, p * LANES:(p + 1) * LANES] = o_t.T.astype(o_ref.dtype)


def _fox_attention(h_bf, v_t, cum_cols, batch, seq, cb, vrow, t=ATTN_TILE):
    nq = seq // t
    specs = _attn_specs(t, seq, nq, cb, vrow) + [pl.BlockSpec((seq, 8), lambda b, i: (b, 0))]
    return _attn_call(functools.partial(_fox_kernel, t=t), "fox_attn", batch, seq, t, specs,
                      (h_bf, h_bf, v_t, cum_cols))


def _diff_kernel(q_ref, k_ref, vt_ref, bd_ref, bp_ref, lam_ref, lc_ref, g_ref, o_ref, s_ref, *, t):
    qi = pl.program_id(1)
    lam_p = lam_ref[...]
    lam = (jnp.exp(jnp.sum(lam_p[0:1] * lam_p[1:2], axis=-1, keepdims=True))
           - jnp.exp(jnp.sum(lam_p[2:3] * lam_p[3:4], axis=-1, keepdims=True)) + lc_ref[0:1, 0:1])
    qms = _head_queries(q_ref, DIFF_QK_DIM)

    def raw(c, j):
        sl = slice((c // 4) * LANES, (c // 4 + 1) * LANES)
        return lax.dot_general(k_ref[pl.ds(pl.multiple_of(j * t, t), t), sl], qms[c], _NT,
                               preferred_element_type=F32)

    def fix(c, s_t, j, kind):
        return s_t if kind == "far" else _band_bias(s_t, bd_ref[c // 2], bp_ref[c // 2], kind == "diag")

    def values(c, j):
        return vt_ref[(c // 4) * LANES:(c // 4 + 1) * LANES, pl.ds(pl.multiple_of(j * t, t), t)]

    carries = _sweep(2 * GROUP_HEADS, t, qi, raw, fix, values, True, s_ref)
    row = lax.broadcasted_iota(jnp.int32, (LANES, t), 0)
    outs = []
    for h in range(GROUP_HEADS):
        (_, l1, a1), (_, l2, a2) = carries[2 * h], carries[2 * h + 1]
        o = a1 / l1 - lam * (a2 / l2)
        lo = (h % 2) * HEAD_DIM
        o = jnp.where((row >= lo) & (row < lo + HEAD_DIM), o, 0.0)
        ms = jnp.sum(o * o, axis=0, keepdims=True) * (1.0 / HEAD_DIM)
        outs.append(o * lax.rsqrt(ms + DIFF_SUBLN_EPS))
    for p in range(GROUP_HEADS // 2):
        sl = slice(p * LANES, (p + 1) * LANES)
        o_t = _merge_pair(outs[2 * p], outs[2 * p + 1])
        o_ref[:, sl] = (o_t.T * g_ref[:, sl] * lc_ref[1:2, 0:1]).astype(o_ref.dtype)


def _diff_attention(h_bf, v_t, bias_d, bias_p, lam_p, lam_c, gain, batch, seq, cb, vrow, t=ATTN_TILE):
    nq = seq // t
    bias_spec = pl.BlockSpec((GROUP_HEADS, BIAS_BLOCK, BIAS_BLOCK), lambda b, i: (0, 0, 0))
    full = lambda a: pl.BlockSpec(a.shape, lambda b, i: (0, 0))
    specs = _attn_specs(t, seq, nq, cb, vrow) + [bias_spec, bias_spec, full(lam_p), full(lam_c), full(gain)]
    return _attn_call(functools.partial(_diff_kernel, t=t), "diff_attn", batch, seq, t, specs,
                      (h_bf, h_bf, v_t, bias_d, bias_p, lam_p, lam_c, gain))


def _moba_kernel(q_ref, k_ref, vt_ref, bd_ref, bp_ref, o_ref, s_ref, kmean_ref, *, t, n_blocks):
    qi = pl.program_id(1)
    per_tile = t // MOBA_BLOCK

    @pl.when(qi == 0)
    def _():
        kmean_ref[...] = jnp.zeros_like(kmean_ref)
        for blk in range(n_blocks):
            kb = k_ref[blk * MOBA_BLOCK:(blk + 1) * MOBA_BLOCK, :].astype(F32)
            kmean_ref[blk:blk + 1, :] = jnp.sum(kb, axis=0, keepdims=True) * (1.0 / MOBA_BLOCK)

    blk_i = lax.broadcasted_iota(jnp.int32, (t, LANES), 1)
    blk_f = blk_i.astype(F32)
    row_i = lax.broadcasted_iota(jnp.int32, (t, LANES), 0)
    row_blk = sum((row_i >= n * MOBA_BLOCK).astype(jnp.int32) for n in range(1, per_tile))
    own = qi * per_tile + row_blk
    qms = _head_queries(q_ref, HEAD_DIM)
    q_cats = []
    for h in range(GROUP_HEADS):
        sl = slice((h // 2) * LANES, (h // 2 + 1) * LANES)
        gate = lax.dot_general(qms[h].astype(F32), kmean_ref[:, sl], _NT,
                               precision=lax.Precision.HIGHEST, preferred_element_type=F32)
        g = jnp.where(blk_i < own, gate, NEG)
        keep = jnp.where(blk_i == own, 1.0, 0.0)
        for _ in range(MOBA_TOPK):
            mx = jnp.max(g, axis=-1, keepdims=True)
            first = jnp.min(jnp.where(g == mx, blk_f, 1e9), axis=-1, keepdims=True)
            pick = jnp.where(mx > 0.5 * NEG, jnp.where(blk_f == first, 1.0, 0.0), 0.0)
            keep = jnp.maximum(keep, pick)
            g = jnp.where(pick > 0.0, NEG, g)
        q_cats.append(jnp.concatenate([qms[h], (1.0 - keep).astype(BF16)], axis=1))

    def raw(h, j):
        sl = slice((h // 2) * LANES, (h // 2 + 1) * LANES)
        reject = jnp.where(blk_i == j * per_tile + row_blk, NEG, 0.0).astype(BF16)
        k_cat = jnp.concatenate([k_ref[pl.ds(pl.multiple_of(j * t, t), t), sl], reject], axis=1)
        return lax.dot_general(k_cat, q_cats[h], _NT, preferred_element_type=F32)

    def fix(h, s_t, j, kind):
        return s_t if kind == "far" else _band_bias(s_t, bd_ref[h], bp_ref[h], kind == "diag")

    def values(h, j):
        return vt_ref[(h // 2) * LANES:(h // 2 + 1) * LANES, pl.ds(pl.multiple_of(j * t, t), t)]

    carries = _sweep(GROUP_HEADS, t, qi, raw, fix, values, True, s_ref)
    outs = [acc / l for (_, l, acc) in carries]
    for p in range(GROUP_HEADS // 2):
        o_t = _merge_pair(outs[2 * p], outs[2 * p + 1])
        o_ref[:, p * LANES:(p + 1) * LANES] = o_t.T.astype(o_ref.dtype)


def _moba_attention(h_bf, v_t, bias_d, bias_p, batch, seq, cb, vrow, t=ATTN_TILE):
    nq = seq // t
    bias_spec = pl.BlockSpec((GROUP_HEADS, BIAS_BLOCK, BIAS_BLOCK), lambda b, i: (1, 0, 0))
    specs = _attn_specs(t, seq, nq, cb, vrow) + [bias_spec, bias_spec]
    return _attn_call(functools.partial(_moba_kernel, t=t, n_blocks=seq // MOBA_BLOCK), "moba_attn",
                      batch, seq, t, specs, (h_bf, h_bf, v_t, bias_d, bias_p),
                      scratch=[pltpu.VMEM((LANES, GROUP_WIDTH), F32)])


def _mla_prep_kernel(c_ref, gq_ref, gkv_ref, wqa_ref, wqb_ref, wk_ref, wvt_ref, e_ref,
                     tq_ref, tk_ref, q_ref, k_ref, vt_ref):
    c = c_ref[...]
    cq = c[:, :MLA_Q_LORA]
    ckv = c[:, MLA_Q_LORA:MLA_Q_LORA + MLA_KV_LORA]
    kr = c[:, MLA_Q_LORA + MLA_KV_LORA:]
    cq = cq * lax.rsqrt(jnp.mean(cq * cq, axis=-1, keepdims=True) + RMS_EPS) * gq_ref[...]
    ckv = ckv * lax.rsqrt(jnp.mean(ckv * ckv, axis=-1, keepdims=True) + RMS_EPS) * gkv_ref[...]
    cq = cq.astype(BF16)
    ckv = ckv.astype(BF16)
    tq = tq_ref[...]
    cos_q = jnp.concatenate([tq[:, :LANES]] * GROUP_HEADS, axis=1)
    sin_q = jnp.concatenate([tq[:, LANES:]] * GROUP_HEADS, axis=1)
    q = (jnp.dot(cq, wqa_ref[...], preferred_element_type=F32) * cos_q
         + jnp.dot(cq, wqb_ref[...], preferred_element_type=F32) * sin_q)
    q_ref[...] = (q * ((MLA_NOPE + MLA_ROPE) ** -0.5 * LOG2E)).astype(q_ref.dtype)
    k_rope = (kr * tk_ref[...]).astype(BF16)
    k = (jnp.dot(ckv, wk_ref[...], preferred_element_type=F32)
         + jnp.dot(k_rope, e_ref[...], preferred_element_type=F32))
    k_ref[...] = k.astype(k_ref.dtype)
    vt_ref[...] = lax.dot_general(wvt_ref[...], ckv, _NT, preferred_element_type=F32).astype(vt_ref.dtype)


def _mla_prep(c_lat, gq, gkv, wqa, wqb, wk, wvt, e_mat, tab_q, tab_k, seq, tm=512):
    m = c_lat.shape[0]
    ns = seq // tm
    full = lambda a: pl.BlockSpec(a.shape, lambda i: (0, 0))
    slot_w = GROUP_HEADS * LANES
    return pl.pallas_call(
        _mla_prep_kernel,
        grid=(m // tm,),
        in_specs=[pl.BlockSpec((tm, c_lat.shape[1]), lambda i: (i, 0)),
                  full(gq), full(gkv), full(wqa), full(wqb), full(wk), full(wvt), full(e_mat),
                  pl.BlockSpec((tm, tab_q.shape[1]), lambda i: (i % ns, 0)),
                  pl.BlockSpec((tm, tab_k.shape[1]), lambda i: (i % ns, 0))],
        out_specs=(pl.BlockSpec((tm, slot_w), lambda i: (i, 0)),
                   pl.BlockSpec((tm, slot_w), lambda i: (i, 0)),
                   pl.BlockSpec((GROUP_WIDTH, tm), lambda i: (0, i))),
        out_shape=(jax.ShapeDtypeStruct((m, slot_w), BF16),
                   jax.ShapeDtypeStruct((m, slot_w), BF16),
                   jax.ShapeDtypeStruct((GROUP_WIDTH, m), BF16)),
        compiler_params=_params("arbitrary"),
        name="mla_prep",
    )(c_lat, gq, gkv, wqa, wqb, wk, wvt, e_mat, tab_q, tab_k)


def _mla_kernel(q_ref, k_ref, vt_ref, o_ref, s_ref, *, t):
    qi = pl.program_id(1)
    causal = _causal_t(t)

    def raw(h, j):
        hsl = slice(h * LANES, (h + 1) * LANES)
        return lax.dot_general(k_ref[pl.ds(pl.multiple_of(j * t, t), t), hsl], q_ref[:, hsl], _NT,
                               preferred_element_type=F32)

    def fix(h, s_t, j, kind):
        return jnp.where(causal, s_t, NEG) if kind == "diag" else s_t

    def values(h, j):
        return vt_ref[(h // 2) * LANES:(h // 2 + 1) * LANES, pl.ds(pl.multiple_of(j * t, t), t)]

    carries = _sweep(GROUP_HEADS, t, qi, raw, fix, values, False, s_ref)
    outs = [acc / l for (_, l, acc) in carries]
    for p in range(GROUP_HEADS // 2):
        o_t = _merge_pair(outs[2 * p], outs[2 * p + 1])
        o_ref[:, p * LANES:(p + 1) * LANES] = o_t.T.astype(o_ref.dtype)


def _mla_attention(q, k, v_t, batch, seq, t=ATTN_TILE):
    nq = seq // t
    slot_w = q.shape[1]
    specs = [pl.BlockSpec((t, slot_w), lambda b, i: (b * nq + i, 0)),
             pl.BlockSpec((seq, slot_w), lambda b, i: (b, 0)),
             pl.BlockSpec((GROUP_WIDTH, seq), lambda b, i: (0, b))]
    return _attn_call(functools.partial(_mla_kernel, t=t), "mla_attn", batch, seq, t, specs, (q, k, v_t))


def _layernorm(t, g, b):
    mu = jnp.mean(t, axis=-1, keepdims=True)
    d = t - mu
    var = jnp.mean(d * d, axis=-1, keepdims=True)
    return d * lax.rsqrt(var + LN_EPS) * g + b


def _out_kernel(o1_ref, o2_ref, o3_ref, o4_ref, w_ref, x_ref, g_ref, b_ref, of_ref, ob_ref, *, alpha):
    mix = None
    for n, o_ref in enumerate((o1_ref, o2_ref, o3_ref, o4_ref)):
        part = jnp.dot(o_ref[...], w_ref[n * GROUP_WIDTH:(n + 1) * GROUP_WIDTH, :],
                       preferred_element_type=F32)
        mix = part if mix is None else mix + part
    y = _layernorm(alpha * x_ref[...] + mix, g_ref[...], b_ref[...])
    of_ref[...] = y
    ob_ref[...] = y.astype(ob_ref.dtype)


def _out_proj(outs, w_o, x, g, b, alpha, tm=512):
    m, d = x.shape
    o_spec = pl.BlockSpec((tm, GROUP_WIDTH), lambda i: (i, 0))
    row_spec = pl.BlockSpec((tm, d), lambda i: (i, 0))
    vec_spec = pl.BlockSpec((1, d), lambda i: (0, 0))
    return pl.pallas_call(
        functools.partial(_out_kernel, alpha=alpha),
        grid=(m // tm,),
        in_specs=[o_spec, o_spec, o_spec, o_spec,
                  pl.BlockSpec(w_o.shape, lambda i: (0, 0)), row_spec, vec_spec, vec_spec],
        out_specs=(row_spec, row_spec),
        out_shape=(jax.ShapeDtypeStruct((m, d), F32), jax.ShapeDtypeStruct((m, d), BF16)),
        compiler_params=_params("arbitrary"),
        name="out_proj_ln",
    )(*outs, w_o, x, g, b)


def _ffn_kernel(xb_ref, xh_ref, xf_ref, wg_ref, wv_ref, cg_ref, cv_ref, wd_ref, g_ref, b_ref,
                of_ref, ob_ref, acc_ref, *, alpha, tiles_per_seq, n_chunks):
    i = pl.program_id(0)
    c = pl.program_id(1)
    halo = xh_ref[...]
    halo = jnp.where(i % tiles_per_seq == 0, jnp.zeros_like(halo), halo)
    xe = jnp.concatenate([halo, xb_ref[...]], axis=0)

    def branch(w_ref, cp_ref):
        u = jnp.dot(xe, w_ref[...], preferred_element_type=F32)
        cp = cp_ref[...]
        y = cp[2:3] * u + cp[1:2] * pltpu.roll(u, 1, 0) + cp[0:1] * pltpu.roll(u, 2, 0) + cp[3:4]
        return y[CONV_HALO:]

    gate = branch(wg_ref, cg_ref)
    val = branch(wv_ref, cv_ref)
    act = (gate * jax.nn.sigmoid(gate) * val).astype(BF16)
    part = jnp.dot(act, wd_ref[...], preferred_element_type=F32)

    @pl.when(c == 0)
    def _():
        acc_ref[...] = part

    @pl.when(c > 0)
    def _():
        acc_ref[...] += part

    @pl.when(c == n_chunks - 1)
    def _():
        y = _layernorm(alpha * xf_ref[...] + acc_ref[...], g_ref[...], b_ref[...])
        of_ref[...] = y
        ob_ref[...] = y.astype(ob_ref.dtype)


def _ffn(xb, xf, w_up, conv_p, w_down, g, b, alpha, seq, tm=512):
    m, d = xf.shape
    n_chunks = D_FF_PAD // FF_CHUNK
    halo_blocks = tm // CONV_HALO
    row_spec = pl.BlockSpec((tm, d), lambda i, c: (i, 0))
    vec_spec = pl.BlockSpec((1, d), lambda i, c: (0, 0))
    return pl.pallas_call(
        functools.partial(_ffn_kernel, alpha=alpha, tiles_per_seq=seq // tm, n_chunks=n_chunks),
        grid=(m // tm, n_chunks),
        in_specs=[row_spec,
                  pl.BlockSpec((CONV_HALO, d), lambda i, c: (jnp.maximum(i * halo_blocks - 1, 0), 0)),
                  row_spec,
                  pl.BlockSpec((d, FF_CHUNK), lambda i, c: (0, c)),
                  pl.BlockSpec((d, FF_CHUNK), lambda i, c: (0, n_chunks + c)),
                  pl.BlockSpec((8, FF_CHUNK), lambda i, c: (0, c)),
                  pl.BlockSpec((8, FF_CHUNK), lambda i, c: (0, n_chunks + c)),
                  pl.BlockSpec((FF_CHUNK, d), lambda i, c: (c, 0)),
                  vec_spec, vec_spec],
        out_specs=(row_spec, row_spec),
        out_shape=(jax.ShapeDtypeStruct((m, d), F32), jax.ShapeDtypeStruct((m, d), BF16)),
        scratch_shapes=[pltpu.VMEM((tm, d), F32)],
        compiler_params=_params("arbitrary", "arbitrary"),
        name="conv_ffn_ln",
    )(xb, xb, xf, w_up, w_up, conv_p, conv_p, w_down, g, b)


def _swap_halves(w):
    half = w.shape[-1] // 2
    return jnp.concatenate([-w[..., half:], w[..., :half]], axis=-1)


def _prep_weights(w_in, mla_w_uq, mla_w_ukv, w_up, conv_w, conv_b, w_down):
    depth, d, _ = w_in.shape
    gw = GROUP_WIDTH
    fox0 = 0
    diff0 = 3 * gw + GROUP_HEADS
    moba0 = diff0 + 3 * gw
    mla0 = moba0 + 3 * gw
    sm = HEAD_DIM ** -0.5 * LOG2E
    sd = DIFF_QK_DIM ** -0.5 * LOG2E
    w_qk = jnp.concatenate([
        w_in[:, :, fox0:fox0 + gw] * sm, w_in[:, :, fox0 + gw:fox0 + 2 * gw],
        w_in[:, :, diff0:diff0 + gw] * sd, w_in[:, :, diff0 + gw:diff0 + 2 * gw],
        w_in[:, :, moba0:moba0 + gw] * sm, w_in[:, :, moba0 + gw:moba0 + 2 * gw]], axis=-1).astype(BF16)
    w_t = jnp.swapaxes(jnp.concatenate([
        w_in[:, :, fox0 + 2 * gw:fox0 + 3 * gw], w_in[:, :, diff0 + 2 * gw:diff0 + 3 * gw],
        w_in[:, :, moba0 + 2 * gw:moba0 + 3 * gw], w_in[:, :, 3 * gw:3 * gw + GROUP_HEADS],
        jnp.zeros((depth, d, 8 - GROUP_HEADS), F32)], axis=-1), 1, 2)
    w_vt, w_gt = w_t[:, :3 * gw].astype(BF16), w_t[:, 3 * gw:].astype(BF16)
    kr0 = mla0 + MLA_Q_LORA + MLA_KV_LORA
    w_kr = w_in[:, :, kr0:kr0 + MLA_ROPE]
    w_lat = jnp.concatenate([w_in[:, :, mla0:kr0], w_kr, _swap_halves(w_kr),
                             jnp.zeros((depth, d, LANES - 2 * MLA_ROPE), F32)], axis=-1).astype(BF16)

    uq = mla_w_uq.reshape(depth, MLA_Q_LORA, GROUP_HEADS, MLA_NOPE + MLA_ROPE)
    pad = LANES - MLA_NOPE - MLA_ROPE
    wqa = jnp.pad(uq, ((0, 0), (0, 0), (0, 0), (0, pad)))
    wqb = jnp.concatenate([jnp.zeros_like(uq[..., :MLA_NOPE]), _swap_halves(uq[..., MLA_NOPE:]),
                           jnp.zeros(uq.shape[:-1] + (pad,), F32)], axis=-1)
    ukv = mla_w_ukv.reshape(depth, MLA_KV_LORA, GROUP_HEADS, MLA_NOPE + HEAD_DIM)
    wk = jnp.pad(ukv[..., :MLA_NOPE], ((0, 0), (0, 0), (0, 0), (0, LANES - MLA_NOPE)))
    flat = lambda a: a.reshape(depth, a.shape[1], -1).astype(BF16)
    wvt = jnp.swapaxes(flat(ukv[..., MLA_NOPE:]), 1, 2)

    ffp = D_FF_PAD - D_FF
    padc = lambda a: jnp.pad(a, ((0, 0), (0, 0), (0, ffp)))
    w_up_p = jnp.concatenate([padc(w_up[:, :, :D_FF]), padc(w_up[:, :, D_FF:])], axis=-1).astype(BF16)
    conv = jnp.concatenate([conv_w, conv_b[:, None, :],
                            jnp.zeros((depth, 8 - CONV_WIDTH - 1, 2 * D_FF), F32)], axis=1)
    conv_p = jnp.concatenate([padc(conv[:, :, :D_FF]), padc(conv[:, :, D_FF:])], axis=-1)
    w_down_p = jnp.pad(w_down, ((0, 0), (0, ffp), (0, 0))).astype(BF16)
    return w_qk, w_vt, w_gt, w_lat, flat(wqa), flat(wqb), flat(wk), wvt, w_up_p, conv_p, w_down_p


def _rope_constants(seq):
    inv = ROPE_THETA ** (-jnp.arange(0, MLA_ROPE // 2, dtype=F32) * 2.0 / MLA_ROPE)
    ang = jnp.arange(seq, dtype=F32)[:, None] * inv[None, :]
    cos, sin = jnp.cos(ang), jnp.sin(ang)
    cos2 = jnp.concatenate([cos, cos], axis=1)
    sin2 = jnp.concatenate([sin, sin], axis=1)
    pad = jnp.zeros((seq, LANES - MLA_NOPE - MLA_ROPE), F32)
    tab_q = jnp.concatenate([jnp.ones((seq, MLA_NOPE), F32), cos2, pad,
                             jnp.zeros((seq, MLA_NOPE), F32), sin2, pad], axis=1)
    tab_k = jnp.concatenate([cos2, sin2, jnp.zeros((seq, LANES - 2 * MLA_ROPE), F32)], axis=1)
    r = jnp.arange(LANES)[:, None]
    c = jnp.arange(GROUP_HEADS * LANES)[None, :]
    e_mat = ((r < 2 * MLA_ROPE) & (c % LANES == MLA_NOPE + r % MLA_ROPE)).astype(BF16)
    return tab_q, tab_k, e_mat


def kernel(x, w_in, b_forget, diff_lambda, diff_subln, mla_q_norm, mla_kv_norm, mla_w_uq, mla_w_ukv,
           rel_bias, w_o, ln1_g, ln1_b, w_up, conv_w, conv_b, w_down, ln2_g, ln2_b):
    batch, seq, d = x.shape
    depth = w_in.shape[0]
    alpha = (2 * depth) ** 0.25
    assert seq % ATTN_TILE == 0 and ATTN_TILE % MOBA_BLOCK == 0 and seq // MOBA_BLOCK <= LANES

    (w_qk, w_vt, w_gt, w_lat, wqa, wqb, wk, wvt, w_up_p, conv_p, w_down_p) = _prep_weights(
        w_in, mla_w_uq, mla_w_ukv, w_up, conv_w, conv_b, w_down)
    w_o_b = w_o.astype(BF16)
    tab_q, tab_k, e_mat = _rope_constants(seq)
    bias_d, bias_p = _bias_blocks(rel_bias, BIAS_BLOCK)
    b_col = jnp.pad(b_forget, ((0, 0), (0, 8 - GROUP_HEADS)))[:, :, None]
    gain = jnp.tile(diff_subln, (1, GROUP_HEADS))[:, None, :]

    xf = x.reshape(batch * seq, d)
    xb = xf.astype(BF16)
    for l in range(depth):
        h_bf = _matmul(xb, w_qk[l], BF16, tm=512, tn=3 * GROUP_WIDTH)
        v_t = _matmul_nt(w_vt[l], xb, BF16, tm=512)
        c_lat = _matmul(xb, w_lat[l], F32, tm=512, tn=w_lat.shape[-1])
        cum = _fox_cumlog(_matmul_nt(w_gt[l], xb, F32, tm=1024), b_col[l], batch, seq)
        fox_o = _fox_attention(h_bf, v_t, cum.T, batch, seq, 0, 0)
        lam_init = 0.8 - 0.6 * math.exp(-0.3 * l)
        lam_c = jnp.concatenate([jnp.full((1, LANES), lam_init, F32),
                                 jnp.full((1, LANES), 1.0 - lam_init, F32),
                                 jnp.zeros((6, LANES), F32)], axis=0)
        diff_o = _diff_attention(h_bf, v_t, bias_d, bias_p, diff_lambda[l], lam_c, gain[l], batch, seq, 2, 1)
        moba_o = _moba_attention(h_bf, v_t, bias_d, bias_p, batch, seq, 4, 2)
        q_m, k_m, v_m = _mla_prep(c_lat, mla_q_norm[l][None], mla_kv_norm[l][None],
                                  wqa[l], wqb[l], wk[l], wvt[l], e_mat, tab_q, tab_k, seq)
        mla_o = _mla_attention(q_m, k_m, v_m, batch, seq)
        xf, xb = _out_proj((fox_o, diff_o, moba_o, mla_o), w_o_b[l], xf, ln1_g[l][None], ln1_b[l][None], alpha)
        xf, xb = _ffn(xb, xf, w_up_p[l], conv_p[l], w_down_p[l], ln2_g[l][None], ln2_b[l][None], alpha, seq)
    return xf.reshape(batch, seq, d)
```

```python
import functools
import math

import jax
import jax.numpy as jnp
from jax import lax
from jax.experimental import pallas as pl
from jax.experimental.pallas import tpu as pltpu

F32 = jnp.float32
BF16 = jnp.bfloat16

HEAD_DIM = 64
GROUP_HEADS = 4
GROUP_WIDTH = GROUP_HEADS * HEAD_DIM
LANES = 128
DIFF_QK_DIM = HEAD_DIM // 2
DIFF_SUBLN_EPS = 1e-5
MOBA_BLOCK = 256
MOBA_TOPK = 3
MLA_Q_LORA = 256
MLA_KV_LORA = 128
MLA_NOPE = 64
MLA_ROPE = 32
ROPE_THETA = 10000.0
REL_BUCKETS = 32
REL_MAX_DIST = 128
D_FF = 2752
FF_CHUNK = 256
D_FF_PAD = -(-D_FF // FF_CHUNK) * FF_CHUNK
CONV_WIDTH = 3
CONV_HALO = 16
LN_EPS = 1e-5
RMS_EPS = 1e-6
NEG = -1e30
LOG2E = 1.4426950408889634

BIAS_BLOCK = 128
ATTN_TILE = 512
SCORE_LOOKAHEAD = 2
VMEM_LIMIT = 56 * 1024 * 1024

_NT = (((1,), (1,)), ((), ()))


def _params(*sem):
    return pltpu.CompilerParams(dimension_semantics=sem, vmem_limit_bytes=VMEM_LIMIT)


def _mm_kernel(x_ref, w_ref, o_ref):
    o_ref[...] = jnp.dot(x_ref[...], w_ref[...], preferred_element_type=F32).astype(o_ref.dtype)


def _matmul(x, w, out_dtype, tm, tn):
    m, k = x.shape
    n = w.shape[1]
    return pl.pallas_call(
        _mm_kernel,
        grid=(m // tm, n // tn),
        in_specs=[pl.BlockSpec((tm, k), lambda i, j: (i, 0)),
                  pl.BlockSpec((k, tn), lambda i, j: (0, j))],
        out_specs=pl.BlockSpec((tm, tn), lambda i, j: (i, j)),
        out_shape=jax.ShapeDtypeStruct((m, n), out_dtype),
        compiler_params=_params("arbitrary", "arbitrary"),
        name="in_proj",
    )(x, w)


def _mm_nt_kernel(w_ref, x_ref, o_ref):
    o_ref[...] = lax.dot_general(w_ref[...], x_ref[...], _NT, preferred_element_type=F32).astype(o_ref.dtype)


def _matmul_nt(w_t, x, out_dtype, tm):
    n, k = w_t.shape
    m = x.shape[0]
    return pl.pallas_call(
        _mm_nt_kernel,
        grid=(m // tm,),
        in_specs=[pl.BlockSpec((n, k), lambda i: (0, 0)),
                  pl.BlockSpec((tm, k), lambda i: (i, 0))],
        out_specs=pl.BlockSpec((n, tm), lambda i: (0, i)),
        out_shape=jax.ShapeDtypeStruct((n, m), out_dtype),
        compiler_params=_params("arbitrary"),
        name="in_proj_t",
    )(w_t, x)


def _cum_kernel(g_ref, b_ref, o_ref):
    x = g_ref[...] + b_ref[...]
    x = jnp.minimum(x, 0.0) - jnp.log(1.0 + jnp.exp(-jnp.abs(x)))
    n = x.shape[1]
    lane = lax.broadcasted_iota(jnp.int32, x.shape, 1)
    k = 1
    while k < n:
        x = x + jnp.where(lane >= k, pltpu.roll(x, k, 1), 0.0)
        k *= 2
    o_ref[...] = x * LOG2E


def _fox_cumlog(logits_t, b_col, batch, seq):
    return pl.pallas_call(
        _cum_kernel,
        grid=(batch,),
        in_specs=[pl.BlockSpec((8, seq), lambda b: (0, b)),
                  pl.BlockSpec((8, 1), lambda b: (0, 0))],
        out_specs=pl.BlockSpec((8, seq), lambda b: (0, b)),
        out_shape=jax.ShapeDtypeStruct(logits_t.shape, F32),
        compiler_params=_params("arbitrary"),
        name="fox_cumlog",
    )(logits_t, b_col)


def _bias_kernel(rb_ref, d_ref, p_ref, *, t):
    h = pl.program_id(0)
    i = lax.broadcasted_iota(jnp.int32, (t, t), 0)
    j = lax.broadcasted_iota(jnp.int32, (t, t), 1)
    max_exact = REL_BUCKETS // 2
    far = rb_ref[(REL_BUCKETS - 1) * 8 + h]
    for out_ref, off in ((d_ref, 0), (p_ref, t)):
        dist = j - i + off
        d_large = jnp.maximum(dist, max_exact).astype(F32)
        large = max_exact + (jnp.log(d_large / max_exact) / math.log(REL_MAX_DIST / max_exact)
                             * (REL_BUCKETS - max_exact)).astype(jnp.int32)
        large = jnp.minimum(large, REL_BUCKETS - 1)
        bucket = jnp.where(dist < max_exact, dist, large)
        acc = jnp.zeros((t, t), F32)
        for bkt in range(REL_BUCKETS - 1):
            acc = jnp.where(bucket == bkt, (rb_ref[bkt * 8 + h] - far) * LOG2E, acc)
        if off == 0:
            acc = jnp.where(dist >= 0, acc, NEG)
        out_ref[0] = acc


def _bias_blocks(rel_bias, t):
    n_heads = rel_bias.shape[1]
    shape = jax.ShapeDtypeStruct((n_heads, t, t), F32)
    spec = pl.BlockSpec((1, t, t), lambda h: (h, 0, 0))
    return pl.pallas_call(
        functools.partial(_bias_kernel, t=t),
        grid=(n_heads,),
        in_specs=[pl.BlockSpec(memory_space=pltpu.SMEM)],
        out_specs=(spec, spec),
        out_shape=(shape, shape),
        compiler_params=_params("arbitrary"),
        name="rel_bias_blocks",
    )(rel_bias.reshape(-1))


def _online_update(s_t, v_t, carry):
    m, l, acc = carry
    m_new = jnp.maximum(m, jnp.max(s_t, axis=0, keepdims=True))
    p = jnp.exp2(s_t - m_new)
    alpha = jnp.exp2(m - m_new)
    l = alpha * l + jnp.sum(p, axis=0, keepdims=True)
    acc = alpha * acc + jnp.dot(v_t, p.astype(BF16), preferred_element_type=F32)
    return m_new, l, acc


def _sweep(n, t, qi, raw, fix, values, near_bias, s_ref):
    la = s_ref.shape[0]

    def step(j, carries, kind):
        ahead = {}
        out = []
        for c in range(n):
            if c + la < n:
                ahead[c + la] = raw(c + la, j)
            elif kind != "diag":
                s_ref[c + la - n] = raw(c + la - n, j + 1)
            s_t = s_ref[c] if c < la else ahead.pop(c)
            out.append(_online_update(fix(c, s_t, j, kind), values(c, j), carries[c]))
        return tuple(out)

    for c in range(la):
        s_ref[c] = raw(c, 0)
    carries = tuple(_init_carry(t) for _ in range(n))
    if near_bias:
        carries = lax.fori_loop(0, jnp.maximum(qi - 1, 0), lambda j, s: step(j, s, "far"), carries)
        carries = lax.cond(qi >= 1, lambda s: step(qi - 1, s, "near"), lambda s: s, carries)
    else:
        carries = lax.fori_loop(0, qi, lambda j, s: step(j, s, "far"), carries)
    return step(qi, carries, "diag")


def _init_carry(t):
    return (jnp.full((1, t), NEG, F32), jnp.zeros((1, t), F32), jnp.zeros((LANES, t), F32))


def _lane_select(x, lo, hi):
    lane = lax.broadcasted_iota(jnp.int32, x.shape, 1)
    return jnp.where((lane >= lo) & (lane < hi), x, jnp.zeros_like(x))


def _merge_pair(a, b):
    row = lax.broadcasted_iota(jnp.int32, a.shape, 0)
    return jnp.where(row < HEAD_DIM, a, b)


def _causal_t(t):
    return (lax.broadcasted_iota(jnp.int32, (t, t), 0) <= lax.broadcasted_iota(jnp.int32, (t, t), 1))


def _band_bias(s_t, d_t, p_t, diag):
    n = s_t.shape[0] // BIAS_BLOCK
    rows = []
    for b in range(n):
        blocks = []
        for a in range(n):
            blk = s_t[b * BIAS_BLOCK:(b + 1) * BIAS_BLOCK, a * BIAS_BLOCK:(a + 1) * BIAS_BLOCK]
            if diag and b > a:
                blk = jnp.full_like(blk, NEG)
            elif diag and b == a:
                blk = blk + d_t
            elif (diag and b == a - 1) or (not diag and b == n - 1 and a == 0):
                blk = blk + p_t
            blocks.append(blk)
        rows.append(jnp.concatenate(blocks, axis=1))
    return jnp.concatenate(rows, axis=0)


def _attn_specs(t, seq, nq, cb, vrow):
    w = GROUP_WIDTH
    return [pl.BlockSpec((t, w), lambda b, i: (b * nq + i, cb)),
            pl.BlockSpec((seq, w), lambda b, i: (b, cb + 1)),
            pl.BlockSpec((w, seq), lambda b, i: (vrow, b))]


def _attn_call(kernel_fn, name, batch, seq, t, in_specs, args, scratch=()):
    nq = seq // t
    return pl.pallas_call(
        kernel_fn,
        grid=(batch, nq),
        in_specs=in_specs,
        out_specs=pl.BlockSpec((t, GROUP_WIDTH), lambda b, i: (b * nq + i, 0)),
        out_shape=jax.ShapeDtypeStruct((batch * seq, GROUP_WIDTH), BF16),
        scratch_shapes=[pltpu.VMEM((SCORE_LOOKAHEAD, t, t), F32)] + list(scratch),
        compiler_params=_params("arbitrary", "arbitrary"),
        name=name,
    )(*args)


def _head_queries(q_ref, width):
    out = []
    for lo in range(0, GROUP_WIDTH, width):
        p = lo // LANES
        out.append(_lane_select(q_ref[:, p * LANES:(p + 1) * LANES], lo - p * LANES, lo - p * LANES + width))
    return out


def _fox_kernel(q_ref, k_ref, vt_ref, c_ref, o_ref, s_ref, *, t):
    qi = pl.program_id(1)
    causal = _causal_t(t)
    qms = _head_queries(q_ref, HEAD_DIM)

    def raw(h, j):
        sl = slice((h // 2) * LANES, (h // 2 + 1) * LANES)
        return lax.dot_general(k_ref[pl.ds(pl.multiple_of(j * t, t), t), sl], qms[h], _NT,
                               preferred_element_type=F32)

    def fix(h, s_t, j, kind):
        s_t = s_t - c_ref[pl.ds(pl.multiple_of(j * t, t), t), h:h + 1]
        return jnp.where(causal, s_t, NEG) if kind == "diag" else s_t

    def values(h, j):
        return vt_ref[(h // 2) * LANES:(h // 2 + 1) * LANES, pl.ds(pl.multiple_of(j * t, t), t)]

    carries = _sweep(GROUP_HEADS, t, qi, raw, fix, values, False, s_ref)
    outs = [acc / l for (_, l, acc) in carries]
    for p in range(GROUP_HEADS // 2):
        o_t = _merge_pair(outs[2 * p], outs[2 * p + 1])
        o_ref[:, p * LANES:(p + 1) * LANES] = o_t.T.astype(o_ref.dtype)


def _fox_attention(h_bf, v_t, cum_cols, batch, seq, cb, vrow, t=ATTN_TILE):
    nq = seq // t
    specs = _attn_specs(t, seq, nq, cb, vrow) + [pl.BlockSpec((seq, 8), lambda b, i: (b, 0))]
    return _attn_call(functools.partial(_fox_kernel, t=t), "fox_attn", batch, seq, t, specs,
                      (h_bf, h_bf, v_t, cum_cols))


def _diff_kernel(q_ref, k_ref, vt_ref, bd_ref, bp_ref, lam_ref, lc_ref, g_ref, o_ref, s_ref, *, t):
    qi = pl.program_id(1)
    lam_p = lam_ref[...]
    lam = (jnp.exp(jnp.sum(lam_p[0:1] * lam_p[1:2], axis=-1, keepdims=True))
           - jnp.exp(jnp.sum(lam_p[2:3] * lam_p[3:4], axis=-1, keepdims=True)) + lc_ref[0:1, 0:1])
    qms = _head_queries(q_ref, DIFF_QK_DIM)

    def raw(c, j):
        sl = slice((c // 4) * LANES, (c // 4 + 1) * LANES)
        return lax.dot_general(k_ref[pl.ds(pl.multiple_of(j * t, t), t), sl], qms[c], _NT,
                               preferred_element_type=F32)

    def fix(c, s_t, j, kind):
        return s_t if kind == "far" else _band_bias(s_t, bd_ref[c // 2], bp_ref[c // 2], kind == "diag")

    def values(c, j):
        return vt_ref[(c // 4) * LANES:(c // 4 + 1) * LANES, pl.ds(pl.multiple_of(j * t, t), t)]

    carries = _sweep(2 * GROUP_HEADS, t, qi, raw, fix, values, True, s_ref)
    row = lax.broadcasted_iota(jnp.int32, (LANES, t), 0)
    outs = []
    for h in range(GROUP_HEADS):
        (_, l1, a1), (_, l2, a2) = carries[2 * h], carries[2 * h + 1]
        o = a1 / l1 - lam * (a2 / l2)
        lo = (h % 2) * HEAD_DIM
        o = jnp.where((row >= lo) & (row < lo + HEAD_DIM), o, 0.0)
        ms = jnp.sum(o * o, axis=0, keepdims=True) * (1.0 / HEAD_DIM)
        outs.append(o * lax.rsqrt(ms + DIFF_SUBLN_EPS))
    for p in range(GROUP_HEADS // 2):
        sl = slice(p * LANES, (p + 1) * LANES)
        o_t = _merge_pair(outs[2 * p], outs[2 * p + 1])
        o_ref[:, sl] = (o_t.T * g_ref[:, sl] * lc_ref[1:2, 0:1]).astype(o_ref.dtype)


def _diff_attention(h_bf, v_t, bias_d, bias_p, lam_p, lam_c, gain, batch, seq, cb, vrow, t=ATTN_TILE):
    nq = seq // t
    bias_spec = pl.BlockSpec((GROUP_HEADS, BIAS_BLOCK, BIAS_BLOCK), lambda b, i: (0, 0, 0))
    full = lambda a: pl.BlockSpec(a.shape, lambda b, i: (0, 0))
    specs = _attn_specs(t, seq, nq, cb, vrow) + [bias_spec, bias_spec, full(lam_p), full(lam_c), full(gain)]
    return _attn_call(functools.partial(_diff_kernel, t=t), "diff_attn", batch, seq, t, specs,
                      (h_bf, h_bf, v_t, bias_d, bias_p, lam_p, lam_c, gain))


def _moba_kernel(q_ref, k_ref, vt_ref, bd_ref, bp_ref, o_ref, s_ref, kmean_ref, *, t, n_blocks):
    qi = pl.program_id(1)
    per_tile = t // MOBA_BLOCK

    @pl.when(qi == 0)
    def _():
        kmean_ref[...] = jnp.zeros_like(kmean_ref)
        for blk in range(n_blocks):
            kb = k_ref[blk * MOBA_BLOCK:(blk + 1) * MOBA_BLOCK, :].astype(F32)
            kmean_ref[blk:blk + 1, :] = jnp.sum(kb, axis=0, keepdims=True) * (1.0 / MOBA_BLOCK)

    blk_i = lax.broadcasted_iota(jnp.int32, (t, LANES), 1)
    blk_f = blk_i.astype(F32)
    row_i = lax.broadcasted_iota(jnp.int32, (t, LANES), 0)
    row_blk = sum((row_i >= n * MOBA_BLOCK).astype(jnp.int32) for n in range(1, per_tile))
    own = qi * per_tile + row_blk
    qms = _head_queries(q_ref, HEAD_DIM)
    q_cats = []
    for h in range(GROUP_HEADS):
        sl = slice((h // 2) * LANES, (h // 2 + 1) * LANES)
        gate = lax.dot_general(qms[h].astype(F32), kmean_ref[:, sl], _NT,
                               precision=lax.Precision.HIGHEST, preferred_element_type=F32)
        g = jnp.where(blk_i < own, gate, NEG)
        keep = jnp.where(blk_i == own, 1.0, 0.0)
        for _ in range(MOBA_TOPK):
            mx = jnp.max(g, axis=-1, keepdims=True)
            first = jnp.min(jnp.where(g == mx, blk_f, 1e9), axis=-1, keepdims=True)
            pick = jnp.where(mx > 0.5 * NEG, jnp.where(blk_f == first, 1.0, 0.0), 0.0)
            keep = jnp.maximum(keep, pick)
            g = jnp.where(pick > 0.0, NEG, g)
        q_cats.append(jnp.concatenate([qms[h], (1.0 - keep).astype(BF16)], axis=1))

    def raw(h, j):
        sl = slice((h // 2) * LANES, (h // 2 + 1) * LANES)
        reject = jnp.where(blk_i == j * per_tile + row_blk, NEG, 0.0).astype(BF16)
        k_cat = jnp.concatenate([k_ref[pl.ds(pl.multiple_of(j * t, t), t), sl], reject], axis=1)
        return lax.dot_general(k_cat, q_cats[h], _NT, preferred_element_type=F32)

    def fix(h, s_t, j, kind):
        return s_t if kind == "far" else _band_bias(s_t, bd_ref[h], bp_ref[h], kind == "diag")

    def values(h, j):
        return vt_ref[(h // 2) * LANES:(h // 2 + 1) * LANES, pl.ds(pl.multiple_of(j * t, t), t)]

    carries = _sweep(GROUP_HEADS, t, qi, raw, fix, values, True, s_ref)
    outs = [acc / l for (_, l, acc) in carries]
    for p in range(GROUP_HEADS // 2):
        o_t = _merge_pair(outs[2 * p], outs[2 * p + 1])
        o_ref[:, p * LANES:(p + 1) * LANES] = o_t.T.astype(o_ref.dtype)


def _moba_attention(h_bf, v_t, bias_d, bias_p, batch, seq, cb, vrow, t=ATTN_TILE):
    nq = seq // t
    bias_spec = pl.BlockSpec((GROUP_HEADS, BIAS_BLOCK, BIAS_BLOCK), lambda b, i: (1, 0, 0))
    specs = _attn_specs(t, seq, nq, cb, vrow) + [bias_spec, bias_spec]
    return _attn_call(functools.partial(_moba_kernel, t=t, n_blocks=seq // MOBA_BLOCK), "moba_attn",
                      batch, seq, t, specs, (h_bf, h_bf, v_t, bias_d, bias_p),
                      scratch=[pltpu.VMEM((LANES, GROUP_WIDTH), F32)])


def _mla_prep_kernel(c_ref, gq_ref, gkv_ref, wqa_ref, wqb_ref, wk_ref, wvt_ref, e_ref,
                     tq_ref, tk_ref, q_ref, k_ref, vt_ref):
    c = c_ref[...]
    cq = c[:, :MLA_Q_LORA]
    ckv = c[:, MLA_Q_LORA:MLA_Q_LORA + MLA_KV_LORA]
    kr = c[:, MLA_Q_LORA + MLA_KV_LORA:]
    cq = cq * lax.rsqrt(jnp.mean(cq * cq, axis=-1, keepdims=True) + RMS_EPS) * gq_ref[...]
    ckv = ckv * lax.rsqrt(jnp.mean(ckv * ckv, axis=-1, keepdims=True) + RMS_EPS) * gkv_ref[...]
    cq = cq.astype(BF16)
    ckv = ckv.astype(BF16)
    tq = tq_ref[...]
    cos_q = jnp.concatenate([tq[:, :LANES]] * GROUP_HEADS, axis=1)
    sin_q = jnp.concatenate([tq[:, LANES:]] * GROUP_HEADS, axis=1)
    q = (jnp.dot(cq, wqa_ref[...], preferred_element_type=F32) * cos_q
         + jnp.dot(cq, wqb_ref[...], preferred_element_type=F32) * sin_q)
    q_ref[...] = (q * ((MLA_NOPE + MLA_ROPE) ** -0.5 * LOG2E)).astype(q_ref.dtype)
    k_rope = (kr * tk_ref[...]).astype(BF16)
    k = (jnp.dot(ckv, wk_ref[...], preferred_element_type=F32)
         + jnp.dot(k_rope, e_ref[...], preferred_element_type=F32))
    k_ref[...] = k.astype(k_ref.dtype)
    vt_ref[...] = lax.dot_general(wvt_ref[...], ckv, _NT, preferred_element_type=F32).astype(vt_ref.dtype)


def _mla_prep(c_lat, gq, gkv, wqa, wqb, wk, wvt, e_mat, tab_q, tab_k, seq, tm=512):
    m = c_lat.shape[0]
    ns = seq // tm
    full = lambda a: pl.BlockSpec(a.shape, lambda i: (0, 0))
    slot_w = GROUP_HEADS * LANES
    return pl.pallas_call(
        _mla_prep_kernel,
        grid=(m // tm,),
        in_specs=[pl.BlockSpec((tm, c_lat.shape[1]), lambda i: (i, 0)),
                  full(gq), full(gkv), full(wqa), full(wqb), full(wk), full(wvt), full(e_mat),
                  pl.BlockSpec((tm, tab_q.shape[1]), lambda i: (i % ns, 0)),
                  pl.BlockSpec((tm, tab_k.shape[1]), lambda i: (i % ns, 0))],
        out_specs=(pl.BlockSpec((tm, slot_w), lambda i: (i, 0)),
                   pl.BlockSpec((tm, slot_w), lambda i: (i, 0)),
                   pl.BlockSpec((GROUP_WIDTH, tm), lambda i: (0, i))),
        out_shape=(jax.ShapeDtypeStruct((m, slot_w), BF16),
                   jax.ShapeDtypeStruct((m, slot_w), BF16),
                   jax.ShapeDtypeStruct((GROUP_WIDTH, m), BF16)),
        compiler_params=_params("arbitrary"),
        name="mla_prep",
    )(c_lat, gq, gkv, wqa, wqb, wk, wvt, e_mat, tab_q, tab_k)


def _mla_kernel(q_ref, k_ref, vt_ref, o_ref, s_ref, *, t):
    qi = pl.program_id(1)
    causal = _causal_t(t)

    def raw(h, j):
        hsl = slice(h * LANES, (h + 1) * LANES)
        return lax.dot_general(k_ref[pl.ds(pl.multiple_of(j * t, t), t), hsl], q_ref[:, hsl], _NT,
                               preferred_element_type=F32)

    def fix(h, s_t, j, kind):
        return jnp.where(causal, s_t, NEG) if kind == "diag" else s_t

    def values(h, j):
        return vt_ref[(h // 2) * LANES:(h // 2 + 1) * LANES, pl.ds(pl.multiple_of(j * t, t), t)]

    carries = _sweep(GROUP_HEADS, t, qi, raw, fix, values, False, s_ref)
    outs = [acc / l for (_, l, acc) in carries]
    for p in range(GROUP_HEADS // 2):
        o_t = _merge_pair(outs[2 * p], outs[2 * p + 1])
        o_ref[:, p * LANES:(p + 1) * LANES] = o_t.T.astype(o_ref.dtype)


def _mla_attention(q, k, v_t, batch, seq, t=ATTN_TILE):
    nq = seq // t
    slot_w = q.shape[1]
    specs = [pl.BlockSpec((t, slot_w), lambda b, i: (b * nq + i, 0)),
             pl.BlockSpec((seq, slot_w), lambda b, i: (b, 0)),
             pl.BlockSpec((GROUP_WIDTH, seq), lambda b, i: (0, b))]
    return _attn_call(functools.partial(_mla_kernel, t=t), "mla_attn", batch, seq, t, specs, (q, k, v_t))


def _layernorm(t, g, b):
    mu = jnp.mean(t, axis=-1, keepdims=True)
    d = t - mu
    var = jnp.mean(d * d, axis=-1, keepdims=True)
    return d * lax.rsqrt(var + LN_EPS) * g + b


def _out_kernel(o1_ref, o2_ref, o3_ref, o4_ref, w_ref, x_ref, g_ref, b_ref, of_ref, ob_ref, *, alpha):
    mix = None
    for n, o_ref in enumerate((o1_ref, o2_ref, o3_ref, o4_ref)):
        part = jnp.dot(o_ref[...], w_ref[n * GROUP_WIDTH:(n + 1) * GROUP_WIDTH, :],
                       preferred_element_type=F32)
        mix = part if mix is None else mix + part
    y = _layernorm(alpha * x_ref[...] + mix, g_ref[...], b_ref[...])
    of_ref[...] = y
    ob_ref[...] = y.astype(ob_ref.dtype)


def _out_proj(outs, w_o, x, g, b, alpha, tm=512):
    m, d = x.shape
    o_spec = pl.BlockSpec((tm, GROUP_WIDTH), lambda i: (i, 0))
    row_spec = pl.BlockSpec((tm, d), lambda i: (i, 0))
    vec_spec = pl.BlockSpec((1, d), lambda i: (0, 0))
    return pl.pallas_call(
        functools.partial(_out_kernel, alpha=alpha),
        grid=(m // tm,),
        in_specs=[o_spec, o_spec, o_spec, o_spec,
                  pl.BlockSpec(w_o.shape, lambda i: (0, 0)), row_spec, vec_spec, vec_spec],
        out_specs=(row_spec, row_spec),
        out_shape=(jax.ShapeDtypeStruct((m, d), F32), jax.ShapeDtypeStruct((m, d), BF16)),
        compiler_params=_params("arbitrary"),
        name="out_proj_ln",
    )(*outs, w_o, x, g, b)


def _ffn_kernel(xb_ref, xh_ref, xf_ref, wup_ref, cp_ref, wd_ref, g_ref, b_ref,
                of_ref, ob_ref, *, alpha, tiles_per_seq, n_chunks):
    i = pl.program_id(0)
    halo = xh_ref[...]
    halo = jnp.where(i % tiles_per_seq == 0, jnp.zeros_like(halo), halo)
    xe = jnp.concatenate([halo, xb_ref[...]], axis=0)

    def cols(c, br):
        return slice(br * D_FF_PAD + c * FF_CHUNK, br * D_FF_PAD + (c + 1) * FF_CHUNK)

    def project(c):
        return [jnp.dot(xe, wup_ref[:, cols(c, br)], preferred_element_type=F32) for br in range(2)]

    def gated(c, us):
        ys = []
        for br, u in enumerate(us):
            cp = cp_ref[:, cols(c, br)]
            y = cp[2:3] * u + cp[1:2] * pltpu.roll(u, 1, 0) + cp[0:1] * pltpu.roll(u, 2, 0) + cp[3:4]
            ys.append(y[CONV_HALO:])
        return (ys[0] * jax.nn.sigmoid(ys[0]) * ys[1]).astype(BF16)

    acc = None
    u_next = project(0)
    for c in range(n_chunks):
        us = u_next
        if c + 1 < n_chunks:
            u_next = project(c + 1)
        part = jnp.dot(gated(c, us), wd_ref[c * FF_CHUNK:(c + 1) * FF_CHUNK, :], preferred_element_type=F32)
        acc = part if acc is None else acc + part
    y = _layernorm(alpha * xf_ref[...] + acc, g_ref[...], b_ref[...])
    of_ref[...] = y
    ob_ref[...] = y.astype(ob_ref.dtype)


def _ffn(xb, xf, w_up, conv_p, w_down, g, b, alpha, seq, tm=256):
    m, d = xf.shape
    n_chunks = D_FF_PAD // FF_CHUNK
    halo_blocks = tm // CONV_HALO
    row_spec = pl.BlockSpec((tm, d), lambda i: (i, 0))
    vec_spec = pl.BlockSpec((1, d), lambda i: (0, 0))
    resident = lambda a: pl.BlockSpec(a.shape, lambda i: (0, 0), pipeline_mode=pl.Buffered(1))
    return pl.pallas_call(
        functools.partial(_ffn_kernel, alpha=alpha, tiles_per_seq=seq // tm, n_chunks=n_chunks),
        grid=(m // tm,),
        in_specs=[row_spec,
                  pl.BlockSpec((CONV_HALO, d), lambda i: (jnp.maximum(i * halo_blocks - 1, 0), 0)),
                  row_spec, resident(w_up), resident(conv_p), resident(w_down), vec_spec, vec_spec],
        out_specs=(row_spec, row_spec),
        out_shape=(jax.ShapeDtypeStruct((m, d), F32), jax.ShapeDtypeStruct((m, d), BF16)),
        compiler_params=_params("arbitrary"),
        name="conv_ffn_ln",
    )(xb, xb, xf, w_up, conv_p, w_down, g, b)


def _swap_halves(w):
    half = w.shape[-1] // 2
    return jnp.concatenate([-w[..., half:], w[..., :half]], axis=-1)


def _prep_weights(w_in, mla_w_uq, mla_w_ukv, w_up, conv_w, conv_b, w_down):
    depth, d, _ = w_in.shape
    gw = GROUP_WIDTH
    fox0 = 0
    diff0 = 3 * gw + GROUP_HEADS
    moba0 = diff0 + 3 * gw
    mla0 = moba0 + 3 * gw
    sm = HEAD_DIM ** -0.5 * LOG2E
    sd = DIFF_QK_DIM ** -0.5 * LOG2E
    w_qk = jnp.concatenate([
        w_in[:, :, fox0:fox0 + gw] * sm, w_in[:, :, fox0 + gw:fox0 + 2 * gw],
        w_in[:, :, diff0:diff0 + gw] * sd, w_in[:, :, diff0 + gw:diff0 + 2 * gw],
        w_in[:, :, moba0:moba0 + gw] * sm, w_in[:, :, moba0 + gw:moba0 + 2 * gw]], axis=-1).astype(BF16)
    w_t = jnp.swapaxes(jnp.concatenate([
        w_in[:, :, fox0 + 2 * gw:fox0 + 3 * gw], w_in[:, :, diff0 + 2 * gw:diff0 + 3 * gw],
        w_in[:, :, moba0 + 2 * gw:moba0 + 3 * gw], w_in[:, :, 3 * gw:3 * gw + GROUP_HEADS],
        jnp.zeros((depth, d, 8 - GROUP_HEADS), F32)], axis=-1), 1, 2)
    w_vt, w_gt = w_t[:, :3 * gw].astype(BF16), w_t[:, 3 * gw:].astype(BF16)
    kr0 = mla0 + MLA_Q_LORA + MLA_KV_LORA
    w_kr = w_in[:, :, kr0:kr0 + MLA_ROPE]
    w_lat = jnp.concatenate([w_in[:, :, mla0:kr0], w_kr, _swap_halves(w_kr),
                             jnp.zeros((depth, d, LANES - 2 * MLA_ROPE), F32)], axis=-1).astype(BF16)

    uq = mla_w_uq.reshape(depth, MLA_Q_LORA, GROUP_HEADS, MLA_NOPE + MLA_ROPE)
    pad = LANES - MLA_NOPE - MLA_ROPE
    wqa = jnp.pad(uq, ((0, 0), (0, 0), (0, 0), (0, pad)))
    wqb = jnp.concatenate([jnp.zeros_like(uq[..., :MLA_NOPE]), _swap_halves(uq[..., MLA_NOPE:]),
                           jnp.zeros(uq.shape[:-1] + (pad,), F32)], axis=-1)
    ukv = mla_w_ukv.reshape(depth, MLA_KV_LORA, GROUP_HEADS, MLA_NOPE + HEAD_DIM)
    wk = jnp.pad(ukv[..., :MLA_NOPE], ((0, 0), (0, 0), (0, 0), (0, LANES - MLA_NOPE)))
    flat = lambda a: a.reshape(depth, a.shape[1], -1).astype(BF16)
    wvt = jnp.swapaxes(flat(ukv[..., MLA_NOPE:]), 1, 2)

    ffp = D_FF_PAD - D_FF
    padc = lambda a: jnp.pad(a, ((0, 0), (0, 0), (0, ffp)))
    w_up_p = jnp.concatenate([padc(w_up[:, :, :D_FF]), padc(w_up[:, :, D_FF:])], axis=-1).astype(BF16)
    conv = jnp.concatenate([conv_w, conv_b[:, None, :],
                            jnp.zeros((depth, 8 - CONV_WIDTH - 1, 2 * D_FF), F32)], axis=1)
    conv_p = jnp.concatenate([padc(conv[:, :, :D_FF]), padc(conv[:, :, D_FF:])], axis=-1)
    w_down_p = jnp.pad(w_down, ((0, 0), (0, ffp), (0, 0))).astype(BF16)
    return w_qk, w_vt, w_gt, w_lat, flat(wqa), flat(wqb), flat(wk), wvt, w_up_p, conv_p, w_down_p


def _rope_constants(seq):
    inv = ROPE_THETA ** (-jnp.arange(0, MLA_ROPE // 2, dtype=F32) * 2.0 / MLA_ROPE)
    ang = jnp.arange(seq, dtype=F32)[:, None] * inv[None, :]
    cos, sin = jnp.cos(ang), jnp.sin(ang)
    cos2 = jnp.concatenate([cos, cos], axis=1)
    sin2 = jnp.concatenate([sin, sin], axis=1)
    pad = jnp.zeros((seq, LANES - MLA_NOPE - MLA_ROPE), F32)
    tab_q = jnp.concatenate([jnp.ones((seq, MLA_NOPE), F32), cos2, pad,
                             jnp.zeros((seq, MLA_NOPE), F32), sin2, pad], axis=1)
    tab_k = jnp.concatenate([cos2, sin2, jnp.zeros((seq, LANES - 2 * MLA_ROPE), F32)], axis=1)
    r = jnp.arange(LANES)[:, None]
    c = jnp.arange(GROUP_HEADS * LANES)[None, :]
    e_mat = ((r < 2 * MLA_ROPE) & (c % LANES == MLA_NOPE + r % MLA_ROPE)).astype(BF16)
    return tab_q, tab_k, e_mat


def kernel(x, w_in, b_forget, diff_lambda, diff_subln, mla_q_norm, mla_kv_norm, mla_w_uq, mla_w_ukv,
           rel_bias, w_o, ln1_g, ln1_b, w_up, conv_w, conv_b, w_down, ln2_g, ln2_b):
    batch, seq, d = x.shape
    depth = w_in.shape[0]
    alpha = (2 * depth) ** 0.25
    assert seq % ATTN_TILE == 0 and ATTN_TILE % MOBA_BLOCK == 0 and seq // MOBA_BLOCK <= LANES

    (w_qk, w_vt, w_gt, w_lat, wqa, wqb, wk, wvt, w_up_p, conv_p, w_down_p) = _prep_weights(
        w_in, mla_w_uq, mla_w_ukv, w_up, conv_w, conv_b, w_down)
    w_o_b = w_o.astype(BF16)
    tab_q, tab_k, e_mat = _rope_constants(seq)
    bias_d, bias_p = _bias_blocks(rel_bias, BIAS_BLOCK)
    b_col = jnp.pad(b_forget, ((0, 0), (0, 8 - GROUP_HEADS)))[:, :, None]
    gain = jnp.tile(diff_subln, (1, GROUP_HEADS))[:, None, :]

    xf = x.reshape(batch * seq, d)
    xb = xf.astype(BF16)
    for l in range(depth):
        h_bf = _matmul(xb, w_qk[l], BF16, tm=512, tn=3 * GROUP_WIDTH)
        v_t = _matmul_nt(w_vt[l], xb, BF16, tm=512)
        c_lat = _matmul(xb, w_lat[l], F32, tm=512, tn=w_lat.shape[-1])
        cum = _fox_cumlog(_matmul_nt(w_gt[l], xb, F32, tm=1024), b_col[l], batch, seq)
        fox_o = _fox_attention(h_bf, v_t, cum.T, batch, seq, 0, 0)
        lam_init = 0.8 - 0.6 * math.exp(-0.3 * l)
        lam_c = jnp.concatenate([jnp.full((1, LANES), lam_init, F32),
                                 jnp.full((1, LANES), 1.0 - lam_init, F32),
                                 jnp.zeros((6, LANES), F32)], axis=0)
        diff_o = _diff_attention(h_bf, v_t, bias_d, bias_p, diff_lambda[l], lam_c, gain[l], batch, seq, 2, 1)
        moba_o = _moba_attention(h_bf, v_t, bias_d, bias_p, batch, seq, 4, 2)
        q_m, k_m, v_m = _mla_prep(c_lat, mla_q_norm[l][None], mla_kv_norm[l][None],
                                  wqa[l], wqb[l], wk[l], wvt[l], e_mat, tab_q, tab_k, seq)
        mla_o = _mla_attention(q_m, k_m, v_m, batch, seq)
        xf, xb = _out_proj((fox_o, diff_o, moba_o, mla_o), w_o_b[l], xf, ln1_g[l][None], ln1_b[l][None], alpha)
        xf, xb = _ffn(xb, xf, w_up_p[l], conv_p[l], w_down_p[l], ln2_g[l][None], ln2_b[l][None], alpha, seq)
    return xf.reshape(batch, seq, d)
```

```python
import functools
import math

import jax
import jax.numpy as jnp
from jax import lax
from jax.experimental import pallas as pl
from jax.experimental.pallas import tpu as pltpu

F32 = jnp.float32
BF16 = jnp.bfloat16

HEAD_DIM = 64
GROUP_HEADS = 4
GROUP_WIDTH = GROUP_HEADS * HEAD_DIM
LANES = 128
DIFF_QK_DIM = HEAD_DIM // 2
DIFF_SUBLN_EPS = 1e-5
MOBA_BLOCK = 256
MOBA_TOPK = 3
MLA_Q_LORA = 256
MLA_KV_LORA = 128
MLA_NOPE = 64
MLA_ROPE = 32
ROPE_THETA = 10000.0
REL_BUCKETS = 32
REL_MAX_DIST = 128
D_FF = 2752
FF_CHUNK = 256
D_FF_PAD = -(-D_FF // FF_CHUNK) * FF_CHUNK
CONV_WIDTH = 3
CONV_HALO = 16
LN_EPS = 1e-5
RMS_EPS = 1e-6
NEG = -1e30
LOG2E = 1.4426950408889634

BIAS_BLOCK = 128
ATTN_TILE = 512
V_ROWS = LANES + 16
V_SLAB = (GROUP_HEADS // 2) * V_ROWS
SCORE_LOOKAHEAD = 2
VMEM_LIMIT = 56 * 1024 * 1024

_NT = (((1,), (1,)), ((), ()))


def _params(*sem):
    return pltpu.CompilerParams(dimension_semantics=sem, vmem_limit_bytes=VMEM_LIMIT)


def _mm_kernel(x_ref, w_ref, o_ref):
    o_ref[...] = jnp.dot(x_ref[...], w_ref[...], preferred_element_type=F32).astype(o_ref.dtype)


def _matmul(x, w, out_dtype, tm, tn):
    m, k = x.shape
    n = w.shape[1]
    return pl.pallas_call(
        _mm_kernel,
        grid=(m // tm, n // tn),
        in_specs=[pl.BlockSpec((tm, k), lambda i, j: (i, 0)),
                  pl.BlockSpec((k, tn), lambda i, j: (0, j))],
        out_specs=pl.BlockSpec((tm, tn), lambda i, j: (i, j)),
        out_shape=jax.ShapeDtypeStruct((m, n), out_dtype),
        compiler_params=_params("arbitrary", "arbitrary"),
        name="in_proj",
    )(x, w)


def _mm_nt_kernel(w_ref, b_ref, x_ref, o_ref):
    y = lax.dot_general(w_ref[...], x_ref[...], _NT, preferred_element_type=F32) + b_ref[...]
    o_ref[...] = y.astype(o_ref.dtype)


def _matmul_nt(w_t, b_col, x, out_dtype, tm):
    n, k = w_t.shape
    m = x.shape[0]
    return pl.pallas_call(
        _mm_nt_kernel,
        grid=(m // tm,),
        in_specs=[pl.BlockSpec((n, k), lambda i: (0, 0)),
                  pl.BlockSpec((n, 1), lambda i: (0, 0)),
                  pl.BlockSpec((tm, k), lambda i: (i, 0))],
        out_specs=pl.BlockSpec((n, tm), lambda i: (0, i)),
        out_shape=jax.ShapeDtypeStruct((n, m), out_dtype),
        compiler_params=_params("arbitrary"),
        name="in_proj_t",
    )(w_t, b_col, x)


def _cum_kernel(g_ref, b_ref, o_ref):
    x = g_ref[...] + b_ref[...]
    x = jnp.minimum(x, 0.0) - jnp.log(1.0 + jnp.exp(-jnp.abs(x)))
    n = x.shape[1]
    lane = lax.broadcasted_iota(jnp.int32, x.shape, 1)
    k = 1
    while k < n:
        x = x + jnp.where(lane >= k, pltpu.roll(x, k, 1), 0.0)
        k *= 2
    o_ref[...] = x * LOG2E


def _fox_cumlog(logits_t, b_col, batch, seq):
    return pl.pallas_call(
        _cum_kernel,
        grid=(batch,),
        in_specs=[pl.BlockSpec((8, seq), lambda b: (0, b)),
                  pl.BlockSpec((8, 1), lambda b: (0, 0))],
        out_specs=pl.BlockSpec((8, seq), lambda b: (0, b)),
        out_shape=jax.ShapeDtypeStruct(logits_t.shape, F32),
        compiler_params=_params("arbitrary"),
        name="fox_cumlog",
    )(logits_t, b_col)


def _bias_kernel(rb_ref, d_ref, p_ref, *, t):
    h = pl.program_id(0)
    i = lax.broadcasted_iota(jnp.int32, (t, t), 0)
    j = lax.broadcasted_iota(jnp.int32, (t, t), 1)
    max_exact = REL_BUCKETS // 2
    far = rb_ref[(REL_BUCKETS - 1) * 8 + h]
    for out_ref, off in ((d_ref, 0), (p_ref, t)):
        dist = j - i + off
        d_large = jnp.maximum(dist, max_exact).astype(F32)
        large = max_exact + (jnp.log(d_large / max_exact) / math.log(REL_MAX_DIST / max_exact)
                             * (REL_BUCKETS - max_exact)).astype(jnp.int32)
        large = jnp.minimum(large, REL_BUCKETS - 1)
        bucket = jnp.where(dist < max_exact, dist, large)
        acc = jnp.zeros((t, t), F32)
        for bkt in range(REL_BUCKETS - 1):
            acc = jnp.where(bucket == bkt, (rb_ref[bkt * 8 + h] - far) * LOG2E, acc)
        if off == 0:
            acc = jnp.where(dist >= 0, acc, NEG)
        out_ref[0] = acc


def _bias_blocks(rel_bias, t):
    n_heads = rel_bias.shape[1]
    shape = jax.ShapeDtypeStruct((n_heads, t, t), F32)
    spec = pl.BlockSpec((1, t, t), lambda h: (h, 0, 0))
    return pl.pallas_call(
        functools.partial(_bias_kernel, t=t),
        grid=(n_heads,),
        in_specs=[pl.BlockSpec(memory_space=pltpu.SMEM)],
        out_specs=(spec, spec),
        out_shape=(shape, shape),
        compiler_params=_params("arbitrary"),
        name="rel_bias_blocks",
    )(rel_bias.reshape(-1))


def _online_update(s_t, v_t, carry):
    m, acc = carry
    m_new = jnp.maximum(m, jnp.max(s_t, axis=0, keepdims=True))
    p = jnp.exp2((s_t - m_new).astype(BF16))
    alpha = jnp.exp2(m - m_new)
    acc = alpha * acc + jnp.dot(v_t, p, preferred_element_type=F32)
    return m_new, acc


def _normalized(acc):
    return acc[:LANES] / acc[LANES:LANES + 1]


def _sweep(n, t, qi, raw, fix, values, near_bias, s_ref):
    la = s_ref.shape[0]

    def step(j, carries, kind):
        ahead = {}
        out = []
        for c in range(n):
            if c + la < n:
                ahead[c + la] = raw(c + la, j)
            elif kind != "diag":
                s_ref[c + la - n] = raw(c + la - n, j + 1)
            s_t = s_ref[c] if c < la else ahead.pop(c)
            out.append(_online_update(fix(c, s_t, j, kind), values(c, j), carries[c]))
        return tuple(out)

    for c in range(la):
        s_ref[c] = raw(c, 0)
    carries = tuple(_init_carry(t) for _ in range(n))
    if near_bias:
        carries = lax.fori_loop(0, jnp.maximum(qi - 1, 0), lambda j, s: step(j, s, "far"), carries)
        carries = lax.cond(qi >= 1, lambda s: step(qi - 1, s, "near"), lambda s: s, carries)
    else:
        carries = lax.fori_loop(0, qi, lambda j, s: step(j, s, "far"), carries)
    return step(qi, carries, "diag")


def _init_carry(t):
    return (jnp.full((1, t), NEG, F32), jnp.zeros((V_ROWS, t), F32))


def _lane_select(x, lo, hi):
    lane = lax.broadcasted_iota(jnp.int32, x.shape, 1)
    return jnp.where((lane >= lo) & (lane < hi), x, jnp.zeros_like(x))


def _merge_pair(a, b):
    row = lax.broadcasted_iota(jnp.int32, a.shape, 0)
    return jnp.where(row < HEAD_DIM, a, b)


def _causal_t(t):
    return (lax.broadcasted_iota(jnp.int32, (t, t), 0) <= lax.broadcasted_iota(jnp.int32, (t, t), 1))


def _band_bias(s_t, d_t, p_t, diag):
    n = s_t.shape[0] // BIAS_BLOCK
    rows = []
    for b in range(n):
        blocks = []
        for a in range(n):
            blk = s_t[b * BIAS_BLOCK:(b + 1) * BIAS_BLOCK, a * BIAS_BLOCK:(a + 1) * BIAS_BLOCK]
            if diag and b > a:
                blk = jnp.full_like(blk, NEG)
            elif diag and b == a:
                blk = blk + d_t
            elif (diag and b == a - 1) or (not diag and b == n - 1 and a == 0):
                blk = blk + p_t
            blocks.append(blk)
        rows.append(jnp.concatenate(blocks, axis=1))
    return jnp.concatenate(rows, axis=0)


def _attn_specs(t, seq, nq, cb, vrow):
    w = GROUP_WIDTH
    return [pl.BlockSpec((t, w), lambda b, i: (b * nq + i, cb)),
            pl.BlockSpec((seq, w), lambda b, i: (b, cb + 1)),
            pl.BlockSpec((V_SLAB, seq), lambda b, i: (vrow, b))]


def _attn_call(kernel_fn, name, batch, seq, t, in_specs, args, scratch=()):
    nq = seq // t
    return pl.pallas_call(
        kernel_fn,
        grid=(batch, nq),
        in_specs=in_specs,
        out_specs=pl.BlockSpec((t, GROUP_WIDTH), lambda b, i: (b * nq + i, 0)),
        out_shape=jax.ShapeDtypeStruct((batch * seq, GROUP_WIDTH), BF16),
        scratch_shapes=[pltpu.VMEM((SCORE_LOOKAHEAD, t, t), F32)] + list(scratch),
        compiler_params=_params("arbitrary", "arbitrary"),
        name=name,
    )(*args)


def _head_queries(q_ref, width):
    out = []
    for lo in range(0, GROUP_WIDTH, width):
        p = lo // LANES
        out.append(_lane_select(q_ref[:, p * LANES:(p + 1) * LANES], lo - p * LANES, lo - p * LANES + width))
    return out


def _fox_kernel(q_ref, k_ref, vt_ref, c_ref, o_ref, s_ref, *, t):
    qi = pl.program_id(1)
    causal = _causal_t(t)
    qms = _head_queries(q_ref, HEAD_DIM)

    def raw(h, j):
        sl = slice((h // 2) * LANES, (h // 2 + 1) * LANES)
        return lax.dot_general(k_ref[pl.ds(pl.multiple_of(j * t, t), t), sl], qms[h], _NT,
                               preferred_element_type=F32)

    def fix(h, s_t, j, kind):
        s_t = s_t - c_ref[pl.ds(pl.multiple_of(j * t, t), t), h:h + 1]
        return jnp.where(causal, s_t, NEG) if kind == "diag" else s_t

    def values(h, j):
        return vt_ref[(h // 2) * V_ROWS:(h // 2 + 1) * V_ROWS, pl.ds(pl.multiple_of(j * t, t), t)]

    carries = _sweep(GROUP_HEADS, t, qi, raw, fix, values, False, s_ref)
    outs = [_normalized(acc) for (_, acc) in carries]
    for p in range(GROUP_HEADS // 2):
        o_t = _merge_pair(outs[2 * p], outs[2 * p + 1])
        o_ref[:, p * LANES:(p + 1) * LANES] = o_t.T.astype(o_ref.dtype)


def _fox_attention(h_bf, v_t, cum_cols, batch, seq, cb, vrow, t=ATTN_TILE):
    nq = seq // t
    specs = _attn_specs(t, seq, nq, cb, vrow) + [pl.BlockSpec((seq, 8), lambda b, i: (b, 0))]
    return _attn_call(functools.partial(_fox_kernel, t=t), "fox_attn", batch, seq, t, specs,
                      (h_bf, h_bf, v_t, cum_cols))


def _diff_kernel(q_ref, k_ref, vt_ref, bd_ref, bp_ref, lam_ref, lc_ref, g_ref, o_ref, s_ref, *, t):
    qi = pl.program_id(1)
    lam_p = lam_ref[...]
    lam = (jnp.exp(jnp.sum(lam_p[0:1] * lam_p[1:2], axis=-1, keepdims=True))
           - jnp.exp(jnp.sum(lam_p[2:3] * lam_p[3:4], axis=-1, keepdims=True)) + lc_ref[0:1, 0:1])
    qms = _head_queries(q_ref, DIFF_QK_DIM)

    def raw(c, j):
        sl = slice((c // 4) * LANES, (c // 4 + 1) * LANES)
        return lax.dot_general(k_ref[pl.ds(pl.multiple_of(j * t, t), t), sl], qms[c], _NT,
                               preferred_element_type=F32)

    def fix(c, s_t, j, kind):
        return s_t if kind == "far" else _band_bias(s_t, bd_ref[c // 2], bp_ref[c // 2], kind == "diag")

    def values(c, j):
        return vt_ref[(c // 4) * V_ROWS:(c // 4 + 1) * V_ROWS, pl.ds(pl.multiple_of(j * t, t), t)]

    carries = _sweep(2 * GROUP_HEADS, t, qi, raw, fix, values, True, s_ref)
    row = lax.broadcasted_iota(jnp.int32, (LANES, t), 0)
    outs = []
    for h in range(GROUP_HEADS):
        o = _normalized(carries[2 * h][1]) - lam * _normalized(carries[2 * h + 1][1])
        lo = (h % 2) * HEAD_DIM
        o = jnp.where((row >= lo) & (row < lo + HEAD_DIM), o, 0.0)
        ms = jnp.sum(o * o, axis=0, keepdims=True) * (1.0 / HEAD_DIM)
        outs.append(o * lax.rsqrt(ms + DIFF_SUBLN_EPS))
    for p in range(GROUP_HEADS // 2):
        sl = slice(p * LANES, (p + 1) * LANES)
        o_t = _merge_pair(outs[2 * p], outs[2 * p + 1])
        o_ref[:, sl] = (o_t.T * g_ref[:, sl] * lc_ref[1:2, 0:1]).astype(o_ref.dtype)


def _diff_attention(h_bf, v_t, bias_d, bias_p, lam_p, lam_c, gain, batch, seq, cb, vrow, t=ATTN_TILE):
    nq = seq // t
    bias_spec = pl.BlockSpec((GROUP_HEADS, BIAS_BLOCK, BIAS_BLOCK), lambda b, i: (0, 0, 0))
    full = lambda a: pl.BlockSpec(a.shape, lambda b, i: (0, 0))
    specs = _attn_specs(t, seq, nq, cb, vrow) + [bias_spec, bias_spec, full(lam_p), full(lam_c), full(gain)]
    return _attn_call(functools.partial(_diff_kernel, t=t), "diff_attn", batch, seq, t, specs,
                      (h_bf, h_bf, v_t, bias_d, bias_p, lam_p, lam_c, gain))


def _moba_kernel(q_ref, k_ref, vt_ref, bd_ref, bp_ref, o_ref, s_ref, kmean_ref, *, t, n_blocks):
    qi = pl.program_id(1)
    per_tile = t // MOBA_BLOCK

    @pl.when(qi == 0)
    def _():
        kmean_ref[...] = jnp.zeros_like(kmean_ref)
        for blk in range(n_blocks):
            kb = k_ref[blk * MOBA_BLOCK:(blk + 1) * MOBA_BLOCK, :].astype(F32)
            kmean_ref[blk:blk + 1, :] = jnp.sum(kb, axis=0, keepdims=True) * (1.0 / MOBA_BLOCK)

    blk_i = lax.broadcasted_iota(jnp.int32, (t, LANES), 1)
    blk_f = blk_i.astype(F32)
    row_i = lax.broadcasted_iota(jnp.int32, (t, LANES), 0)
    row_blk = sum((row_i >= n * MOBA_BLOCK).astype(jnp.int32) for n in range(1, per_tile))
    own = qi * per_tile + row_blk
    qms = _head_queries(q_ref, HEAD_DIM)
    q_cats = []
    for h in range(GROUP_HEADS):
        sl = slice((h // 2) * LANES, (h // 2 + 1) * LANES)
        gate = lax.dot_general(qms[h].astype(F32), kmean_ref[:, sl], _NT,
                               precision=lax.Precision.HIGHEST, preferred_element_type=F32)
        g = jnp.where(blk_i < own, gate, NEG)
        keep = jnp.where(blk_i == own, 1.0, 0.0)
        for _ in range(MOBA_TOPK):
            mx = jnp.max(g, axis=-1, keepdims=True)
            first = jnp.min(jnp.where(g == mx, blk_f, 1e9), axis=-1, keepdims=True)
            pick = jnp.where(mx > 0.5 * NEG, jnp.where(blk_f == first, 1.0, 0.0), 0.0)
            keep = jnp.maximum(keep, pick)
            g = jnp.where(pick > 0.0, NEG, g)
        q_cats.append(jnp.concatenate([qms[h], (1.0 - keep).astype(BF16)], axis=1))

    def raw(h, j):
        sl = slice((h // 2) * LANES, (h // 2 + 1) * LANES)
        reject = jnp.where(blk_i == j * per_tile + row_blk, NEG, 0.0).astype(BF16)
        k_cat = jnp.concatenate([k_ref[pl.ds(pl.multiple_of(j * t, t), t), sl], reject], axis=1)
        return lax.dot_general(k_cat, q_cats[h], _NT, preferred_element_type=F32)

    def fix(h, s_t, j, kind):
        return s_t if kind == "far" else _band_bias(s_t, bd_ref[h], bp_ref[h], kind == "diag")

    def values(h, j):
        return vt_ref[(h // 2) * V_ROWS:(h // 2 + 1) * V_ROWS, pl.ds(pl.multiple_of(j * t, t), t)]

    carries = _sweep(GROUP_HEADS, t, qi, raw, fix, values, True, s_ref)
    outs = [_normalized(acc) for (_, acc) in carries]
    for p in range(GROUP_HEADS // 2):
        o_t = _merge_pair(outs[2 * p], outs[2 * p + 1])
        o_ref[:, p * LANES:(p + 1) * LANES] = o_t.T.astype(o_ref.dtype)


def _moba_attention(h_bf, v_t, bias_d, bias_p, batch, seq, cb, vrow, t=ATTN_TILE):
    nq = seq // t
    bias_spec = pl.BlockSpec((GROUP_HEADS, BIAS_BLOCK, BIAS_BLOCK), lambda b, i: (1, 0, 0))
    specs = _attn_specs(t, seq, nq, cb, vrow) + [bias_spec, bias_spec]
    return _attn_call(functools.partial(_moba_kernel, t=t, n_blocks=seq // MOBA_BLOCK), "moba_attn",
                      batch, seq, t, specs, (h_bf, h_bf, v_t, bias_d, bias_p),
                      scratch=[pltpu.VMEM((LANES, GROUP_WIDTH), F32)])


def _mla_prep_kernel(c_ref, gq_ref, gkv_ref, wqa_ref, wqb_ref, wk_ref, wvt_ref, ones_ref, e_ref,
                     tq_ref, tk_ref, q_ref, k_ref, vt_ref):
    c = c_ref[...]
    cq = c[:, :MLA_Q_LORA]
    ckv = c[:, MLA_Q_LORA:MLA_Q_LORA + MLA_KV_LORA]
    kr = c[:, MLA_Q_LORA + MLA_KV_LORA:]
    cq = cq * lax.rsqrt(jnp.mean(cq * cq, axis=-1, keepdims=True) + RMS_EPS) * gq_ref[...]
    ckv = ckv * lax.rsqrt(jnp.mean(ckv * ckv, axis=-1, keepdims=True) + RMS_EPS) * gkv_ref[...]
    cq = cq.astype(BF16)
    ckv = ckv.astype(BF16)
    tq = tq_ref[...]
    cos_q = jnp.concatenate([tq[:, :LANES]] * GROUP_HEADS, axis=1)
    sin_q = jnp.concatenate([tq[:, LANES:]] * GROUP_HEADS, axis=1)
    q = (jnp.dot(cq, wqa_ref[...], preferred_element_type=F32) * cos_q
         + jnp.dot(cq, wqb_ref[...], preferred_element_type=F32) * sin_q)
    q_ref[...] = (q * ((MLA_NOPE + MLA_ROPE) ** -0.5 * LOG2E)).astype(q_ref.dtype)
    k_rope = (kr * tk_ref[...]).astype(BF16)
    k = (jnp.dot(ckv, wk_ref[...], preferred_element_type=F32)
         + jnp.dot(k_rope, e_ref[...], preferred_element_type=F32))
    k_ref[...] = k.astype(k_ref.dtype)
    v_t = lax.dot_general(wvt_ref[...], ckv, _NT, preferred_element_type=F32) + ones_ref[...]
    vt_ref[...] = v_t.astype(vt_ref.dtype)


def _mla_prep(c_lat, gq, gkv, wqa, wqb, wk, wvt, ones, e_mat, tab_q, tab_k, seq, tm=512):
    m = c_lat.shape[0]
    ns = seq // tm
    full = lambda a: pl.BlockSpec(a.shape, lambda i: (0, 0))
    slot_w = GROUP_HEADS * LANES
    return pl.pallas_call(
        _mla_prep_kernel,
        grid=(m // tm,),
        in_specs=[pl.BlockSpec((tm, c_lat.shape[1]), lambda i: (i, 0)),
                  full(gq), full(gkv), full(wqa), full(wqb), full(wk), full(wvt), full(ones), full(e_mat),
                  pl.BlockSpec((tm, tab_q.shape[1]), lambda i: (i % ns, 0)),
                  pl.BlockSpec((tm, tab_k.shape[1]), lambda i: (i % ns, 0))],
        out_specs=(pl.BlockSpec((tm, slot_w), lambda i: (i, 0)),
                   pl.BlockSpec((tm, slot_w), lambda i: (i, 0)),
                   pl.BlockSpec((V_SLAB, tm), lambda i: (0, i))),
        out_shape=(jax.ShapeDtypeStruct((m, slot_w), BF16),
                   jax.ShapeDtypeStruct((m, slot_w), BF16),
                   jax.ShapeDtypeStruct((V_SLAB, m), BF16)),
        compiler_params=_params("arbitrary"),
        name="mla_prep",
    )(c_lat, gq, gkv, wqa, wqb, wk, wvt, ones, e_mat, tab_q, tab_k)


def _mla_kernel(q_ref, k_ref, vt_ref, o_ref, s_ref, *, t):
    qi = pl.program_id(1)
    causal = _causal_t(t)

    def raw(h, j):
        hsl = slice(h * LANES, (h + 1) * LANES)
        return lax.dot_general(k_ref[pl.ds(pl.multiple_of(j * t, t), t), hsl], q_ref[:, hsl], _NT,
                               preferred_element_type=F32)

    def fix(h, s_t, j, kind):
        return jnp.where(causal, s_t, NEG) if kind == "diag" else s_t

    def values(h, j):
        return vt_ref[(h // 2) * V_ROWS:(h // 2 + 1) * V_ROWS, pl.ds(pl.multiple_of(j * t, t), t)]

    carries = _sweep(GROUP_HEADS, t, qi, raw, fix, values, False, s_ref)
    outs = [_normalized(acc) for (_, acc) in carries]
    for p in range(GROUP_HEADS // 2):
        o_t = _merge_pair(outs[2 * p], outs[2 * p + 1])
        o_ref[:, p * LANES:(p + 1) * LANES] = o_t.T.astype(o_ref.dtype)


def _mla_attention(q, k, v_t, batch, seq, t=ATTN_TILE):
    nq = seq // t
    slot_w = q.shape[1]
    specs = [pl.BlockSpec((t, slot_w), lambda b, i: (b * nq + i, 0)),
             pl.BlockSpec((seq, slot_w), lambda b, i: (b, 0)),
             pl.BlockSpec((V_SLAB, seq), lambda b, i: (0, b))]
    return _attn_call(functools.partial(_mla_kernel, t=t), "mla_attn", batch, seq, t, specs, (q, k, v_t))


def _layernorm(t, g, b):
    mu = jnp.mean(t, axis=-1, keepdims=True)
    d = t - mu
    var = jnp.mean(d * d, axis=-1, keepdims=True)
    return d * lax.rsqrt(var + LN_EPS) * g + b


def _out_kernel(o1_ref, o2_ref, o3_ref, o4_ref, w_ref, x_ref, g_ref, b_ref, of_ref, ob_ref, *, alpha):
    mix = None
    for n, o_ref in enumerate((o1_ref, o2_ref, o3_ref, o4_ref)):
        part = jnp.dot(o_ref[...], w_ref[n * GROUP_WIDTH:(n + 1) * GROUP_WIDTH, :],
                       preferred_element_type=F32)
        mix = part if mix is None else mix + part
    y = _layernorm(alpha * x_ref[...] + mix, g_ref[...], b_ref[...])
    of_ref[...] = y
    ob_ref[...] = y.astype(ob_ref.dtype)


def _out_proj(outs, w_o, x, g, b, alpha, tm=512):
    m, d = x.shape
    o_spec = pl.BlockSpec((tm, GROUP_WIDTH), lambda i: (i, 0))
    row_spec = pl.BlockSpec((tm, d), lambda i: (i, 0))
    vec_spec = pl.BlockSpec((1, d), lambda i: (0, 0))
    return pl.pallas_call(
        functools.partial(_out_kernel, alpha=alpha),
        grid=(m // tm,),
        in_specs=[o_spec, o_spec, o_spec, o_spec,
                  pl.BlockSpec(w_o.shape, lambda i: (0, 0)), row_spec, vec_spec, vec_spec],
        out_specs=(row_spec, row_spec),
        out_shape=(jax.ShapeDtypeStruct((m, d), F32), jax.ShapeDtypeStruct((m, d), BF16)),
        compiler_params=_params("arbitrary"),
        name="out_proj_ln",
    )(*outs, w_o, x, g, b)


def _ffn_kernel(xb_ref, xh_ref, xf_ref, wup_ref, cp_ref, wd_ref, g_ref, b_ref,
                of_ref, ob_ref, *, alpha, tiles_per_seq, n_chunks):
    i = pl.program_id(0)
    halo = xh_ref[...]
    halo = jnp.where(i % tiles_per_seq == 0, jnp.zeros_like(halo), halo)
    xe = jnp.concatenate([halo, xb_ref[...]], axis=0)

    def cols(c, br):
        return slice(br * D_FF_PAD + c * FF_CHUNK, br * D_FF_PAD + (c + 1) * FF_CHUNK)

    def project(c):
        return [jnp.dot(xe, wup_ref[:, cols(c, br)], preferred_element_type=F32) for br in range(2)]

    def gated(c, us):
        ys = []
        for br, u in enumerate(us):
            cp = cp_ref[:, cols(c, br)]
            y = cp[2:3] * u + cp[1:2] * pltpu.roll(u, 1, 0) + cp[0:1] * pltpu.roll(u, 2, 0) + cp[3:4]
            ys.append(y[CONV_HALO:])
        return (ys[0] * jax.nn.sigmoid(ys[0]) * ys[1]).astype(BF16)

    acc = None
    u_next = project(0)
    for c in range(n_chunks):
        us = u_next
        if c + 1 < n_chunks:
            u_next = project(c + 1)
        part = jnp.dot(gated(c, us), wd_ref[c * FF_CHUNK:(c + 1) * FF_CHUNK, :], preferred_element_type=F32)
        acc = part if acc is None else acc + part
    y = _layernorm(alpha * xf_ref[...] + acc, g_ref[...], b_ref[...])
    of_ref[...] = y
    ob_ref[...] = y.astype(ob_ref.dtype)


def _ffn(xb, xf, w_up, conv_p, w_down, g, b, alpha, seq, tm=256):
    m, d = xf.shape
    n_chunks = D_FF_PAD // FF_CHUNK
    halo_blocks = tm // CONV_HALO
    row_spec = pl.BlockSpec((tm, d), lambda i: (i, 0))
    vec_spec = pl.BlockSpec((1, d), lambda i: (0, 0))
    resident = lambda a: pl.BlockSpec(a.shape, lambda i: (0, 0), pipeline_mode=pl.Buffered(1))
    return pl.pallas_call(
        functools.partial(_ffn_kernel, alpha=alpha, tiles_per_seq=seq // tm, n_chunks=n_chunks),
        grid=(m // tm,),
        in_specs=[row_spec,
                  pl.BlockSpec((CONV_HALO, d), lambda i: (jnp.maximum(i * halo_blocks - 1, 0), 0)),
                  row_spec, resident(w_up), resident(conv_p), resident(w_down), vec_spec, vec_spec],
        out_specs=(row_spec, row_spec),
        out_shape=(jax.ShapeDtypeStruct((m, d), F32), jax.ShapeDtypeStruct((m, d), BF16)),
        compiler_params=_params("arbitrary"),
        name="conv_ffn_ln",
    )(xb, xb, xf, w_up, conv_p, w_down, g, b)


def _swap_halves(w):
    half = w.shape[-1] // 2
    return jnp.concatenate([-w[..., half:], w[..., :half]], axis=-1)


def _value_slab(w_v):
    depth, k, _ = w_v.shape
    w = w_v.reshape(depth, k, GROUP_HEADS // 2, LANES)
    w = jnp.pad(w, ((0, 0), (0, 0), (0, 0), (0, V_ROWS - LANES)))
    return jnp.swapaxes(w.reshape(depth, k, V_SLAB), 1, 2)


def _ones_rows(n_slabs):
    row = jnp.arange(n_slabs * V_SLAB) % V_ROWS
    return (row == LANES).astype(F32)[:, None]


def _prep_weights(w_in, mla_w_uq, mla_w_ukv, w_up, conv_w, conv_b, w_down):
    depth, d, _ = w_in.shape
    gw = GROUP_WIDTH
    fox0 = 0
    diff0 = 3 * gw + GROUP_HEADS
    moba0 = diff0 + 3 * gw
    mla0 = moba0 + 3 * gw
    sm = HEAD_DIM ** -0.5 * LOG2E
    sd = DIFF_QK_DIM ** -0.5 * LOG2E
    w_qk = jnp.concatenate([
        w_in[:, :, fox0:fox0 + gw] * sm, w_in[:, :, fox0 + gw:fox0 + 2 * gw],
        w_in[:, :, diff0:diff0 + gw] * sd, w_in[:, :, diff0 + gw:diff0 + 2 * gw],
        w_in[:, :, moba0:moba0 + gw] * sm, w_in[:, :, moba0 + gw:moba0 + 2 * gw]], axis=-1).astype(BF16)
    w_vt = jnp.concatenate([_value_slab(w_in[:, :, g0 + 2 * gw:g0 + 3 * gw])
                            for g0 in (fox0, diff0, moba0)], axis=1).astype(BF16)
    w_gt = jnp.swapaxes(jnp.pad(w_in[:, :, 3 * gw:3 * gw + GROUP_HEADS],
                                ((0, 0), (0, 0), (0, 8 - GROUP_HEADS))), 1, 2).astype(BF16)
    kr0 = mla0 + MLA_Q_LORA + MLA_KV_LORA
    w_kr = w_in[:, :, kr0:kr0 + MLA_ROPE]
    w_lat = jnp.concatenate([w_in[:, :, mla0:kr0], w_kr, _swap_halves(w_kr),
                             jnp.zeros((depth, d, LANES - 2 * MLA_ROPE), F32)], axis=-1).astype(BF16)

    uq = mla_w_uq.reshape(depth, MLA_Q_LORA, GROUP_HEADS, MLA_NOPE + MLA_ROPE)
    pad = LANES - MLA_NOPE - MLA_ROPE
    wqa = jnp.pad(uq, ((0, 0), (0, 0), (0, 0), (0, pad)))
    wqb = jnp.concatenate([jnp.zeros_like(uq[..., :MLA_NOPE]), _swap_halves(uq[..., MLA_NOPE:]),
                           jnp.zeros(uq.shape[:-1] + (pad,), F32)], axis=-1)
    ukv = mla_w_ukv.reshape(depth, MLA_KV_LORA, GROUP_HEADS, MLA_NOPE + HEAD_DIM)
    wk = jnp.pad(ukv[..., :MLA_NOPE], ((0, 0), (0, 0), (0, 0), (0, LANES - MLA_NOPE)))
    flat = lambda a: a.reshape(depth, a.shape[1], -1).astype(BF16)
    wvt = _value_slab(flat(ukv[..., MLA_NOPE:])).astype(BF16)

    ffp = D_FF_PAD - D_FF
    padc = lambda a: jnp.pad(a, ((0, 0), (0, 0), (0, ffp)))
    w_up_p = jnp.concatenate([padc(w_up[:, :, :D_FF]), padc(w_up[:, :, D_FF:])], axis=-1).astype(BF16)
    conv = jnp.concatenate([conv_w, conv_b[:, None, :],
                            jnp.zeros((depth, 8 - CONV_WIDTH - 1, 2 * D_FF), F32)], axis=1)
    conv_p = jnp.concatenate([padc(conv[:, :, :D_FF]), padc(conv[:, :, D_FF:])], axis=-1)
    w_down_p = jnp.pad(w_down, ((0, 0), (0, ffp), (0, 0))).astype(BF16)
    return w_qk, w_vt, w_gt, w_lat, flat(wqa), flat(wqb), flat(wk), wvt, w_up_p, conv_p, w_down_p


def _rope_constants(seq):
    inv = ROPE_THETA ** (-jnp.arange(0, MLA_ROPE // 2, dtype=F32) * 2.0 / MLA_ROPE)
    ang = jnp.arange(seq, dtype=F32)[:, None] * inv[None, :]
    cos, sin = jnp.cos(ang), jnp.sin(ang)
    cos2 = jnp.concatenate([cos, cos], axis=1)
    sin2 = jnp.concatenate([sin, sin], axis=1)
    pad = jnp.zeros((seq, LANES - MLA_NOPE - MLA_ROPE), F32)
    tab_q = jnp.concatenate([jnp.ones((seq, MLA_NOPE), F32), cos2, pad,
                             jnp.zeros((seq, MLA_NOPE), F32), sin2, pad], axis=1)
    tab_k = jnp.concatenate([cos2, sin2, jnp.zeros((seq, LANES - 2 * MLA_ROPE), F32)], axis=1)
    r = jnp.arange(LANES)[:, None]
    c = jnp.arange(GROUP_HEADS * LANES)[None, :]
    e_mat = ((r < 2 * MLA_ROPE) & (c % LANES == MLA_NOPE + r % MLA_ROPE)).astype(BF16)
    return tab_q, tab_k, e_mat


def kernel(x, w_in, b_forget, diff_lambda, diff_subln, mla_q_norm, mla_kv_norm, mla_w_uq, mla_w_ukv,
           rel_bias, w_o, ln1_g, ln1_b, w_up, conv_w, conv_b, w_down, ln2_g, ln2_b):
    batch, seq, d = x.shape
    depth = w_in.shape[0]
    alpha = (2 * depth) ** 0.25
    assert seq % ATTN_TILE == 0 and ATTN_TILE % MOBA_BLOCK == 0 and seq // MOBA_BLOCK <= LANES

    (w_qk, w_vt, w_gt, w_lat, wqa, wqb, wk, wvt, w_up_p, conv_p, w_down_p) = _prep_weights(
        w_in, mla_w_uq, mla_w_ukv, w_up, conv_w, conv_b, w_down)
    w_o_b = w_o.astype(BF16)
    tab_q, tab_k, e_mat = _rope_constants(seq)
    bias_d, bias_p = _bias_blocks(rel_bias, BIAS_BLOCK)
    b_col = jnp.pad(b_forget, ((0, 0), (0, 8 - GROUP_HEADS)))[:, :, None]
    gain = jnp.tile(diff_subln, (1, GROUP_HEADS))[:, None, :]

    xf = x.reshape(batch * seq, d)
    xb = xf.astype(BF16)
    for l in range(depth):
        h_bf = _matmul(xb, w_qk[l], BF16, tm=512, tn=3 * GROUP_WIDTH)
        v_t = _matmul_nt(w_vt[l], _ones_rows(3), xb, BF16, tm=512)
        c_lat = _matmul(xb, w_lat[l], F32, tm=512, tn=w_lat.shape[-1])
        cum = _fox_cumlog(_matmul_nt(w_gt[l], jnp.zeros((8, 1), F32), xb, F32, tm=1024), b_col[l], batch, seq)
        fox_o = _fox_attention(h_bf, v_t, cum.T, batch, seq, 0, 0)
        lam_init = 0.8 - 0.6 * math.exp(-0.3 * l)
        lam_c = jnp.concatenate([jnp.full((1, LANES), lam_init, F32),
                                 jnp.full((1, LANES), 1.0 - lam_init, F32),
                                 jnp.zeros((6, LANES), F32)], axis=0)
        diff_o = _diff_attention(h_bf, v_t, bias_d, bias_p, diff_lambda[l], lam_c, gain[l], batch, seq, 2, 1)
        moba_o = _moba_attention(h_bf, v_t, bias_d, bias_p, batch, seq, 4, 2)
        q_m, k_m, v_m = _mla_prep(c_lat, mla_q_norm[l][None], mla_kv_norm[l][None],
                                  wqa[l], wqb[l], wk[l], wvt[l], _ones_rows(1), e_mat, tab_q, tab_k, seq)
        mla_o = _mla_attention(q_m, k_m, v_m, batch, seq)
        xf, xb = _out_proj((fox_o, diff_o, moba_o, mla_o), w_o_b[l], xf, ln1_g[l][None], ln1_b[l][None], alpha)
        xf, xb = _ffn(xb, xf, w_up_p[l], conv_p[l], w_down_p[l], ln2_g[l][None], ln2_b[l][None], alpha, seq)
    return xf.reshape(batch, seq, d)
```

```python
import functools
import math

import jax
import jax.numpy as jnp
from jax import lax
from jax.experimental import pallas as pl
from jax.experimental.pallas import tpu as pltpu

F32 = jnp.float32
BF16 = jnp.bfloat16

HEAD_DIM = 64
GROUP_HEADS = 4
GROUP_WIDTH = GROUP_HEADS * HEAD_DIM
LANES = 128
DIFF_QK_DIM = HEAD_DIM // 2
DIFF_SUBLN_EPS = 1e-5
MOBA_BLOCK = 256
MOBA_TOPK = 3
MLA_Q_LORA = 256
MLA_KV_LORA = 128
MLA_NOPE = 64
MLA_ROPE = 32
ROPE_THETA = 10000.0
REL_BUCKETS = 32
REL_MAX_DIST = 128
D_FF = 2752
FF_CHUNK = 256
D_FF_PAD = -(-D_FF // FF_CHUNK) * FF_CHUNK
CONV_WIDTH = 3
CONV_HALO = 16
LN_EPS = 1e-5
RMS_EPS = 1e-6
NEG = -1e30
LOG2E = 1.4426950408889634

BIAS_BLOCK = 128
ATTN_TILE = 512
V_ROWS = LANES + 16
V_SLAB = (GROUP_HEADS // 2) * V_ROWS
SCORE_LOOKAHEAD = 2
SCORE_SLOTS = 4
VMEM_LIMIT = 56 * 1024 * 1024

_NT = (((1,), (1,)), ((), ()))


def _params(*sem):
    return pltpu.CompilerParams(dimension_semantics=sem, vmem_limit_bytes=VMEM_LIMIT)


def _mm_kernel(x_ref, w_ref, o_ref):
    o_ref[...] = jnp.dot(x_ref[...], w_ref[...], preferred_element_type=F32).astype(o_ref.dtype)


def _matmul(x, w, out_dtype, tm, tn):
    m, k = x.shape
    n = w.shape[1]
    return pl.pallas_call(
        _mm_kernel,
        grid=(m // tm, n // tn),
        in_specs=[pl.BlockSpec((tm, k), lambda i, j: (i, 0)),
                  pl.BlockSpec((k, tn), lambda i, j: (0, j))],
        out_specs=pl.BlockSpec((tm, tn), lambda i, j: (i, j)),
        out_shape=jax.ShapeDtypeStruct((m, n), out_dtype),
        compiler_params=_params("arbitrary", "arbitrary"),
        name="in_proj",
    )(x, w)


def _mm_nt_kernel(w_ref, b_ref, x_ref, o_ref):
    y = lax.dot_general(w_ref[...], x_ref[...], _NT, preferred_element_type=F32) + b_ref[...]
    o_ref[...] = y.astype(o_ref.dtype)


def _matmul_nt(w_t, b_col, x, out_dtype, tm):
    n, k = w_t.shape
    m = x.shape[0]
    return pl.pallas_call(
        _mm_nt_kernel,
        grid=(m // tm,),
        in_specs=[pl.BlockSpec((n, k), lambda i: (0, 0)),
                  pl.BlockSpec((n, 1), lambda i: (0, 0)),
                  pl.BlockSpec((tm, k), lambda i: (i, 0))],
        out_specs=pl.BlockSpec((n, tm), lambda i: (0, i)),
        out_shape=jax.ShapeDtypeStruct((n, m), out_dtype),
        compiler_params=_params("arbitrary"),
        name="in_proj_t",
    )(w_t, b_col, x)


def _cum_kernel(g_ref, b_ref, o_ref):
    x = g_ref[...] + b_ref[...]
    x = jnp.minimum(x, 0.0) - jnp.log(1.0 + jnp.exp(-jnp.abs(x)))
    n = x.shape[1]
    lane = lax.broadcasted_iota(jnp.int32, x.shape, 1)
    k = 1
    while k < n:
        x = x + jnp.where(lane >= k, pltpu.roll(x, k, 1), 0.0)
        k *= 2
    o_ref[...] = x * LOG2E


def _fox_cumlog(logits_t, b_col, batch, seq):
    return pl.pallas_call(
        _cum_kernel,
        grid=(batch,),
        in_specs=[pl.BlockSpec((8, seq), lambda b: (0, b)),
                  pl.BlockSpec((8, 1), lambda b: (0, 0))],
        out_specs=pl.BlockSpec((8, seq), lambda b: (0, b)),
        out_shape=jax.ShapeDtypeStruct(logits_t.shape, F32),
        compiler_params=_params("arbitrary"),
        name="fox_cumlog",
    )(logits_t, b_col)


def _bias_kernel(rb_ref, d_ref, p_ref, *, t):
    h = pl.program_id(0)
    i = lax.broadcasted_iota(jnp.int32, (t, t), 0)
    j = lax.broadcasted_iota(jnp.int32, (t, t), 1)
    max_exact = REL_BUCKETS // 2
    far = rb_ref[(REL_BUCKETS - 1) * 8 + h]
    for out_ref, off in ((d_ref, 0), (p_ref, t)):
        dist = j - i + off
        d_large = jnp.maximum(dist, max_exact).astype(F32)
        large = max_exact + (jnp.log(d_large / max_exact) / math.log(REL_MAX_DIST / max_exact)
                             * (REL_BUCKETS - max_exact)).astype(jnp.int32)
        large = jnp.minimum(large, REL_BUCKETS - 1)
        bucket = jnp.where(dist < max_exact, dist, large)
        acc = jnp.zeros((t, t), F32)
        for bkt in range(REL_BUCKETS - 1):
            acc = jnp.where(bucket == bkt, (rb_ref[bkt * 8 + h] - far) * LOG2E, acc)
        if off == 0:
            acc = jnp.where(dist >= 0, acc, NEG)
        out_ref[0] = acc


def _bias_blocks(rel_bias, t):
    n_heads = rel_bias.shape[1]
    shape = jax.ShapeDtypeStruct((n_heads, t, t), F32)
    spec = pl.BlockSpec((1, t, t), lambda h: (h, 0, 0))
    return pl.pallas_call(
        functools.partial(_bias_kernel, t=t),
        grid=(n_heads,),
        in_specs=[pl.BlockSpec(memory_space=pltpu.SMEM)],
        out_specs=(spec, spec),
        out_shape=(shape, shape),
        compiler_params=_params("arbitrary"),
        name="rel_bias_blocks",
    )(rel_bias.reshape(-1))


def _online_update(s_t, tile_max, v_t, carry):
    m, acc = carry
    m_new = jnp.maximum(m, tile_max)
    p = jnp.exp2((s_t - m_new).astype(BF16))
    acc = jnp.exp2(m - m_new) * acc + jnp.dot(v_t, p, preferred_element_type=F32)
    return m_new, acc


def _normalized(acc):
    return acc[:LANES] / acc[LANES:LANES + 1]


def _sweep(n, t, qi, raw, fix, values, near_bias, s_ref):
    la = SCORE_LOOKAHEAD
    slots = s_ref.shape[0]
    assert n % slots == 0 and la < slots

    def produce(c, j):
        s_t = raw(c, j)
        s_ref[c % slots] = s_t
        return jnp.max(s_t, axis=0, keepdims=True)

    def step(j, state, kind):
        carries, tops = state
        tops = dict(enumerate(tops))
        nxt = []
        out = []
        for c in range(n):
            if c + la < n:
                tops[c + la] = produce(c + la, j)
            elif kind != "diag":
                nxt.append(produce(c + la - n, j + 1))
            s_t, top = s_ref[c % slots], tops.pop(c)
            if kind != "far":
                s_t = fix(c, s_t, kind)
                top = jnp.max(s_t, axis=0, keepdims=True)
            out.append(_online_update(s_t, top, values(c, j), carries[c]))
        return tuple(out), (tuple(nxt) if nxt else state[1])

    state = (tuple(_init_carry(t) for _ in range(n)), tuple(produce(c, 0) for c in range(la)))
    if near_bias:
        state = lax.fori_loop(0, jnp.maximum(qi - 1, 0), lambda j, s: step(j, s, "far"), state)
        state = lax.cond(qi >= 1, lambda s: step(qi - 1, s, "near"), lambda s: s, state)
    else:
        state = lax.fori_loop(0, qi, lambda j, s: step(j, s, "far"), state)
    return step(qi, state, "diag")[0]


def _init_carry(t):
    return (jnp.full((1, t), NEG, F32), jnp.zeros((V_ROWS, t), F32))


def _lane_select(x, lo, hi):
    lane = lax.broadcasted_iota(jnp.int32, x.shape, 1)
    return jnp.where((lane >= lo) & (lane < hi), x, jnp.zeros_like(x))


def _merge_pair(a, b):
    row = lax.broadcasted_iota(jnp.int32, a.shape, 0)
    return jnp.where(row < HEAD_DIM, a, b)


def _causal_t(t):
    return (lax.broadcasted_iota(jnp.int32, (t, t), 0) <= lax.broadcasted_iota(jnp.int32, (t, t), 1))


def _band_bias(s_t, d_t, p_t, diag):
    n = s_t.shape[0] // BIAS_BLOCK
    rows = []
    for b in range(n):
        blocks = []
        for a in range(n):
            blk = s_t[b * BIAS_BLOCK:(b + 1) * BIAS_BLOCK, a * BIAS_BLOCK:(a + 1) * BIAS_BLOCK]
            if diag and b > a:
                blk = jnp.full_like(blk, NEG)
            elif diag and b == a:
                blk = blk + d_t
            elif (diag and b == a - 1) or (not diag and b == n - 1 and a == 0):
                blk = blk + p_t
            blocks.append(blk)
        rows.append(jnp.concatenate(blocks, axis=1))
    return jnp.concatenate(rows, axis=0)


def _attn_specs(t, seq, nq, cb, vrow):
    w = GROUP_WIDTH
    return [pl.BlockSpec((t, w), lambda b, i: (b * nq + i, cb)),
            pl.BlockSpec((seq, w), lambda b, i: (b, cb + 1)),
            pl.BlockSpec((V_SLAB, seq), lambda b, i: (vrow, b))]


def _attn_call(kernel_fn, name, batch, seq, t, in_specs, args, scratch=()):
    nq = seq // t
    return pl.pallas_call(
        kernel_fn,
        grid=(batch, nq),
        in_specs=in_specs,
        out_specs=pl.BlockSpec((t, GROUP_WIDTH), lambda b, i: (b * nq + i, 0)),
        out_shape=jax.ShapeDtypeStruct((batch * seq, GROUP_WIDTH), BF16),
        scratch_shapes=[pltpu.VMEM((SCORE_SLOTS, t, t), F32)] + list(scratch),
        compiler_params=_params("arbitrary", "arbitrary"),
        name=name,
    )(*args)


def _head_queries(q_ref, width):
    out = []
    for lo in range(0, GROUP_WIDTH, width):
        p = lo // LANES
        out.append(_lane_select(q_ref[:, p * LANES:(p + 1) * LANES], lo - p * LANES, lo - p * LANES + width))
    return out


def _fox_kernel(q_ref, k_ref, vt_ref, c_ref, o_ref, s_ref, *, t):
    qi = pl.program_id(1)
    causal = _causal_t(t)
    qms = _head_queries(q_ref, HEAD_DIM)

    def raw(h, j):
        sl = slice((h // 2) * LANES, (h // 2 + 1) * LANES)
        start = pl.multiple_of(j * t, t)
        s_t = lax.dot_general(k_ref[pl.ds(start, t), sl], qms[h], _NT, preferred_element_type=F32)
        return s_t - c_ref[pl.ds(start, t), h:h + 1]

    def fix(h, s_t, kind):
        return jnp.where(causal, s_t, NEG)

    def values(h, j):
        return vt_ref[(h // 2) * V_ROWS:(h // 2 + 1) * V_ROWS, pl.ds(pl.multiple_of(j * t, t), t)]

    carries = _sweep(GROUP_HEADS, t, qi, raw, fix, values, False, s_ref)
    outs = [_normalized(acc) for (_, acc) in carries]
    for p in range(GROUP_HEADS // 2):
        o_t = _merge_pair(outs[2 * p], outs[2 * p + 1])
        o_ref[:, p * LANES:(p + 1) * LANES] = o_t.T.astype(o_ref.dtype)


def _fox_attention(h_bf, v_t, cum_cols, batch, seq, cb, vrow, t=ATTN_TILE):
    nq = seq // t
    specs = _attn_specs(t, seq, nq, cb, vrow) + [pl.BlockSpec((seq, 8), lambda b, i: (b, 0))]
    return _attn_call(functools.partial(_fox_kernel, t=t), "fox_attn", batch, seq, t, specs,
                      (h_bf, h_bf, v_t, cum_cols))


def _diff_kernel(q_ref, k_ref, vt_ref, bd_ref, bp_ref, lam_ref, lc_ref, g_ref, o_ref, s_ref, *, t):
    qi = pl.program_id(1)
    lam_p = lam_ref[...]
    lam = (jnp.exp(jnp.sum(lam_p[0:1] * lam_p[1:2], axis=-1, keepdims=True))
           - jnp.exp(jnp.sum(lam_p[2:3] * lam_p[3:4], axis=-1, keepdims=True)) + lc_ref[0:1, 0:1])
    qms = _head_queries(q_ref, DIFF_QK_DIM)

    def raw(c, j):
        sl = slice((c // 4) * LANES, (c // 4 + 1) * LANES)
        return lax.dot_general(k_ref[pl.ds(pl.multiple_of(j * t, t), t), sl], qms[c], _NT,
                               preferred_element_type=F32)

    def fix(c, s_t, kind):
        return _band_bias(s_t, bd_ref[c // 2], bp_ref[c // 2], kind == "diag")

    def values(c, j):
        return vt_ref[(c // 4) * V_ROWS:(c // 4 + 1) * V_ROWS, pl.ds(pl.multiple_of(j * t, t), t)]

    carries = _sweep(2 * GROUP_HEADS, t, qi, raw, fix, values, True, s_ref)
    row = lax.broadcasted_iota(jnp.int32, (LANES, t), 0)
    outs = []
    for h in range(GROUP_HEADS):
        o = _normalized(carries[2 * h][1]) - lam * _normalized(carries[2 * h + 1][1])
        lo = (h % 2) * HEAD_DIM
        o = jnp.where((row >= lo) & (row < lo + HEAD_DIM), o, 0.0)
        ms = jnp.sum(o * o, axis=0, keepdims=True) * (1.0 / HEAD_DIM)
        outs.append(o * lax.rsqrt(ms + DIFF_SUBLN_EPS))
    for p in range(GROUP_HEADS // 2):
        sl = slice(p * LANES, (p + 1) * LANES)
        o_t = _merge_pair(outs[2 * p], outs[2 * p + 1])
        o_ref[:, sl] = (o_t.T * g_ref[:, sl] * lc_ref[1:2, 0:1]).astype(o_ref.dtype)


def _diff_attention(h_bf, v_t, bias_d, bias_p, lam_p, lam_c, gain, batch, seq, cb, vrow, t=ATTN_TILE):
    nq = seq // t
    bias_spec = pl.BlockSpec((GROUP_HEADS, BIAS_BLOCK, BIAS_BLOCK), lambda b, i: (0, 0, 0))
    full = lambda a: pl.BlockSpec(a.shape, lambda b, i: (0, 0))
    specs = _attn_specs(t, seq, nq, cb, vrow) + [bias_spec, bias_spec, full(lam_p), full(lam_c), full(gain)]
    return _attn_call(functools.partial(_diff_kernel, t=t), "diff_attn", batch, seq, t, specs,
                      (h_bf, h_bf, v_t, bias_d, bias_p, lam_p, lam_c, gain))


def _moba_kernel(q_ref, k_ref, vt_ref, bd_ref, bp_ref, o_ref, s_ref, kmean_ref, *, t, n_blocks):
    qi = pl.program_id(1)
    per_tile = t // MOBA_BLOCK

    @pl.when(qi == 0)
    def _():
        kmean_ref[...] = jnp.zeros_like(kmean_ref)
        for blk in range(n_blocks):
            kb = k_ref[blk * MOBA_BLOCK:(blk + 1) * MOBA_BLOCK, :].astype(F32)
            kmean_ref[blk:blk + 1, :] = jnp.sum(kb, axis=0, keepdims=True) * (1.0 / MOBA_BLOCK)

    blk_i = lax.broadcasted_iota(jnp.int32, (t, LANES), 1)
    blk_f = blk_i.astype(F32)
    row_i = lax.broadcasted_iota(jnp.int32, (t, LANES), 0)
    row_blk = sum((row_i >= n * MOBA_BLOCK).astype(jnp.int32) for n in range(1, per_tile))
    own = qi * per_tile + row_blk
    qms = _head_queries(q_ref, HEAD_DIM)
    q_cats = []
    for h in range(GROUP_HEADS):
        sl = slice((h // 2) * LANES, (h // 2 + 1) * LANES)
        gate = lax.dot_general(qms[h].astype(F32), kmean_ref[:, sl], _NT,
                               precision=lax.Precision.HIGHEST, preferred_element_type=F32)
        g = jnp.where(blk_i < own, gate, NEG)
        keep = jnp.where(blk_i == own, 1.0, 0.0)
        for _ in range(MOBA_TOPK):
            mx = jnp.max(g, axis=-1, keepdims=True)
            first = jnp.min(jnp.where(g == mx, blk_f, 1e9), axis=-1, keepdims=True)
            pick = jnp.where(mx > 0.5 * NEG, jnp.where(blk_f == first, 1.0, 0.0), 0.0)
            keep = jnp.maximum(keep, pick)
            g = jnp.where(pick > 0.0, NEG, g)
        q_cats.append(jnp.concatenate([qms[h], (1.0 - keep).astype(BF16)], axis=1))

    def raw(h, j):
        sl = slice((h // 2) * LANES, (h // 2 + 1) * LANES)
        reject = jnp.where(blk_i == j * per_tile + row_blk, NEG, 0.0).astype(BF16)
        k_cat = jnp.concatenate([k_ref[pl.ds(pl.multiple_of(j * t, t), t), sl], reject], axis=1)
        return lax.dot_general(k_cat, q_cats[h], _NT, preferred_element_type=F32)

    def fix(h, s_t, kind):
        return _band_bias(s_t, bd_ref[h], bp_ref[h], kind == "diag")

    def values(h, j):
        return vt_ref[(h // 2) * V_ROWS:(h // 2 + 1) * V_ROWS, pl.ds(pl.multiple_of(j * t, t), t)]

    carries = _sweep(GROUP_HEADS, t, qi, raw, fix, values, True, s_ref)
    outs = [_normalized(acc) for (_, acc) in carries]
    for p in range(GROUP_HEADS // 2):
        o_t = _merge_pair(outs[2 * p], outs[2 * p + 1])
        o_ref[:, p * LANES:(p + 1) * LANES] = o_t.T.astype(o_ref.dtype)


def _moba_attention(h_bf, v_t, bias_d, bias_p, batch, seq, cb, vrow, t=ATTN_TILE):
    nq = seq // t
    bias_spec = pl.BlockSpec((GROUP_HEADS, BIAS_BLOCK, BIAS_BLOCK), lambda b, i: (1, 0, 0))
    specs = _attn_specs(t, seq, nq, cb, vrow) + [bias_spec, bias_spec]
    return _attn_call(functools.partial(_moba_kernel, t=t, n_blocks=seq // MOBA_BLOCK), "moba_attn",
                      batch, seq, t, specs, (h_bf, h_bf, v_t, bias_d, bias_p),
                      scratch=[pltpu.VMEM((LANES, GROUP_WIDTH), F32)])


def _mla_prep_kernel(c_ref, gq_ref, gkv_ref, wqa_ref, wqb_ref, wk_ref, wvt_ref, ones_ref, e_ref,
                     tq_ref, tk_ref, q_ref, k_ref, vt_ref):
    c = c_ref[...]
    cq = c[:, :MLA_Q_LORA]
    ckv = c[:, MLA_Q_LORA:MLA_Q_LORA + MLA_KV_LORA]
    kr = c[:, MLA_Q_LORA + MLA_KV_LORA:]
    cq = cq * lax.rsqrt(jnp.mean(cq * cq, axis=-1, keepdims=True) + RMS_EPS) * gq_ref[...]
    ckv = ckv * lax.rsqrt(jnp.mean(ckv * ckv, axis=-1, keepdims=True) + RMS_EPS) * gkv_ref[...]
    cq = cq.astype(BF16)
    ckv = ckv.astype(BF16)
    tq = tq_ref[...]
    cos_q = jnp.concatenate([tq[:, :LANES]] * GROUP_HEADS, axis=1)
    sin_q = jnp.concatenate([tq[:, LANES:]] * GROUP_HEADS, axis=1)
    q = (jnp.dot(cq, wqa_ref[...], preferred_element_type=F32) * cos_q
         + jnp.dot(cq, wqb_ref[...], preferred_element_type=F32) * sin_q)
    q_ref[...] = (q * ((MLA_NOPE + MLA_ROPE) ** -0.5 * LOG2E)).astype(q_ref.dtype)
    k_rope = (kr * tk_ref[...]).astype(BF16)
    k = (jnp.dot(ckv, wk_ref[...], preferred_element_type=F32)
         + jnp.dot(k_rope, e_ref[...], preferred_element_type=F32))
    k_ref[...] = k.astype(k_ref.dtype)
    v_t = lax.dot_general(wvt_ref[...], ckv, _NT, preferred_element_type=F32) + ones_ref[...]
    vt_ref[...] = v_t.astype(vt_ref.dtype)


def _mla_prep(c_lat, gq, gkv, wqa, wqb, wk, wvt, ones, e_mat, tab_q, tab_k, seq, tm=512):
    m = c_lat.shape[0]
    ns = seq // tm
    full = lambda a: pl.BlockSpec(a.shape, lambda i: (0, 0))
    slot_w = GROUP_HEADS * LANES
    return pl.pallas_call(
        _mla_prep_kernel,
        grid=(m // tm,),
        in_specs=[pl.BlockSpec((tm, c_lat.shape[1]), lambda i: (i, 0)),
                  full(gq), full(gkv), full(wqa), full(wqb), full(wk), full(wvt), full(ones), full(e_mat),
                  pl.BlockSpec((tm, tab_q.shape[1]), lambda i: (i % ns, 0)),
                  pl.BlockSpec((tm, tab_k.shape[1]), lambda i: (i % ns, 0))],
        out_specs=(pl.BlockSpec((tm, slot_w), lambda i: (i, 0)),
                   pl.BlockSpec((tm, slot_w), lambda i: (i, 0)),
                   pl.BlockSpec((V_SLAB, tm), lambda i: (0, i))),
        out_shape=(jax.ShapeDtypeStruct((m, slot_w), BF16),
                   jax.ShapeDtypeStruct((m, slot_w), BF16),
                   jax.ShapeDtypeStruct((V_SLAB, m), BF16)),
        compiler_params=_params("arbitrary"),
        name="mla_prep",
    )(c_lat, gq, gkv, wqa, wqb, wk, wvt, ones, e_mat, tab_q, tab_k)


def _mla_kernel(q_ref, k_ref, vt_ref, o_ref, s_ref, *, t):
    qi = pl.program_id(1)
    causal = _causal_t(t)

    def raw(h, j):
        hsl = slice(h * LANES, (h + 1) * LANES)
        return lax.dot_general(k_ref[pl.ds(pl.multiple_of(j * t, t), t), hsl], q_ref[:, hsl], _NT,
                               preferred_element_type=F32)

    def fix(h, s_t, kind):
        return jnp.where(causal, s_t, NEG)

    def values(h, j):
        return vt_ref[(h // 2) * V_ROWS:(h // 2 + 1) * V_ROWS, pl.ds(pl.multiple_of(j * t, t), t)]

    carries = _sweep(GROUP_HEADS, t, qi, raw, fix, values, False, s_ref)
    outs = [_normalized(acc) for (_, acc) in carries]
    for p in range(GROUP_HEADS // 2):
        o_t = _merge_pair(outs[2 * p], outs[2 * p + 1])
        o_ref[:, p * LANES:(p + 1) * LANES] = o_t.T.astype(o_ref.dtype)


def _mla_attention(q, k, v_t, batch, seq, t=ATTN_TILE):
    nq = seq // t
    slot_w = q.shape[1]
    specs = [pl.BlockSpec((t, slot_w), lambda b, i: (b * nq + i, 0)),
             pl.BlockSpec((seq, slot_w), lambda b, i: (b, 0)),
             pl.BlockSpec((V_SLAB, seq), lambda b, i: (0, b))]
    return _attn_call(functools.partial(_mla_kernel, t=t), "mla_attn", batch, seq, t, specs, (q, k, v_t))


def _layernorm(t, g, b):
    mu = jnp.mean(t, axis=-1, keepdims=True)
    d = t - mu
    var = jnp.mean(d * d, axis=-1, keepdims=True)
    return d * lax.rsqrt(var + LN_EPS) * g + b


def _out_kernel(o1_ref, o2_ref, o3_ref, o4_ref, w_ref, x_ref, g_ref, b_ref, of_ref, ob_ref, *, alpha):
    mix = None
    for n, o_ref in enumerate((o1_ref, o2_ref, o3_ref, o4_ref)):
        part = jnp.dot(o_ref[...], w_ref[n * GROUP_WIDTH:(n + 1) * GROUP_WIDTH, :],
                       preferred_element_type=F32)
        mix = part if mix is None else mix + part
    y = _layernorm(alpha * x_ref[...] + mix, g_ref[...], b_ref[...])
    of_ref[...] = y
    ob_ref[...] = y.astype(ob_ref.dtype)


def _out_proj(outs, w_o, x, g, b, alpha, tm=512):
    m, d = x.shape
    o_spec = pl.BlockSpec((tm, GROUP_WIDTH), lambda i: (i, 0))
    row_spec = pl.BlockSpec((tm, d), lambda i: (i, 0))
    vec_spec = pl.BlockSpec((1, d), lambda i: (0, 0))
    return pl.pallas_call(
        functools.partial(_out_kernel, alpha=alpha),
        grid=(m // tm,),
        in_specs=[o_spec, o_spec, o_spec, o_spec,
                  pl.BlockSpec(w_o.shape, lambda i: (0, 0)), row_spec, vec_spec, vec_spec],
        out_specs=(row_spec, row_spec),
        out_shape=(jax.ShapeDtypeStruct((m, d), F32), jax.ShapeDtypeStruct((m, d), BF16)),
        compiler_params=_params("arbitrary"),
        name="out_proj_ln",
    )(*outs, w_o, x, g, b)


def _ffn_kernel(xb_ref, xh_ref, xf_ref, wup_ref, cp_ref, wd_ref, g_ref, b_ref,
                of_ref, ob_ref, *, alpha, tiles_per_seq, n_chunks):
    i = pl.program_id(0)
    halo = xh_ref[...]
    halo = jnp.where(i % tiles_per_seq == 0, jnp.zeros_like(halo), halo)
    xe = jnp.concatenate([halo, xb_ref[...]], axis=0)

    def cols(c, br):
        return slice(br * D_FF_PAD + c * FF_CHUNK, br * D_FF_PAD + (c + 1) * FF_CHUNK)

    def project(c):
        return [jnp.dot(xe, wup_ref[:, cols(c, br)], preferred_element_type=F32) for br in range(2)]

    def gated(c, us):
        ys = []
        for br, u in enumerate(us):
            cp = cp_ref[:, cols(c, br)]
            y = cp[2:3] * u + cp[1:2] * pltpu.roll(u, 1, 0) + cp[0:1] * pltpu.roll(u, 2, 0) + cp[3:4]
            ys.append(y[CONV_HALO:])
        return (ys[0] * jax.nn.sigmoid(ys[0]) * ys[1]).astype(BF16)

    acc = None
    u_next = project(0)
    for c in range(n_chunks):
        us = u_next
        if c + 1 < n_chunks:
            u_next = project(c + 1)
        part = jnp.dot(gated(c, us), wd_ref[c * FF_CHUNK:(c + 1) * FF_CHUNK, :], preferred_element_type=F32)
        acc = part if acc is None else acc + part
    y = _layernorm(alpha * xf_ref[...] + acc, g_ref[...], b_ref[...])
    of_ref[...] = y
    ob_ref[...] = y.astype(ob_ref.dtype)


def _ffn(xb, xf, w_up, conv_p, w_down, g, b, alpha, seq, tm=256):
    m, d = xf.shape
    n_chunks = D_FF_PAD // FF_CHUNK
    halo_blocks = tm // CONV_HALO
    row_spec = pl.BlockSpec((tm, d), lambda i: (i, 0))
    vec_spec = pl.BlockSpec((1, d), lambda i: (0, 0))
    resident = lambda a: pl.BlockSpec(a.shape, lambda i: (0, 0), pipeline_mode=pl.Buffered(1))
    return pl.pallas_call(
        functools.partial(_ffn_kernel, alpha=alpha, tiles_per_seq=seq // tm, n_chunks=n_chunks),
        grid=(m // tm,),
        in_specs=[row_spec,
                  pl.BlockSpec((CONV_HALO, d), lambda i: (jnp.maximum(i * halo_blocks - 1, 0), 0)),
                  row_spec, resident(w_up), resident(conv_p), resident(w_down), vec_spec, vec_spec],
        out_specs=(row_spec, row_spec),
        out_shape=(jax.ShapeDtypeStruct((m, d), F32), jax.ShapeDtypeStruct((m, d), BF16)),
        compiler_params=_params("arbitrary"),
        name="conv_ffn_ln",
    )(xb, xb, xf, w_up, conv_p, w_down, g, b)


def _swap_halves(w):
    half = w.shape[-1] // 2
    return jnp.concatenate([-w[..., half:], w[..., :half]], axis=-1)


def _value_slab(w_v):
    depth, k, _ = w_v.shape
    w = w_v.reshape(depth, k, GROUP_HEADS // 2, LANES)
    w = jnp.pad(w, ((0, 0), (0, 0), (0, 0), (0, V_ROWS - LANES)))
    return jnp.swapaxes(w.reshape(depth, k, V_SLAB), 1, 2)


def _ones_rows(n_slabs):
    row = jnp.arange(n_slabs * V_SLAB) % V_ROWS
    return (row == LANES).astype(F32)[:, None]


def _prep_weights(w_in, mla_w_uq, mla_w_ukv, w_up, conv_w, conv_b, w_down):
    depth, d, _ = w_in.shape
    gw = GROUP_WIDTH
    fox0 = 0
    diff0 = 3 * gw + GROUP_HEADS
    moba0 = diff0 + 3 * gw
    mla0 = moba0 + 3 * gw
    sm = HEAD_DIM ** -0.5 * LOG2E
    sd = DIFF_QK_DIM ** -0.5 * LOG2E
    w_qk = jnp.concatenate([
        w_in[:, :, fox0:fox0 + gw] * sm, w_in[:, :, fox0 + gw:fox0 + 2 * gw],
        w_in[:, :, diff0:diff0 + gw] * sd, w_in[:, :, diff0 + gw:diff0 + 2 * gw],
        w_in[:, :, moba0:moba0 + gw] * sm, w_in[:, :, moba0 + gw:moba0 + 2 * gw]], axis=-1).astype(BF16)
    w_vt = jnp.concatenate([_value_slab(w_in[:, :, g0 + 2 * gw:g0 + 3 * gw])
                            for g0 in (fox0, diff0, moba0)], axis=1).astype(BF16)
    w_gt = jnp.swapaxes(jnp.pad(w_in[:, :, 3 * gw:3 * gw + GROUP_HEADS],
                                ((0, 0), (0, 0), (0, 8 - GROUP_HEADS))), 1, 2).astype(BF16)
    kr0 = mla0 + MLA_Q_LORA + MLA_KV_LORA
    w_kr = w_in[:, :, kr0:kr0 + MLA_ROPE]
    w_lat = jnp.concatenate([w_in[:, :, mla0:kr0], w_kr, _swap_halves(w_kr),
                             jnp.zeros((depth, d, LANES - 2 * MLA_ROPE), F32)], axis=-1).astype(BF16)

    uq = mla_w_uq.reshape(depth, MLA_Q_LORA, GROUP_HEADS, MLA_NOPE + MLA_ROPE)
    pad = LANES - MLA_NOPE - MLA_ROPE
    wqa = jnp.pad(uq, ((0, 0), (0, 0), (0, 0), (0, pad)))
    wqb = jnp.concatenate([jnp.zeros_like(uq[..., :MLA_NOPE]), _swap_halves(uq[..., MLA_NOPE:]),
                           jnp.zeros(uq.shape[:-1] + (pad,), F32)], axis=-1)
    ukv = mla_w_ukv.reshape(depth, MLA_KV_LORA, GROUP_HEADS, MLA_NOPE + HEAD_DIM)
    wk = jnp.pad(ukv[..., :MLA_NOPE], ((0, 0), (0, 0), (0, 0), (0, LANES - MLA_NOPE)))
    flat = lambda a: a.reshape(depth, a.shape[1], -1).astype(BF16)
    wvt = _value_slab(flat(ukv[..., MLA_NOPE:])).astype(BF16)

    ffp = D_FF_PAD - D_FF
    padc = lambda a: jnp.pad(a, ((0, 0), (0, 0), (0, ffp)))
    w_up_p = jnp.concatenate([padc(w_up[:, :, :D_FF]), padc(w_up[:, :, D_FF:])], axis=-1).astype(BF16)
    conv = jnp.concatenate([conv_w, conv_b[:, None, :],
                            jnp.zeros((depth, 8 - CONV_WIDTH - 1, 2 * D_FF), F32)], axis=1)
    conv_p = jnp.concatenate([padc(conv[:, :, :D_FF]), padc(conv[:, :, D_FF:])], axis=-1)
    w_down_p = jnp.pad(w_down, ((0, 0), (0, ffp), (0, 0))).astype(BF16)
    return w_qk, w_vt, w_gt, w_lat, flat(wqa), flat(wqb), flat(wk), wvt, w_up_p, conv_p, w_down_p


def _rope_constants(seq):
    inv = ROPE_THETA ** (-jnp.arange(0, MLA_ROPE // 2, dtype=F32) * 2.0 / MLA_ROPE)
    ang = jnp.arange(seq, dtype=F32)[:, None] * inv[None, :]
    cos, sin = jnp.cos(ang), jnp.sin(ang)
    cos2 = jnp.concatenate([cos, cos], axis=1)
    sin2 = jnp.concatenate([sin, sin], axis=1)
    pad = jnp.zeros((seq, LANES - MLA_NOPE - MLA_ROPE), F32)
    tab_q = jnp.concatenate([jnp.ones((seq, MLA_NOPE), F32), cos2, pad,
                             jnp.zeros((seq, MLA_NOPE), F32), sin2, pad], axis=1)
    tab_k = jnp.concatenate([cos2, sin2, jnp.zeros((seq, LANES - 2 * MLA_ROPE), F32)], axis=1)
    r = jnp.arange(LANES)[:, None]
    c = jnp.arange(GROUP_HEADS * LANES)[None, :]
    e_mat = ((r < 2 * MLA_ROPE) & (c % LANES == MLA_NOPE + r % MLA_ROPE)).astype(BF16)
    return tab_q, tab_k, e_mat


def kernel(x, w_in, b_forget, diff_lambda, diff_subln, mla_q_norm, mla_kv_norm, mla_w_uq, mla_w_ukv,
           rel_bias, w_o, ln1_g, ln1_b, w_up, conv_w, conv_b, w_down, ln2_g, ln2_b):
    batch, seq, d = x.shape
    depth = w_in.shape[0]
    alpha = (2 * depth) ** 0.25
    assert seq % ATTN_TILE == 0 and ATTN_TILE % MOBA_BLOCK == 0 and seq // MOBA_BLOCK <= LANES

    (w_qk, w_vt, w_gt, w_lat, wqa, wqb, wk, wvt, w_up_p, conv_p, w_down_p) = _prep_weights(
        w_in, mla_w_uq, mla_w_ukv, w_up, conv_w, conv_b, w_down)
    w_o_b = w_o.astype(BF16)
    tab_q, tab_k, e_mat = _rope_constants(seq)
    bias_d, bias_p = _bias_blocks(rel_bias, BIAS_BLOCK)
    b_col = jnp.pad(b_forget, ((0, 0), (0, 8 - GROUP_HEADS)))[:, :, None]
    gain = jnp.tile(diff_subln, (1, GROUP_HEADS))[:, None, :]

    xf = x.reshape(batch * seq, d)
    xb = xf.astype(BF16)
    for l in range(depth):
        h_bf = _matmul(xb, w_qk[l], BF16, tm=512, tn=3 * GROUP_WIDTH)
        v_t = _matmul_nt(w_vt[l], _ones_rows(3), xb, BF16, tm=512)
        c_lat = _matmul(xb, w_lat[l], F32, tm=512, tn=w_lat.shape[-1])
        cum = _fox_cumlog(_matmul_nt(w_gt[l], jnp.zeros((8, 1), F32), xb, F32, tm=1024), b_col[l], batch, seq)
        fox_o = _fox_attention(h_bf, v_t, cum.T, batch, seq, 0, 0)
        lam_init = 0.8 - 0.6 * math.exp(-0.3 * l)
        lam_c = jnp.concatenate([jnp.full((1, LANES), lam_init, F32),
                                 jnp.full((1, LANES), 1.0 - lam_init, F32),
                                 jnp.zeros((6, LANES), F32)], axis=0)
        diff_o = _diff_attention(h_bf, v_t, bias_d, bias_p, diff_lambda[l], lam_c, gain[l], batch, seq, 2, 1)
        moba_o = _moba_attention(h_bf, v_t, bias_d, bias_p, batch, seq, 4, 2)
        q_m, k_m, v_m = _mla_prep(c_lat, mla_q_norm[l][None], mla_kv_norm[l][None],
                                  wqa[l], wqb[l], wk[l], wvt[l], _ones_rows(1), e_mat, tab_q, tab_k, seq)
        mla_o = _mla_attention(q_m, k_m, v_m, batch, seq)
        xf, xb = _out_proj((fox_o, diff_o, moba_o, mla_o), w_o_b[l], xf, ln1_g[l][None], ln1_b[l][None], alpha)
        xf, xb = _ffn(xb, xf, w_up_p[l], conv_p[l], w_down_p[l], ln2_g[l][None], ln2_b[l][None], alpha, seq)
    return xf.reshape(batch, seq, d)
```

```python
import functools
import math

import jax
import jax.numpy as jnp
from jax import lax
from jax.experimental import pallas as pl
from jax.experimental.pallas import tpu as pltpu

F32 = jnp.float32
BF16 = jnp.bfloat16

HEAD_DIM = 64
GROUP_HEADS = 4
GROUP_WIDTH = GROUP_HEADS * HEAD_DIM
LANES = 128
DIFF_QK_DIM = HEAD_DIM // 2
DIFF_SUBLN_EPS = 1e-5
MOBA_BLOCK = 256
MOBA_TOPK = 3
MLA_Q_LORA = 256
MLA_KV_LORA = 128
MLA_NOPE = 64
MLA_ROPE = 32
ROPE_THETA = 10000.0
REL_BUCKETS = 32
REL_MAX_DIST = 128
D_FF = 2752
FF_CHUNK = 256
D_FF_PAD = -(-D_FF // FF_CHUNK) * FF_CHUNK
CONV_WIDTH = 3
CONV_HALO = 16
LN_EPS = 1e-5
RMS_EPS = 1e-6
NEG = -1e30
LOG2E = 1.4426950408889634

BIAS_BLOCK = 128
ATTN_TILE = 512
V_ROWS = LANES + 16
V_SLAB = (GROUP_HEADS // 2) * V_ROWS
SCORE_LOOKAHEAD = 2
SCORE_SLOTS = 4
VMEM_LIMIT = 56 * 1024 * 1024

_NT = (((1,), (1,)), ((), ()))


def _params(*sem):
    return pltpu.CompilerParams(dimension_semantics=sem, vmem_limit_bytes=VMEM_LIMIT)


def _cum_kernel(g_ref, b_ref, o_ref):
    x = g_ref[...] + b_ref[...]
    x = jnp.minimum(x, 0.0) - jnp.log(1.0 + jnp.exp(-jnp.abs(x)))
    n = x.shape[1]
    lane = lax.broadcasted_iota(jnp.int32, x.shape, 1)
    k = 1
    while k < n:
        x = x + jnp.where(lane >= k, pltpu.roll(x, k, 1), 0.0)
        k *= 2
    o_ref[...] = x * LOG2E


def _fox_cumlog(logits_t, b_col, batch, seq):
    return pl.pallas_call(
        _cum_kernel,
        grid=(batch,),
        in_specs=[pl.BlockSpec((8, seq), lambda b: (0, b)),
                  pl.BlockSpec((8, 1), lambda b: (0, 0))],
        out_specs=pl.BlockSpec((8, seq), lambda b: (0, b)),
        out_shape=jax.ShapeDtypeStruct(logits_t.shape, F32),
        compiler_params=_params("arbitrary"),
        name="fox_cumlog",
    )(logits_t, b_col)


def _bias_kernel(rb_ref, d_ref, p_ref, *, t):
    h = pl.program_id(0)
    i = lax.broadcasted_iota(jnp.int32, (t, t), 0)
    j = lax.broadcasted_iota(jnp.int32, (t, t), 1)
    max_exact = REL_BUCKETS // 2
    far = rb_ref[(REL_BUCKETS - 1) * 8 + h]
    for out_ref, off in ((d_ref, 0), (p_ref, t)):
        dist = j - i + off
        d_large = jnp.maximum(dist, max_exact).astype(F32)
        large = max_exact + (jnp.log(d_large / max_exact) / math.log(REL_MAX_DIST / max_exact)
                             * (REL_BUCKETS - max_exact)).astype(jnp.int32)
        large = jnp.minimum(large, REL_BUCKETS - 1)
        bucket = jnp.where(dist < max_exact, dist, large)
        acc = jnp.zeros((t, t), F32)
        for bkt in range(REL_BUCKETS - 1):
            acc = jnp.where(bucket == bkt, (rb_ref[bkt * 8 + h] - far) * LOG2E, acc)
        if off == 0:
            acc = jnp.where(dist >= 0, acc, NEG)
        out_ref[0] = acc


def _bias_blocks(rel_bias, t):
    n_heads = rel_bias.shape[1]
    shape = jax.ShapeDtypeStruct((n_heads, t, t), F32)
    spec = pl.BlockSpec((1, t, t), lambda h: (h, 0, 0))
    return pl.pallas_call(
        functools.partial(_bias_kernel, t=t),
        grid=(n_heads,),
        in_specs=[pl.BlockSpec(memory_space=pltpu.SMEM)],
        out_specs=(spec, spec),
        out_shape=(shape, shape),
        compiler_params=_params("arbitrary"),
        name="rel_bias_blocks",
    )(rel_bias.reshape(-1))


def _online_update(s_t, tile_max, v_t, carry):
    m, acc = carry
    m_new = jnp.maximum(m, tile_max)
    p = jnp.exp2((s_t - m_new).astype(BF16))
    acc = jnp.exp2(m - m_new) * acc + jnp.dot(v_t, p, preferred_element_type=F32)
    return m_new, acc


def _normalized(acc):
    return acc[:LANES] / acc[LANES:LANES + 1]


def _sweep(n, t, qi, raw, fix, values, near_bias, s_ref):
    la = SCORE_LOOKAHEAD
    slots = s_ref.shape[0]
    assert n % slots == 0 and la < slots

    def produce(c, j):
        s_t = raw(c, j)
        s_ref[c % slots] = s_t
        return jnp.max(s_t, axis=0, keepdims=True)

    def step(j, state, kind):
        carries, tops = state
        tops = dict(enumerate(tops))
        nxt = []
        out = []
        for c in range(n):
            if c + la < n:
                tops[c + la] = produce(c + la, j)
            elif kind != "diag":
                nxt.append(produce(c + la - n, j + 1))
            s_t, top = s_ref[c % slots], tops.pop(c)
            if kind != "far":
                s_t = fix(c, s_t, kind)
                top = jnp.max(s_t, axis=0, keepdims=True)
            out.append(_online_update(s_t, top, values(c, j), carries[c]))
        return tuple(out), (tuple(nxt) if nxt else state[1])

    state = (tuple(_init_carry(t) for _ in range(n)), tuple(produce(c, 0) for c in range(la)))
    if near_bias:
        state = lax.fori_loop(0, jnp.maximum(qi - 1, 0), lambda j, s: step(j, s, "far"), state)
        state = lax.cond(qi >= 1, lambda s: step(qi - 1, s, "near"), lambda s: s, state)
    else:
        state = lax.fori_loop(0, qi, lambda j, s: step(j, s, "far"), state)
    return step(qi, state, "diag")[0]


def _init_carry(t):
    return (jnp.full((1, t), NEG, F32), jnp.zeros((V_ROWS, t), F32))


def _lane_select(x, lo, hi):
    lane = lax.broadcasted_iota(jnp.int32, x.shape, 1)
    return jnp.where((lane >= lo) & (lane < hi), x, jnp.zeros_like(x))


def _merge_pair(a, b):
    row = lax.broadcasted_iota(jnp.int32, a.shape, 0)
    return jnp.where(row < HEAD_DIM, a, b)


def _causal_t(t):
    return (lax.broadcasted_iota(jnp.int32, (t, t), 0) <= lax.broadcasted_iota(jnp.int32, (t, t), 1))


def _band_bias(s_t, d_t, p_t, diag):
    n = s_t.shape[0] // BIAS_BLOCK
    rows = []
    for b in range(n):
        blocks = []
        for a in range(n):
            blk = s_t[b * BIAS_BLOCK:(b + 1) * BIAS_BLOCK, a * BIAS_BLOCK:(a + 1) * BIAS_BLOCK]
            if diag and b > a:
                blk = jnp.full_like(blk, NEG)
            elif diag and b == a:
                blk = blk + d_t
            elif (diag and b == a - 1) or (not diag and b == n - 1 and a == 0):
                blk = blk + p_t
            blocks.append(blk)
        rows.append(jnp.concatenate(blocks, axis=1))
    return jnp.concatenate(rows, axis=0)


def _attn_specs(t, seq, nq, cb, vrow):
    w = GROUP_WIDTH
    return [pl.BlockSpec((t, w), lambda b, i: (b * nq + i, cb)),
            pl.BlockSpec((seq, w), lambda b, i: (b, cb + 1)),
            pl.BlockSpec((V_SLAB, seq), lambda b, i: (vrow, b))]


def _attn_call(kernel_fn, name, batch, seq, t, in_specs, args, scratch=()):
    nq = seq // t
    return pl.pallas_call(
        kernel_fn,
        grid=(batch, nq),
        in_specs=in_specs,
        out_specs=pl.BlockSpec((t, GROUP_WIDTH), lambda b, i: (b * nq + i, 0)),
        out_shape=jax.ShapeDtypeStruct((batch * seq, GROUP_WIDTH), BF16),
        scratch_shapes=[pltpu.VMEM((SCORE_SLOTS, t, t), F32)] + list(scratch),
        compiler_params=_params("arbitrary", "arbitrary"),
        name=name,
    )(*args)


def _head_queries(q_ref, width):
    out = []
    for lo in range(0, GROUP_WIDTH, width):
        p = lo // LANES
        out.append(_lane_select(q_ref[:, p * LANES:(p + 1) * LANES], lo - p * LANES, lo - p * LANES + width))
    return out


def _fox_kernel(q_ref, k_ref, vt_ref, c_ref, o_ref, s_ref, *, t):
    qi = pl.program_id(1)
    causal = _causal_t(t)
    qms = _head_queries(q_ref, HEAD_DIM)

    def raw(h, j):
        sl = slice((h // 2) * LANES, (h // 2 + 1) * LANES)
        start = pl.multiple_of(j * t, t)
        s_t = lax.dot_general(k_ref[pl.ds(start, t), sl], qms[h], _NT, preferred_element_type=F32)
        return s_t - c_ref[pl.ds(start, t), h:h + 1]

    def fix(h, s_t, kind):
        return jnp.where(causal, s_t, NEG)

    def values(h, j):
        return vt_ref[(h // 2) * V_ROWS:(h // 2 + 1) * V_ROWS, pl.ds(pl.multiple_of(j * t, t), t)]

    carries = _sweep(GROUP_HEADS, t, qi, raw, fix, values, False, s_ref)
    outs = [_normalized(acc) for (_, acc) in carries]
    for p in range(GROUP_HEADS // 2):
        o_t = _merge_pair(outs[2 * p], outs[2 * p + 1])
        o_ref[:, p * LANES:(p + 1) * LANES] = o_t.T.astype(o_ref.dtype)


def _fox_attention(h_bf, v_t, cum_cols, batch, seq, cb, vrow, t=ATTN_TILE):
    nq = seq // t
    specs = _attn_specs(t, seq, nq, cb, vrow) + [pl.BlockSpec((seq, 8), lambda b, i: (b, 0))]
    return _attn_call(functools.partial(_fox_kernel, t=t), "fox_attn", batch, seq, t, specs,
                      (h_bf, h_bf, v_t, cum_cols))


def _diff_kernel(q_ref, k_ref, vt_ref, bd_ref, bp_ref, lam_ref, lc_ref, g_ref, o_ref, s_ref, *, t):
    qi = pl.program_id(1)
    lam_p = lam_ref[...]
    lam = (jnp.exp(jnp.sum(lam_p[0:1] * lam_p[1:2], axis=-1, keepdims=True))
           - jnp.exp(jnp.sum(lam_p[2:3] * lam_p[3:4], axis=-1, keepdims=True)) + lc_ref[0:1, 0:1])
    qms = _head_queries(q_ref, DIFF_QK_DIM)

    def raw(c, j):
        sl = slice((c // 4) * LANES, (c // 4 + 1) * LANES)
        return lax.dot_general(k_ref[pl.ds(pl.multiple_of(j * t, t), t), sl], qms[c], _NT,
                               preferred_element_type=F32)

    def fix(c, s_t, kind):
        return _band_bias(s_t, bd_ref[c // 2], bp_ref[c // 2], kind == "diag")

    def values(c, j):
        return vt_ref[(c // 4) * V_ROWS:(c // 4 + 1) * V_ROWS, pl.ds(pl.multiple_of(j * t, t), t)]

    carries = _sweep(2 * GROUP_HEADS, t, qi, raw, fix, values, True, s_ref)
    row = lax.broadcasted_iota(jnp.int32, (LANES, t), 0)
    outs = []
    for h in range(GROUP_HEADS):
        o = _normalized(carries[2 * h][1]) - lam * _normalized(carries[2 * h + 1][1])
        lo = (h % 2) * HEAD_DIM
        o = jnp.where((row >= lo) & (row < lo + HEAD_DIM), o, 0.0)
        ms = jnp.sum(o * o, axis=0, keepdims=True) * (1.0 / HEAD_DIM)
        outs.append(o * lax.rsqrt(ms + DIFF_SUBLN_EPS))
    for p in range(GROUP_HEADS // 2):
        sl = slice(p * LANES, (p + 1) * LANES)
        o_t = _merge_pair(outs[2 * p], outs[2 * p + 1])
        o_ref[:, sl] = (o_t.T * g_ref[:, sl] * lc_ref[1:2, 0:1]).astype(o_ref.dtype)


def _diff_attention(h_bf, v_t, bias_d, bias_p, lam_p, lam_c, gain, batch, seq, cb, vrow, t=ATTN_TILE):
    nq = seq // t
    bias_spec = pl.BlockSpec((GROUP_HEADS, BIAS_BLOCK, BIAS_BLOCK), lambda b, i: (0, 0, 0))
    full = lambda a: pl.BlockSpec(a.shape, lambda b, i: (0, 0))
    specs = _attn_specs(t, seq, nq, cb, vrow) + [bias_spec, bias_spec, full(lam_p), full(lam_c), full(gain)]
    return _attn_call(functools.partial(_diff_kernel, t=t), "diff_attn", batch, seq, t, specs,
                      (h_bf, h_bf, v_t, bias_d, bias_p, lam_p, lam_c, gain))


def _moba_kernel(q_ref, k_ref, vt_ref, bd_ref, bp_ref, o_ref, s_ref, kmean_ref, *, t, n_blocks):
    qi = pl.program_id(1)
    per_tile = t // MOBA_BLOCK

    @pl.when(qi == 0)
    def _():
        kmean_ref[...] = jnp.zeros_like(kmean_ref)
        for blk in range(n_blocks):
            kb = k_ref[blk * MOBA_BLOCK:(blk + 1) * MOBA_BLOCK, :].astype(F32)
            kmean_ref[blk:blk + 1, :] = jnp.sum(kb, axis=0, keepdims=True) * (1.0 / MOBA_BLOCK)

    blk_i = lax.broadcasted_iota(jnp.int32, (t, LANES), 1)
    blk_f = blk_i.astype(F32)
    row_i = lax.broadcasted_iota(jnp.int32, (t, LANES), 0)
    row_blk = sum((row_i >= n * MOBA_BLOCK).astype(jnp.int32) for n in range(1, per_tile))
    own = qi * per_tile + row_blk
    qms = _head_queries(q_ref, HEAD_DIM)
    q_cats = []
    for h in range(GROUP_HEADS):
        sl = slice((h // 2) * LANES, (h // 2 + 1) * LANES)
        gate = lax.dot_general(qms[h].astype(F32), kmean_ref[:, sl], _NT,
                               precision=lax.Precision.HIGHEST, preferred_element_type=F32)
        g = jnp.where(blk_i < own, gate, NEG)
        keep = jnp.where(blk_i == own, 1.0, 0.0)
        for _ in range(MOBA_TOPK):
            mx = jnp.max(g, axis=-1, keepdims=True)
            first = jnp.min(jnp.where(g == mx, blk_f, 1e9), axis=-1, keepdims=True)
            pick = jnp.where(mx > 0.5 * NEG, jnp.where(blk_f == first, 1.0, 0.0), 0.0)
            keep = jnp.maximum(keep, pick)
            g = jnp.where(pick > 0.0, NEG, g)
        q_cats.append(jnp.concatenate([qms[h], (1.0 - keep).astype(BF16)], axis=1))

    key_blk = blk_i - row_blk

    def raw(h, j):
        sl = slice((h // 2) * LANES, (h // 2 + 1) * LANES)
        reject = jnp.where(key_blk == j * per_tile, NEG, 0.0).astype(BF16)
        k_cat = jnp.concatenate([k_ref[pl.ds(pl.multiple_of(j * t, t), t), sl], reject], axis=1)
        return lax.dot_general(k_cat, q_cats[h], _NT, preferred_element_type=F32)

    def fix(h, s_t, kind):
        return _band_bias(s_t, bd_ref[h], bp_ref[h], kind == "diag")

    def values(h, j):
        return vt_ref[(h // 2) * V_ROWS:(h // 2 + 1) * V_ROWS, pl.ds(pl.multiple_of(j * t, t), t)]

    carries = _sweep(GROUP_HEADS, t, qi, raw, fix, values, True, s_ref)
    outs = [_normalized(acc) for (_, acc) in carries]
    for p in range(GROUP_HEADS // 2):
        o_t = _merge_pair(outs[2 * p], outs[2 * p + 1])
        o_ref[:, p * LANES:(p + 1) * LANES] = o_t.T.astype(o_ref.dtype)


def _moba_attention(h_bf, v_t, bias_d, bias_p, batch, seq, cb, vrow, t=ATTN_TILE):
    nq = seq // t
    bias_spec = pl.BlockSpec((GROUP_HEADS, BIAS_BLOCK, BIAS_BLOCK), lambda b, i: (1, 0, 0))
    specs = _attn_specs(t, seq, nq, cb, vrow) + [bias_spec, bias_spec]
    return _attn_call(functools.partial(_moba_kernel, t=t, n_blocks=seq // MOBA_BLOCK), "moba_attn",
                      batch, seq, t, specs, (h_bf, h_bf, v_t, bias_d, bias_p),
                      scratch=[pltpu.VMEM((LANES, GROUP_WIDTH), F32)])


def _in_proj_kernel(x_ref, wlat_ref, wqk_ref, wvt3_ref, ones3_ref, wgt_ref,
                    gq_ref, gkv_ref, wqa_ref, wqb_ref, wk_ref, wvt_ref, ones_ref, e_ref, tq_ref, tk_ref,
                    h_ref, vt3_ref, gate_ref, q_ref, k_ref, vt_ref):
    x = x_ref[...]
    c = jnp.dot(x, wlat_ref[...], preferred_element_type=F32)
    h_ref[...] = jnp.dot(x, wqk_ref[...], preferred_element_type=F32).astype(h_ref.dtype)
    v3 = lax.dot_general(wvt3_ref[...], x, _NT, preferred_element_type=F32) + ones3_ref[...]
    vt3_ref[...] = v3.astype(vt3_ref.dtype)
    gate_ref[...] = lax.dot_general(wgt_ref[...], x, _NT, preferred_element_type=F32)
    cq = c[:, :MLA_Q_LORA]
    ckv = c[:, MLA_Q_LORA:MLA_Q_LORA + MLA_KV_LORA]
    kr = c[:, MLA_Q_LORA + MLA_KV_LORA:]
    cq = cq * lax.rsqrt(jnp.mean(cq * cq, axis=-1, keepdims=True) + RMS_EPS) * gq_ref[...]
    ckv = ckv * lax.rsqrt(jnp.mean(ckv * ckv, axis=-1, keepdims=True) + RMS_EPS) * gkv_ref[...]
    cq = cq.astype(BF16)
    ckv = ckv.astype(BF16)
    tq = tq_ref[...]
    cos_q = jnp.concatenate([tq[:, :LANES]] * GROUP_HEADS, axis=1)
    sin_q = jnp.concatenate([tq[:, LANES:]] * GROUP_HEADS, axis=1)
    q = (jnp.dot(cq, wqa_ref[...], preferred_element_type=F32) * cos_q
         + jnp.dot(cq, wqb_ref[...], preferred_element_type=F32) * sin_q)
    q_ref[...] = (q * ((MLA_NOPE + MLA_ROPE) ** -0.5 * LOG2E)).astype(q_ref.dtype)
    k_rope = (kr * tk_ref[...]).astype(BF16)
    k = (jnp.dot(ckv, wk_ref[...], preferred_element_type=F32)
         + jnp.dot(k_rope, e_ref[...], preferred_element_type=F32))
    k_ref[...] = k.astype(k_ref.dtype)
    v_t = lax.dot_general(wvt_ref[...], ckv, _NT, preferred_element_type=F32) + ones_ref[...]
    vt_ref[...] = v_t.astype(vt_ref.dtype)


def _in_proj(xb, w_lat, w_qk, w_vt3, ones3, w_gt, gq, gkv, wqa, wqb, wk, wvt, ones, e_mat, tab_q, tab_k,
             seq, tm=512):
    m, d = xb.shape
    ns = seq // tm
    full = lambda a: pl.BlockSpec(a.shape, lambda i: (0, 0), pipeline_mode=pl.Buffered(1))
    rows = lambda w: pl.BlockSpec((tm, w), lambda i: (i, 0))
    cols = lambda r: pl.BlockSpec((r, tm), lambda i: (0, i))
    slot_w = GROUP_HEADS * LANES
    consts = (w_lat, w_qk, w_vt3, ones3, w_gt, gq, gkv, wqa, wqb, wk, wvt, ones, e_mat)
    return pl.pallas_call(
        _in_proj_kernel,
        grid=(m // tm,),
        in_specs=[rows(d)] + [full(a) for a in consts]
                 + [pl.BlockSpec((tm, tab_q.shape[1]), lambda i: (i % ns, 0)),
                    pl.BlockSpec((tm, tab_k.shape[1]), lambda i: (i % ns, 0))],
        out_specs=(rows(w_qk.shape[1]), cols(w_vt3.shape[0]), cols(w_gt.shape[0]),
                   rows(slot_w), rows(slot_w), cols(V_SLAB)),
        out_shape=(jax.ShapeDtypeStruct((m, w_qk.shape[1]), BF16),
                   jax.ShapeDtypeStruct((w_vt3.shape[0], m), BF16),
                   jax.ShapeDtypeStruct((w_gt.shape[0], m), F32),
                   jax.ShapeDtypeStruct((m, slot_w), BF16),
                   jax.ShapeDtypeStruct((m, slot_w), BF16),
                   jax.ShapeDtypeStruct((V_SLAB, m), BF16)),
        compiler_params=_params("arbitrary"),
        name="in_proj",
    )(xb, *consts, tab_q, tab_k)


def _mla_kernel(q_ref, k_ref, vt_ref, o_ref, s_ref, *, t):
    qi = pl.program_id(1)
    causal = _causal_t(t)

    def raw(h, j):
        hsl = slice(h * LANES, (h + 1) * LANES)
        return lax.dot_general(k_ref[pl.ds(pl.multiple_of(j * t, t), t), hsl], q_ref[:, hsl], _NT,
                               preferred_element_type=F32)

    def fix(h, s_t, kind):
        return jnp.where(causal, s_t, NEG)

    def values(h, j):
        return vt_ref[(h // 2) * V_ROWS:(h // 2 + 1) * V_ROWS, pl.ds(pl.multiple_of(j * t, t), t)]

    carries = _sweep(GROUP_HEADS, t, qi, raw, fix, values, False, s_ref)
    outs = [_normalized(acc) for (_, acc) in carries]
    for p in range(GROUP_HEADS // 2):
        o_t = _merge_pair(outs[2 * p], outs[2 * p + 1])
        o_ref[:, p * LANES:(p + 1) * LANES] = o_t.T.astype(o_ref.dtype)


def _mla_attention(q, k, v_t, batch, seq, t=ATTN_TILE):
    nq = seq // t
    slot_w = q.shape[1]
    specs = [pl.BlockSpec((t, slot_w), lambda b, i: (b * nq + i, 0)),
             pl.BlockSpec((seq, slot_w), lambda b, i: (b, 0)),
             pl.BlockSpec((V_SLAB, seq), lambda b, i: (0, b))]
    return _attn_call(functools.partial(_mla_kernel, t=t), "mla_attn", batch, seq, t, specs, (q, k, v_t))


def _layernorm(t, g, b):
    mu = jnp.mean(t, axis=-1, keepdims=True)
    d = t - mu
    var = jnp.mean(d * d, axis=-1, keepdims=True)
    return d * lax.rsqrt(var + LN_EPS) * g + b


def _out_kernel(o1_ref, o2_ref, o3_ref, o4_ref, w_ref, x_ref, g_ref, b_ref, of_ref, ob_ref, *, alpha):
    mix = None
    for n, o_ref in enumerate((o1_ref, o2_ref, o3_ref, o4_ref)):
        part = jnp.dot(o_ref[...], w_ref[n * GROUP_WIDTH:(n + 1) * GROUP_WIDTH, :],
                       preferred_element_type=F32)
        mix = part if mix is None else mix + part
    y = _layernorm(alpha * x_ref[...] + mix, g_ref[...], b_ref[...])
    of_ref[...] = y
    ob_ref[...] = y.astype(ob_ref.dtype)


def _out_proj(outs, w_o, x, g, b, alpha, tm=512):
    m, d = x.shape
    o_spec = pl.BlockSpec((tm, GROUP_WIDTH), lambda i: (i, 0))
    row_spec = pl.BlockSpec((tm, d), lambda i: (i, 0))
    vec_spec = pl.BlockSpec((1, d), lambda i: (0, 0))
    return pl.pallas_call(
        functools.partial(_out_kernel, alpha=alpha),
        grid=(m // tm,),
        in_specs=[o_spec, o_spec, o_spec, o_spec,
                  pl.BlockSpec(w_o.shape, lambda i: (0, 0)), row_spec, vec_spec, vec_spec],
        out_specs=(row_spec, row_spec),
        out_shape=(jax.ShapeDtypeStruct((m, d), F32), jax.ShapeDtypeStruct((m, d), BF16)),
        compiler_params=_params("arbitrary"),
        name="out_proj_ln",
    )(*outs, w_o, x, g, b)


def _ffn_kernel(xb_ref, xh_ref, xf_ref, wup_ref, cp_ref, wd_ref, g_ref, b_ref,
                of_ref, ob_ref, *, alpha, tiles_per_seq, n_chunks):
    i = pl.program_id(0)
    halo = xh_ref[...]
    halo = jnp.where(i % tiles_per_seq == 0, jnp.zeros_like(halo), halo)
    xe = jnp.concatenate([halo, xb_ref[...]], axis=0)

    def cols(c, br):
        return slice(br * D_FF_PAD + c * FF_CHUNK, br * D_FF_PAD + (c + 1) * FF_CHUNK)

    def project(c):
        return [jnp.dot(xe, wup_ref[:, cols(c, br)], preferred_element_type=F32) for br in range(2)]

    def gated(c, us):
        ys = []
        for br, u in enumerate(us):
            cp = cp_ref[:, cols(c, br)]
            y = cp[2:3] * u + cp[1:2] * pltpu.roll(u, 1, 0) + cp[0:1] * pltpu.roll(u, 2, 0) + cp[3:4]
            ys.append(y[CONV_HALO:])
        return (ys[0] * jax.nn.sigmoid(ys[0]) * ys[1]).astype(BF16)

    acc = None
    u_next = project(0)
    for c in range(n_chunks):
        us = u_next
        if c + 1 < n_chunks:
            u_next = project(c + 1)
        part = jnp.dot(gated(c, us), wd_ref[c * FF_CHUNK:(c + 1) * FF_CHUNK, :], preferred_element_type=F32)
        acc = part if acc is None else acc + part
    y = _layernorm(alpha * xf_ref[...] + acc, g_ref[...], b_ref[...])
    of_ref[...] = y
    ob_ref[...] = y.astype(ob_ref.dtype)


def _ffn(xb, xf, w_up, conv_p, w_down, g, b, alpha, seq, tm=256):
    m, d = xf.shape
    n_chunks = D_FF_PAD // FF_CHUNK
    halo_blocks = tm // CONV_HALO
    row_spec = pl.BlockSpec((tm, d), lambda i: (i, 0))
    vec_spec = pl.BlockSpec((1, d), lambda i: (0, 0))
    resident = lambda a: pl.BlockSpec(a.shape, lambda i: (0, 0), pipeline_mode=pl.Buffered(1))
    return pl.pallas_call(
        functools.partial(_ffn_kernel, alpha=alpha, tiles_per_seq=seq // tm, n_chunks=n_chunks),
        grid=(m // tm,),
        in_specs=[row_spec,
                  pl.BlockSpec((CONV_HALO, d), lambda i: (jnp.maximum(i * halo_blocks - 1, 0), 0)),
                  row_spec, resident(w_up), resident(conv_p), resident(w_down), vec_spec, vec_spec],
        out_specs=(row_spec, row_spec),
        out_shape=(jax.ShapeDtypeStruct((m, d), F32), jax.ShapeDtypeStruct((m, d), BF16)),
        compiler_params=_params("arbitrary"),
        name="conv_ffn_ln",
    )(xb, xb, xf, w_up, conv_p, w_down, g, b)


def _swap_halves(w):
    half = w.shape[-1] // 2
    return jnp.concatenate([-w[..., half:], w[..., :half]], axis=-1)


def _value_slab(w_v):
    depth, k, _ = w_v.shape
    w = w_v.reshape(depth, k, GROUP_HEADS // 2, LANES)
    w = jnp.pad(w, ((0, 0), (0, 0), (0, 0), (0, V_ROWS - LANES)))
    return jnp.swapaxes(w.reshape(depth, k, V_SLAB), 1, 2)


def _ones_rows(n_slabs):
    row = jnp.arange(n_slabs * V_SLAB) % V_ROWS
    return (row == LANES).astype(F32)[:, None]


def _prep_weights(w_in, mla_w_uq, mla_w_ukv, w_up, conv_w, conv_b, w_down):
    depth, d, _ = w_in.shape
    gw = GROUP_WIDTH
    fox0 = 0
    diff0 = 3 * gw + GROUP_HEADS
    moba0 = diff0 + 3 * gw
    mla0 = moba0 + 3 * gw
    sm = HEAD_DIM ** -0.5 * LOG2E
    sd = DIFF_QK_DIM ** -0.5 * LOG2E
    w_qk = jnp.concatenate([
        w_in[:, :, fox0:fox0 + gw] * sm, w_in[:, :, fox0 + gw:fox0 + 2 * gw],
        w_in[:, :, diff0:diff0 + gw] * sd, w_in[:, :, diff0 + gw:diff0 + 2 * gw],
        w_in[:, :, moba0:moba0 + gw] * sm, w_in[:, :, moba0 + gw:moba0 + 2 * gw]], axis=-1).astype(BF16)
    w_vt = jnp.concatenate([_value_slab(w_in[:, :, g0 + 2 * gw:g0 + 3 * gw])
                            for g0 in (fox0, diff0, moba0)], axis=1).astype(BF16)
    w_gt = jnp.swapaxes(jnp.pad(w_in[:, :, 3 * gw:3 * gw + GROUP_HEADS],
                                ((0, 0), (0, 0), (0, 8 - GROUP_HEADS))), 1, 2).astype(BF16)
    kr0 = mla0 + MLA_Q_LORA + MLA_KV_LORA
    w_kr = w_in[:, :, kr0:kr0 + MLA_ROPE]
    w_lat = jnp.concatenate([w_in[:, :, mla0:kr0], w_kr, _swap_halves(w_kr),
                             jnp.zeros((depth, d, LANES - 2 * MLA_ROPE), F32)], axis=-1).astype(BF16)

    uq = mla_w_uq.reshape(depth, MLA_Q_LORA, GROUP_HEADS, MLA_NOPE + MLA_ROPE)
    pad = LANES - MLA_NOPE - MLA_ROPE
    wqa = jnp.pad(uq, ((0, 0), (0, 0), (0, 0), (0, pad)))
    wqb = jnp.concatenate([jnp.zeros_like(uq[..., :MLA_NOPE]), _swap_halves(uq[..., MLA_NOPE:]),
                           jnp.zeros(uq.shape[:-1] + (pad,), F32)], axis=-1)
    ukv = mla_w_ukv.reshape(depth, MLA_KV_LORA, GROUP_HEADS, MLA_NOPE + HEAD_DIM)
    wk = jnp.pad(ukv[..., :MLA_NOPE], ((0, 0), (0, 0), (0, 0), (0, LANES - MLA_NOPE)))
    flat = lambda a: a.reshape(depth, a.shape[1], -1).astype(BF16)
    wvt = _value_slab(flat(ukv[..., MLA_NOPE:])).astype(BF16)

    ffp = D_FF_PAD - D_FF
    padc = lambda a: jnp.pad(a, ((0, 0), (0, 0), (0, ffp)))
    w_up_p = jnp.concatenate([padc(w_up[:, :, :D_FF]), padc(w_up[:, :, D_FF:])], axis=-1).astype(BF16)
    conv = jnp.concatenate([conv_w, conv_b[:, None, :],
                            jnp.zeros((depth, 8 - CONV_WIDTH - 1, 2 * D_FF), F32)], axis=1)
    conv_p = jnp.concatenate([padc(conv[:, :, :D_FF]), padc(conv[:, :, D_FF:])], axis=-1)
    w_down_p = jnp.pad(w_down, ((0, 0), (0, ffp), (0, 0))).astype(BF16)
    return w_qk, w_vt, w_gt, w_lat, flat(wqa), flat(wqb), flat(wk), wvt, w_up_p, conv_p, w_down_p


def _rope_constants(seq):
    inv = ROPE_THETA ** (-jnp.arange(0, MLA_ROPE // 2, dtype=F32) * 2.0 / MLA_ROPE)
    ang = jnp.arange(seq, dtype=F32)[:, None] * inv[None, :]
    cos, sin = jnp.cos(ang), jnp.sin(ang)
    cos2 = jnp.concatenate([cos, cos], axis=1)
    sin2 = jnp.concatenate([sin, sin], axis=1)
    pad = jnp.zeros((seq, LANES - MLA_NOPE - MLA_ROPE), F32)
    tab_q = jnp.concatenate([jnp.ones((seq, MLA_NOPE), F32), cos2, pad,
                             jnp.zeros((seq, MLA_NOPE), F32), sin2, pad], axis=1)
    tab_k = jnp.concatenate([cos2, sin2, jnp.zeros((seq, LANES - 2 * MLA_ROPE), F32)], axis=1)
    r = jnp.arange(LANES)[:, None]
    c = jnp.arange(GROUP_HEADS * LANES)[None, :]
    e_mat = ((r < 2 * MLA_ROPE) & (c % LANES == MLA_NOPE + r % MLA_ROPE)).astype(BF16)
    return tab_q, tab_k, e_mat


def kernel(x, w_in, b_forget, diff_lambda, diff_subln, mla_q_norm, mla_kv_norm, mla_w_uq, mla_w_ukv,
           rel_bias, w_o, ln1_g, ln1_b, w_up, conv_w, conv_b, w_down, ln2_g, ln2_b):
    batch, seq, d = x.shape
    depth = w_in.shape[0]
    alpha = (2 * depth) ** 0.25
    assert seq % ATTN_TILE == 0 and ATTN_TILE % MOBA_BLOCK == 0 and seq // MOBA_BLOCK <= LANES

    (w_qk, w_vt, w_gt, w_lat, wqa, wqb, wk, wvt, w_up_p, conv_p, w_down_p) = _prep_weights(
        w_in, mla_w_uq, mla_w_ukv, w_up, conv_w, conv_b, w_down)
    w_o_b = w_o.astype(BF16)
    tab_q, tab_k, e_mat = _rope_constants(seq)
    bias_d, bias_p = _bias_blocks(rel_bias, BIAS_BLOCK)
    b_col = jnp.pad(b_forget, ((0, 0), (0, 8 - GROUP_HEADS)))[:, :, None]
    gain = jnp.tile(diff_subln, (1, GROUP_HEADS))[:, None, :]

    xf = x.reshape(batch * seq, d)
    xb = xf.astype(BF16)
    for l in range(depth):
        h_bf, v_t, gate_t, q_m, k_m, v_m = _in_proj(
            xb, w_lat[l], w_qk[l], w_vt[l], _ones_rows(3), w_gt[l], mla_q_norm[l][None], mla_kv_norm[l][None],
            wqa[l], wqb[l], wk[l], wvt[l], _ones_rows(1), e_mat, tab_q, tab_k, seq)
        cum = _fox_cumlog(gate_t, b_col[l], batch, seq)
        fox_o = _fox_attention(h_bf, v_t, cum.T, batch, seq, 0, 0)
        lam_init = 0.8 - 0.6 * math.exp(-0.3 * l)
        lam_c = jnp.concatenate([jnp.full((1, LANES), lam_init, F32),
                                 jnp.full((1, LANES), 1.0 - lam_init, F32),
                                 jnp.zeros((6, LANES), F32)], axis=0)
        diff_o = _diff_attention(h_bf, v_t, bias_d, bias_p, diff_lambda[l], lam_c, gain[l], batch, seq, 2, 1)
        moba_o = _moba_attention(h_bf, v_t, bias_d, bias_p, batch, seq, 4, 2)
        mla_o = _mla_attention(q_m, k_m, v_m, batch, seq)
        xf, xb = _out_proj((fox_o, diff_o, moba_o, mla_o), w_o_b[l], xf, ln1_g[l][None], ln1_b[l][None], alpha)
        xf, xb = _ffn(xb, xf, w_up_p[l], conv_p[l], w_down_p[l], ln2_g[l][None], ln2_b[l][None], alpha, seq)
    return xf.reshape(batch, seq, d)
```

```python
import functools
import math

import jax
import jax.numpy as jnp
from jax import lax
from jax.experimental import pallas as pl
from jax.experimental.pallas import tpu as pltpu

F32 = jnp.float32
BF16 = jnp.bfloat16

HEAD_DIM = 64
GROUP_HEADS = 4
GROUP_WIDTH = GROUP_HEADS * HEAD_DIM
LANES = 128
DIFF_QK_DIM = HEAD_DIM // 2
DIFF_SUBLN_EPS = 1e-5
MOBA_BLOCK = 256
MOBA_TOPK = 3
MLA_Q_LORA = 256
MLA_KV_LORA = 128
MLA_NOPE = 64
MLA_ROPE = 32
ROPE_THETA = 10000.0
REL_BUCKETS = 32
REL_MAX_DIST = 128
D_FF = 2752
FF_CHUNK = 256
D_FF_PAD = -(-D_FF // FF_CHUNK) * FF_CHUNK
CONV_WIDTH = 3
CONV_HALO = 16
LN_EPS = 1e-5
RMS_EPS = 1e-6
NEG = -1e30
LOG2E = 1.4426950408889634

BIAS_BLOCK = 128
ATTN_TILE = 512
V_ROWS = LANES + 16
V_SLAB = (GROUP_HEADS // 2) * V_ROWS
SCORE_LOOKAHEAD = 3
SCORE_SLOTS = 4
VMEM_LIMIT = 56 * 1024 * 1024

_NT = (((1,), (1,)), ((), ()))


def _params(*sem):
    return pltpu.CompilerParams(dimension_semantics=sem, vmem_limit_bytes=VMEM_LIMIT)


def _cum_kernel(g_ref, b_ref, o_ref):
    x = g_ref[...] + b_ref[...]
    x = jnp.minimum(x, 0.0) - jnp.log(1.0 + jnp.exp(-jnp.abs(x)))
    n = x.shape[1]
    lane = lax.broadcasted_iota(jnp.int32, x.shape, 1)
    k = 1
    while k < n:
        x = x + jnp.where(lane >= k, pltpu.roll(x, k, 1), 0.0)
        k *= 2
    o_ref[...] = x * LOG2E


def _fox_cumlog(logits_t, b_col, batch, seq):
    return pl.pallas_call(
        _cum_kernel,
        grid=(batch,),
        in_specs=[pl.BlockSpec((8, seq), lambda b: (0, b)),
                  pl.BlockSpec((8, 1), lambda b: (0, 0))],
        out_specs=pl.BlockSpec((8, seq), lambda b: (0, b)),
        out_shape=jax.ShapeDtypeStruct(logits_t.shape, F32),
        compiler_params=_params("arbitrary"),
        name="fox_cumlog",
    )(logits_t, b_col)


def _bias_kernel(rb_ref, d_ref, p_ref, *, t):
    h = pl.program_id(0)
    i = lax.broadcasted_iota(jnp.int32, (t, t), 0)
    j = lax.broadcasted_iota(jnp.int32, (t, t), 1)
    max_exact = REL_BUCKETS // 2
    far = rb_ref[(REL_BUCKETS - 1) * 8 + h]
    for out_ref, off in ((d_ref, 0), (p_ref, t)):
        dist = j - i + off
        d_large = jnp.maximum(dist, max_exact).astype(F32)
        large = max_exact + (jnp.log(d_large / max_exact) / math.log(REL_MAX_DIST / max_exact)
                             * (REL_BUCKETS - max_exact)).astype(jnp.int32)
        large = jnp.minimum(large, REL_BUCKETS - 1)
        bucket = jnp.where(dist < max_exact, dist, large)
        acc = jnp.zeros((t, t), F32)
        for bkt in range(REL_BUCKETS - 1):
            acc = jnp.where(bucket == bkt, (rb_ref[bkt * 8 + h] - far) * LOG2E, acc)
        if off == 0:
            acc = jnp.where(dist >= 0, acc, NEG)
        out_ref[0] = acc


def _bias_blocks(rel_bias, t):
    n_heads = rel_bias.shape[1]
    shape = jax.ShapeDtypeStruct((n_heads, t, t), F32)
    spec = pl.BlockSpec((1, t, t), lambda h: (h, 0, 0))
    return pl.pallas_call(
        functools.partial(_bias_kernel, t=t),
        grid=(n_heads,),
        in_specs=[pl.BlockSpec(memory_space=pltpu.SMEM)],
        out_specs=(spec, spec),
        out_shape=(shape, shape),
        compiler_params=_params("arbitrary"),
        name="rel_bias_blocks",
    )(rel_bias.reshape(-1))


def _online_update(s_t, tile_max, v_t, carry):
    m, acc = carry
    m_new = jnp.maximum(m, tile_max)
    p = jnp.exp2((s_t - m_new).astype(BF16))
    acc = jnp.exp2(m - m_new) * acc + jnp.dot(v_t, p, preferred_element_type=F32)
    return m_new, acc


def _normalized(acc):
    return acc[:LANES] / acc[LANES:LANES + 1]


def _sweep(n, t, qi, raw, fix, values, near_bias, s_ref):
    la = SCORE_LOOKAHEAD
    slots = s_ref.shape[0]
    assert n % slots == 0 and la < slots

    def produce(c, j):
        s_t = raw(c, j)
        s_ref[c % slots] = s_t
        return jnp.max(s_t, axis=0, keepdims=True)

    def step(j, state, kind):
        carries, tops = state
        tops = dict(enumerate(tops))
        nxt = []
        out = []
        for c in range(n):
            if c + la < n:
                tops[c + la] = produce(c + la, j)
            elif kind != "diag":
                nxt.append(produce(c + la - n, j + 1))
            s_t, top = s_ref[c % slots], tops.pop(c)
            if kind != "far":
                s_t = fix(c, s_t, kind)
                top = jnp.max(s_t, axis=0, keepdims=True)
            out.append(_online_update(s_t, top, values(c, j), carries[c]))
        return tuple(out), (tuple(nxt) if nxt else state[1])

    def far_steps(count, state):
        def pair(i, s):
            return step(2 * i + 1, step(2 * i, s, "far"), "far")

        state = lax.fori_loop(0, lax.shift_right_logical(count, 1), pair, state)
        return lax.cond(jnp.bitwise_and(count, 1) == 1,
                        lambda s: step(count - 1, s, "far"), lambda s: s, state)

    state = (tuple(_init_carry(t) for _ in range(n)), tuple(produce(c, 0) for c in range(la)))
    if near_bias:
        state = far_steps(jnp.maximum(qi - 1, 0), state)
        state = lax.cond(qi >= 1, lambda s: step(qi - 1, s, "near"), lambda s: s, state)
    else:
        state = far_steps(qi, state)
    return step(qi, state, "diag")[0]


def _init_carry(t):
    return (jnp.full((1, t), NEG, F32), jnp.zeros((V_ROWS, t), F32))


def _lane_select(x, lo, hi):
    lane = lax.broadcasted_iota(jnp.int32, x.shape, 1)
    return jnp.where((lane >= lo) & (lane < hi), x, jnp.zeros_like(x))


def _merge_pair(a, b):
    row = lax.broadcasted_iota(jnp.int32, a.shape, 0)
    return jnp.where(row < HEAD_DIM, a, b)


def _causal_t(t):
    return (lax.broadcasted_iota(jnp.int32, (t, t), 0) <= lax.broadcasted_iota(jnp.int32, (t, t), 1))


def _band_bias(s_t, d_t, p_t, diag):
    n = s_t.shape[0] // BIAS_BLOCK
    rows = []
    for b in range(n):
        blocks = []
        for a in range(n):
            blk = s_t[b * BIAS_BLOCK:(b + 1) * BIAS_BLOCK, a * BIAS_BLOCK:(a + 1) * BIAS_BLOCK]
            if diag and b > a:
                blk = jnp.full_like(blk, NEG)
            elif diag and b == a:
                blk = blk + d_t
            elif (diag and b == a - 1) or (not diag and b == n - 1 and a == 0):
                blk = blk + p_t
            blocks.append(blk)
        rows.append(jnp.concatenate(blocks, axis=1))
    return jnp.concatenate(rows, axis=0)


def _attn_specs(t, seq, nq, cb, vrow):
    w = GROUP_WIDTH
    return [pl.BlockSpec((t, w), lambda b, i: (b * nq + i, cb)),
            pl.BlockSpec((seq, w), lambda b, i: (b, cb + 1)),
            pl.BlockSpec((V_SLAB, seq), lambda b, i: (vrow, b))]


def _attn_call(kernel_fn, name, batch, seq, t, in_specs, args, scratch=()):
    nq = seq // t
    return pl.pallas_call(
        kernel_fn,
        grid=(batch, nq),
        in_specs=in_specs,
        out_specs=pl.BlockSpec((t, GROUP_WIDTH), lambda b, i: (b * nq + i, 0)),
        out_shape=jax.ShapeDtypeStruct((batch * seq, GROUP_WIDTH), BF16),
        scratch_shapes=[pltpu.VMEM((SCORE_SLOTS, t, t), F32)] + list(scratch),
        compiler_params=_params("arbitrary", "arbitrary"),
        name=name,
    )(*args)


def _head_queries(q_ref, width):
    out = []
    for lo in range(0, GROUP_WIDTH, width):
        p = lo // LANES
        out.append(_lane_select(q_ref[:, p * LANES:(p + 1) * LANES], lo - p * LANES, lo - p * LANES + width))
    return out


def _fox_kernel(q_ref, k_ref, vt_ref, c_ref, o_ref, s_ref, *, t):
    qi = pl.program_id(1)
    causal = _causal_t(t)
    qms = _head_queries(q_ref, HEAD_DIM)

    def raw(h, j):
        sl = slice((h // 2) * LANES, (h // 2 + 1) * LANES)
        start = pl.multiple_of(j * t, t)
        s_t = lax.dot_general(k_ref[pl.ds(start, t), sl], qms[h], _NT, preferred_element_type=F32)
        return s_t - c_ref[pl.ds(start, t), h:h + 1]

    def fix(h, s_t, kind):
        return jnp.where(causal, s_t, NEG)

    def values(h, j):
        return vt_ref[(h // 2) * V_ROWS:(h // 2 + 1) * V_ROWS, pl.ds(pl.multiple_of(j * t, t), t)]

    carries = _sweep(GROUP_HEADS, t, qi, raw, fix, values, False, s_ref)
    outs = [_normalized(acc) for (_, acc) in carries]
    for p in range(GROUP_HEADS // 2):
        o_t = _merge_pair(outs[2 * p], outs[2 * p + 1])
        o_ref[:, p * LANES:(p + 1) * LANES] = o_t.T.astype(o_ref.dtype)


def _fox_attention(h_bf, v_t, cum_cols, batch, seq, cb, vrow, t=ATTN_TILE):
    nq = seq // t
    specs = _attn_specs(t, seq, nq, cb, vrow) + [pl.BlockSpec((seq, 8), lambda b, i: (b, 0))]
    return _attn_call(functools.partial(_fox_kernel, t=t), "fox_attn", batch, seq, t, specs,
                      (h_bf, h_bf, v_t, cum_cols))


def _diff_kernel(q_ref, k_ref, vt_ref, bd_ref, bp_ref, lam_ref, lc_ref, g_ref, o_ref, s_ref, *, t):
    qi = pl.program_id(1)
    lam_p = lam_ref[...]
    lam = (jnp.exp(jnp.sum(lam_p[0:1] * lam_p[1:2], axis=-1, keepdims=True))
           - jnp.exp(jnp.sum(lam_p[2:3] * lam_p[3:4], axis=-1, keepdims=True)) + lc_ref[0:1, 0:1])
    qms = _head_queries(q_ref, DIFF_QK_DIM)

    def raw(c, j):
        sl = slice((c // 4) * LANES, (c // 4 + 1) * LANES)
        return lax.dot_general(k_ref[pl.ds(pl.multiple_of(j * t, t), t), sl], qms[c], _NT,
                               preferred_element_type=F32)

    def fix(c, s_t, kind):
        return _band_bias(s_t, bd_ref[c // 2], bp_ref[c // 2], kind == "diag")

    def values(c, j):
        return vt_ref[(c // 4) * V_ROWS:(c // 4 + 1) * V_ROWS, pl.ds(pl.multiple_of(j * t, t), t)]

    carries = _sweep(2 * GROUP_HEADS, t, qi, raw, fix, values, True, s_ref)
    row = lax.broadcasted_iota(jnp.int32, (LANES, t), 0)
    outs = []
    for h in range(GROUP_HEADS):
        o = _normalized(carries[2 * h][1]) - lam * _normalized(carries[2 * h + 1][1])
        lo = (h % 2) * HEAD_DIM
        o = jnp.where((row >= lo) & (row < lo + HEAD_DIM), o, 0.0)
        ms = jnp.sum(o * o, axis=0, keepdims=True) * (1.0 / HEAD_DIM)
        outs.append(o * lax.rsqrt(ms + DIFF_SUBLN_EPS))
    for p in range(GROUP_HEADS // 2):
        sl = slice(p * LANES, (p + 1) * LANES)
        o_t = _merge_pair(outs[2 * p], outs[2 * p + 1])
        o_ref[:, sl] = (o_t.T * g_ref[:, sl] * lc_ref[1:2, 0:1]).astype(o_ref.dtype)


def _diff_attention(h_bf, v_t, bias_d, bias_p, lam_p, lam_c, gain, batch, seq, cb, vrow, t=ATTN_TILE):
    nq = seq // t
    bias_spec = pl.BlockSpec((GROUP_HEADS, BIAS_BLOCK, BIAS_BLOCK), lambda b, i: (0, 0, 0))
    full = lambda a: pl.BlockSpec(a.shape, lambda b, i: (0, 0))
    specs = _attn_specs(t, seq, nq, cb, vrow) + [bias_spec, bias_spec, full(lam_p), full(lam_c), full(gain)]
    return _attn_call(functools.partial(_diff_kernel, t=t), "diff_attn", batch, seq, t, specs,
                      (h_bf, h_bf, v_t, bias_d, bias_p, lam_p, lam_c, gain))


def _moba_kernel(q_ref, k_ref, vt_ref, bd_ref, bp_ref, o_ref, s_ref, kmean_ref, *, t, n_blocks):
    qi = pl.program_id(1)
    per_tile = t // MOBA_BLOCK

    @pl.when(qi == 0)
    def _():
        kmean_ref[...] = jnp.zeros_like(kmean_ref)
        for blk in range(n_blocks):
            kb = k_ref[blk * MOBA_BLOCK:(blk + 1) * MOBA_BLOCK, :].astype(F32)
            kmean_ref[blk:blk + 1, :] = jnp.sum(kb, axis=0, keepdims=True) * (1.0 / MOBA_BLOCK)

    blk_i = lax.broadcasted_iota(jnp.int32, (t, LANES), 1)
    blk_f = blk_i.astype(F32)
    row_i = lax.broadcasted_iota(jnp.int32, (t, LANES), 0)
    row_blk = sum((row_i >= n * MOBA_BLOCK).astype(jnp.int32) for n in range(1, per_tile))
    own = qi * per_tile + row_blk
    qms = _head_queries(q_ref, HEAD_DIM)
    q_cats = []
    for h in range(GROUP_HEADS):
        sl = slice((h // 2) * LANES, (h // 2 + 1) * LANES)
        gate = lax.dot_general(qms[h].astype(F32), kmean_ref[:, sl], _NT,
                               precision=lax.Precision.HIGHEST, preferred_element_type=F32)
        g = jnp.where(blk_i < own, gate, NEG)
        keep = jnp.where(blk_i == own, 1.0, 0.0)
        for _ in range(MOBA_TOPK):
            mx = jnp.max(g, axis=-1, keepdims=True)
            first = jnp.min(jnp.where(g == mx, blk_f, 1e9), axis=-1, keepdims=True)
            pick = jnp.where(mx > 0.5 * NEG, jnp.where(blk_f == first, 1.0, 0.0), 0.0)
            keep = jnp.maximum(keep, pick)
            g = jnp.where(pick > 0.0, NEG, g)
        q_cats.append(jnp.concatenate([qms[h], (1.0 - keep).astype(BF16)], axis=1))

    key_blk = blk_i - row_blk

    def raw(h, j):
        sl = slice((h // 2) * LANES, (h // 2 + 1) * LANES)
        reject = jnp.where(key_blk == j * per_tile, NEG, 0.0).astype(BF16)
        k_cat = jnp.concatenate([k_ref[pl.ds(pl.multiple_of(j * t, t), t), sl], reject], axis=1)
        return lax.dot_general(k_cat, q_cats[h], _NT, preferred_element_type=F32)

    def fix(h, s_t, kind):
        return _band_bias(s_t, bd_ref[h], bp_ref[h], kind == "diag")

    def values(h, j):
        return vt_ref[(h // 2) * V_ROWS:(h // 2 + 1) * V_ROWS, pl.ds(pl.multiple_of(j * t, t), t)]

    carries = _sweep(GROUP_HEADS, t, qi, raw, fix, values, True, s_ref)
    outs = [_normalized(acc) for (_, acc) in carries]
    for p in range(GROUP_HEADS // 2):
        o_t = _merge_pair(outs[2 * p], outs[2 * p + 1])
        o_ref[:, p * LANES:(p + 1) * LANES] = o_t.T.astype(o_ref.dtype)


def _moba_attention(h_bf, v_t, bias_d, bias_p, batch, seq, cb, vrow, t=ATTN_TILE):
    nq = seq // t
    bias_spec = pl.BlockSpec((GROUP_HEADS, BIAS_BLOCK, BIAS_BLOCK), lambda b, i: (1, 0, 0))
    specs = _attn_specs(t, seq, nq, cb, vrow) + [bias_spec, bias_spec]
    return _attn_call(functools.partial(_moba_kernel, t=t, n_blocks=seq // MOBA_BLOCK), "moba_attn",
                      batch, seq, t, specs, (h_bf, h_bf, v_t, bias_d, bias_p),
                      scratch=[pltpu.VMEM((LANES, GROUP_WIDTH), F32)])


def _in_proj_kernel(x_ref, wlat_ref, wqk_ref, wvt3_ref, ones3_ref, wgt_ref,
                    gq_ref, gkv_ref, wqa_ref, wqb_ref, wk_ref, wvt_ref, ones_ref, e_ref, tq_ref, tk_ref,
                    h_ref, vt3_ref, gate_ref, q_ref, k_ref, vt_ref):
    x = x_ref[...]
    c = jnp.dot(x, wlat_ref[...], preferred_element_type=F32)
    h_ref[...] = jnp.dot(x, wqk_ref[...], preferred_element_type=F32).astype(h_ref.dtype)
    v3 = lax.dot_general(wvt3_ref[...], x, _NT, preferred_element_type=F32) + ones3_ref[...]
    vt3_ref[...] = v3.astype(vt3_ref.dtype)
    gate_ref[...] = lax.dot_general(wgt_ref[...], x, _NT, preferred_element_type=F32)
    cq = c[:, :MLA_Q_LORA]
    ckv = c[:, MLA_Q_LORA:MLA_Q_LORA + MLA_KV_LORA]
    kr = c[:, MLA_Q_LORA + MLA_KV_LORA:]
    cq = cq * lax.rsqrt(jnp.mean(cq * cq, axis=-1, keepdims=True) + RMS_EPS) * gq_ref[...]
    ckv = ckv * lax.rsqrt(jnp.mean(ckv * ckv, axis=-1, keepdims=True) + RMS_EPS) * gkv_ref[...]
    cq = cq.astype(BF16)
    ckv = ckv.astype(BF16)
    tq = tq_ref[...]
    cos_q = jnp.concatenate([tq[:, :LANES]] * GROUP_HEADS, axis=1)
    sin_q = jnp.concatenate([tq[:, LANES:]] * GROUP_HEADS, axis=1)
    q = (jnp.dot(cq, wqa_ref[...], preferred_element_type=F32) * cos_q
         + jnp.dot(cq, wqb_ref[...], preferred_element_type=F32) * sin_q)
    q_ref[...] = (q * ((MLA_NOPE + MLA_ROPE) ** -0.5 * LOG2E)).astype(q_ref.dtype)
    k_rope = (kr * tk_ref[...]).astype(BF16)
    k = (jnp.dot(ckv, wk_ref[...], preferred_element_type=F32)
         + jnp.dot(k_rope, e_ref[...], preferred_element_type=F32))
    k_ref[...] = k.astype(k_ref.dtype)
    v_t = lax.dot_general(wvt_ref[...], ckv, _NT, preferred_element_type=F32) + ones_ref[...]
    vt_ref[...] = v_t.astype(vt_ref.dtype)


def _in_proj(xb, w_lat, w_qk, w_vt3, ones3, w_gt, gq, gkv, wqa, wqb, wk, wvt, ones, e_mat, tab_q, tab_k,
             seq, tm=512):
    m, d = xb.shape
    ns = seq // tm
    full = lambda a: pl.BlockSpec(a.shape, lambda i: (0, 0), pipeline_mode=pl.Buffered(1))
    rows = lambda w: pl.BlockSpec((tm, w), lambda i: (i, 0))
    cols = lambda r: pl.BlockSpec((r, tm), lambda i: (0, i))
    slot_w = GROUP_HEADS * LANES
    consts = (w_lat, w_qk, w_vt3, ones3, w_gt, gq, gkv, wqa, wqb, wk, wvt, ones, e_mat)
    return pl.pallas_call(
        _in_proj_kernel,
        grid=(m // tm,),
        in_specs=[rows(d)] + [full(a) for a in consts]
                 + [pl.BlockSpec((tm, tab_q.shape[1]), lambda i: (i % ns, 0)),
                    pl.BlockSpec((tm, tab_k.shape[1]), lambda i: (i % ns, 0))],
        out_specs=(rows(w_qk.shape[1]), cols(w_vt3.shape[0]), cols(w_gt.shape[0]),
                   rows(slot_w), rows(slot_w), cols(V_SLAB)),
        out_shape=(jax.ShapeDtypeStruct((m, w_qk.shape[1]), BF16),
                   jax.ShapeDtypeStruct((w_vt3.shape[0], m), BF16),
                   jax.ShapeDtypeStruct((w_gt.shape[0], m), F32),
                   jax.ShapeDtypeStruct((m, slot_w), BF16),
                   jax.ShapeDtypeStruct((m, slot_w), BF16),
                   jax.ShapeDtypeStruct((V_SLAB, m), BF16)),
        compiler_params=_params("arbitrary"),
        name="in_proj",
    )(xb, *consts, tab_q, tab_k)


def _mla_kernel(q_ref, k_ref, vt_ref, o_ref, s_ref, *, t):
    qi = pl.program_id(1)
    causal = _causal_t(t)

    def raw(h, j):
        hsl = slice(h * LANES, (h + 1) * LANES)
        return lax.dot_general(k_ref[pl.ds(pl.multiple_of(j * t, t), t), hsl], q_ref[:, hsl], _NT,
                               preferred_element_type=F32)

    def fix(h, s_t, kind):
        return jnp.where(causal, s_t, NEG)

    def values(h, j):
        return vt_ref[(h // 2) * V_ROWS:(h // 2 + 1) * V_ROWS, pl.ds(pl.multiple_of(j * t, t), t)]

    carries = _sweep(GROUP_HEADS, t, qi, raw, fix, values, False, s_ref)
    outs = [_normalized(acc) for (_, acc) in carries]
    for p in range(GROUP_HEADS // 2):
        o_t = _merge_pair(outs[2 * p], outs[2 * p + 1])
        o_ref[:, p * LANES:(p + 1) * LANES] = o_t.T.astype(o_ref.dtype)


def _mla_attention(q, k, v_t, batch, seq, t=ATTN_TILE):
    nq = seq // t
    slot_w = q.shape[1]
    specs = [pl.BlockSpec((t, slot_w), lambda b, i: (b * nq + i, 0)),
             pl.BlockSpec((seq, slot_w), lambda b, i: (b, 0)),
             pl.BlockSpec((V_SLAB, seq), lambda b, i: (0, b))]
    return _attn_call(functools.partial(_mla_kernel, t=t), "mla_attn", batch, seq, t, specs, (q, k, v_t))


def _layernorm(t, g, b):
    mu = jnp.mean(t, axis=-1, keepdims=True)
    d = t - mu
    var = jnp.mean(d * d, axis=-1, keepdims=True)
    return d * lax.rsqrt(var + LN_EPS) * g + b


def _out_kernel(o1_ref, o2_ref, o3_ref, o4_ref, w_ref, x_ref, g_ref, b_ref, of_ref, ob_ref, *, alpha):
    mix = None
    for n, o_ref in enumerate((o1_ref, o2_ref, o3_ref, o4_ref)):
        part = jnp.dot(o_ref[...], w_ref[n * GROUP_WIDTH:(n + 1) * GROUP_WIDTH, :],
                       preferred_element_type=F32)
        mix = part if mix is None else mix + part
    y = _layernorm(alpha * x_ref[...] + mix, g_ref[...], b_ref[...])
    of_ref[...] = y
    ob_ref[...] = y.astype(ob_ref.dtype)


def _out_proj(outs, w_o, x, g, b, alpha, tm=512):
    m, d = x.shape
    o_spec = pl.BlockSpec((tm, GROUP_WIDTH), lambda i: (i, 0))
    row_spec = pl.BlockSpec((tm, d), lambda i: (i, 0))
    vec_spec = pl.BlockSpec((1, d), lambda i: (0, 0))
    return pl.pallas_call(
        functools.partial(_out_kernel, alpha=alpha),
        grid=(m // tm,),
        in_specs=[o_spec, o_spec, o_spec, o_spec,
                  pl.BlockSpec(w_o.shape, lambda i: (0, 0)), row_spec, vec_spec, vec_spec],
        out_specs=(row_spec, row_spec),
        out_shape=(jax.ShapeDtypeStruct((m, d), F32), jax.ShapeDtypeStruct((m, d), BF16)),
        compiler_params=_params("arbitrary"),
        name="out_proj_ln",
    )(*outs, w_o, x, g, b)


def _ffn_kernel(xb_ref, xh_ref, xf_ref, wup_ref, cp_ref, wd_ref, g_ref, b_ref,
                of_ref, ob_ref, *, alpha, tiles_per_seq, n_chunks):
    i = pl.program_id(0)
    halo = xh_ref[...]
    halo = jnp.where(i % tiles_per_seq == 0, jnp.zeros_like(halo), halo)
    xe = jnp.concatenate([halo, xb_ref[...]], axis=0)

    def cols(c, br):
        return slice(br * D_FF_PAD + c * FF_CHUNK, br * D_FF_PAD + (c + 1) * FF_CHUNK)

    def project(c):
        return [jnp.dot(xe, wup_ref[:, cols(c, br)], preferred_element_type=F32) for br in range(2)]

    def gated(c, us):
        ys = []
        for br, u in enumerate(us):
            cp = cp_ref[:, cols(c, br)]
            y = cp[2:3] * u + cp[1:2] * pltpu.roll(u, 1, 0) + cp[0:1] * pltpu.roll(u, 2, 0) + cp[3:4]
            ys.append(y[CONV_HALO:])
        return (ys[0] * jax.nn.sigmoid(ys[0]) * ys[1]).astype(BF16)

    acc = None
    u_next = project(0)
    for c in range(n_chunks):
        us = u_next
        if c + 1 < n_chunks:
            u_next = project(c + 1)
        part = jnp.dot(gated(c, us), wd_ref[c * FF_CHUNK:(c + 1) * FF_CHUNK, :], preferred_element_type=F32)
        acc = part if acc is None else acc + part
    y = _layernorm(alpha * xf_ref[...] + acc, g_ref[...], b_ref[...])
    of_ref[...] = y
    ob_ref[...] = y.astype(ob_ref.dtype)


def _ffn(xb, xf, w_up, conv_p, w_down, g, b, alpha, seq, tm=256):
    m, d = xf.shape
    n_chunks = D_FF_PAD // FF_CHUNK
    halo_blocks = tm // CONV_HALO
    row_spec = pl.BlockSpec((tm, d), lambda i: (i, 0))
    vec_spec = pl.BlockSpec((1, d), lambda i: (0, 0))
    resident = lambda a: pl.BlockSpec(a.shape, lambda i: (0, 0), pipeline_mode=pl.Buffered(1))
    return pl.pallas_call(
        functools.partial(_ffn_kernel, alpha=alpha, tiles_per_seq=seq // tm, n_chunks=n_chunks),
        grid=(m // tm,),
        in_specs=[row_spec,
                  pl.BlockSpec((CONV_HALO, d), lambda i: (jnp.maximum(i * halo_blocks - 1, 0), 0)),
                  row_spec, resident(w_up), resident(conv_p), resident(w_down), vec_spec, vec_spec],
        out_specs=(row_spec, row_spec),
        out_shape=(jax.ShapeDtypeStruct((m, d), F32), jax.ShapeDtypeStruct((m, d), BF16)),
        compiler_params=_params("arbitrary"),
        name="conv_ffn_ln",
    )(xb, xb, xf, w_up, conv_p, w_down, g, b)


def _swap_halves(w):
    half = w.shape[-1] // 2
    return jnp.concatenate([-w[..., half:], w[..., :half]], axis=-1)


def _value_slab(w_v):
    depth, k, _ = w_v.shape
    w = w_v.reshape(depth, k, GROUP_HEADS // 2, LANES)
    w = jnp.pad(w, ((0, 0), (0, 0), (0, 0), (0, V_ROWS - LANES)))
    return jnp.swapaxes(w.reshape(depth, k, V_SLAB), 1, 2)


def _ones_rows(n_slabs):
    row = jnp.arange(n_slabs * V_SLAB) % V_ROWS
    return (row == LANES).astype(F32)[:, None]


def _prep_weights(w_in, mla_w_uq, mla_w_ukv, w_up, conv_w, conv_b, w_down):
    depth, d, _ = w_in.shape
    gw = GROUP_WIDTH
    fox0 = 0
    diff0 = 3 * gw + GROUP_HEADS
    moba0 = diff0 + 3 * gw
    mla0 = moba0 + 3 * gw
    sm = HEAD_DIM ** -0.5 * LOG2E
    sd = DIFF_QK_DIM ** -0.5 * LOG2E
    w_qk = jnp.concatenate([
        w_in[:, :, fox0:fox0 + gw] * sm, w_in[:, :, fox0 + gw:fox0 + 2 * gw],
        w_in[:, :, diff0:diff0 + gw] * sd, w_in[:, :, diff0 + gw:diff0 + 2 * gw],
        w_in[:, :, moba0:moba0 + gw] * sm, w_in[:, :, moba0 + gw:moba0 + 2 * gw]], axis=-1).astype(BF16)
    w_vt = jnp.concatenate([_value_slab(w_in[:, :, g0 + 2 * gw:g0 + 3 * gw])
                            for g0 in (fox0, diff0, moba0)], axis=1).astype(BF16)
    w_gt = jnp.swapaxes(jnp.pad(w_in[:, :, 3 * gw:3 * gw + GROUP_HEADS],
                                ((0, 0), (0, 0), (0, 8 - GROUP_HEADS))), 1, 2).astype(BF16)
    kr0 = mla0 + MLA_Q_LORA + MLA_KV_LORA
    w_kr = w_in[:, :, kr0:kr0 + MLA_ROPE]
    w_lat = jnp.concatenate([w_in[:, :, mla0:kr0], w_kr, _swap_halves(w_kr),
                             jnp.zeros((depth, d, LANES - 2 * MLA_ROPE), F32)], axis=-1).astype(BF16)

    uq = mla_w_uq.reshape(depth, MLA_Q_LORA, GROUP_HEADS, MLA_NOPE + MLA_ROPE)
    pad = LANES - MLA_NOPE - MLA_ROPE
    wqa = jnp.pad(uq, ((0, 0), (0, 0), (0, 0), (0, pad)))
    wqb = jnp.concatenate([jnp.zeros_like(uq[..., :MLA_NOPE]), _swap_halves(uq[..., MLA_NOPE:]),
                           jnp.zeros(uq.shape[:-1] + (pad,), F32)], axis=-1)
    ukv = mla_w_ukv.reshape(depth, MLA_KV_LORA, GROUP_HEADS, MLA_NOPE + HEAD_DIM)
    wk = jnp.pad(ukv[..., :MLA_NOPE], ((0, 0), (0, 0), (0, 0), (0, LANES - MLA_NOPE)))
    flat = lambda a: a.reshape(depth, a.shape[1], -1).astype(BF16)
    wvt = _value_slab(flat(ukv[..., MLA_NOPE:])).astype(BF16)

    ffp = D_FF_PAD - D_FF
    padc = lambda a: jnp.pad(a, ((0, 0), (0, 0), (0, ffp)))
    w_up_p = jnp.concatenate([padc(w_up[:, :, :D_FF]), padc(w_up[:, :, D_FF:])], axis=-1).astype(BF16)
    conv = jnp.concatenate([conv_w, conv_b[:, None, :],
                            jnp.zeros((depth, 8 - CONV_WIDTH - 1, 2 * D_FF), F32)], axis=1)
    conv_p = jnp.concatenate([padc(conv[:, :, :D_FF]), padc(conv[:, :, D_FF:])], axis=-1)
    w_down_p = jnp.pad(w_down, ((0, 0), (0, ffp), (0, 0))).astype(BF16)
    return w_qk, w_vt, w_gt, w_lat, flat(wqa), flat(wqb), flat(wk), wvt, w_up_p, conv_p, w_down_p


def _rope_constants(seq):
    inv = ROPE_THETA ** (-jnp.arange(0, MLA_ROPE // 2, dtype=F32) * 2.0 / MLA_ROPE)
    ang = jnp.arange(seq, dtype=F32)[:, None] * inv[None, :]
    cos, sin = jnp.cos(ang), jnp.sin(ang)
    cos2 = jnp.concatenate([cos, cos], axis=1)
    sin2 = jnp.concatenate([sin, sin], axis=1)
    pad = jnp.zeros((seq, LANES - MLA_NOPE - MLA_ROPE), F32)
    tab_q = jnp.concatenate([jnp.ones((seq, MLA_NOPE), F32), cos2, pad,
                             jnp.zeros((seq, MLA_NOPE), F32), sin2, pad], axis=1)
    tab_k = jnp.concatenate([cos2, sin2, jnp.zeros((seq, LANES - 2 * MLA_ROPE), F32)], axis=1)
    r = jnp.arange(LANES)[:, None]
    c = jnp.arange(GROUP_HEADS * LANES)[None, :]
    e_mat = ((r < 2 * MLA_ROPE) & (c % LANES == MLA_NOPE + r % MLA_ROPE)).astype(BF16)
    return tab_q, tab_k, e_mat


def kernel(x, w_in, b_forget, diff_lambda, diff_subln, mla_q_norm, mla_kv_norm, mla_w_uq, mla_w_ukv,
           rel_bias, w_o, ln1_g, ln1_b, w_up, conv_w, conv_b, w_down, ln2_g, ln2_b):
    batch, seq, d = x.shape
    depth = w_in.shape[0]
    alpha = (2 * depth) ** 0.25
    assert seq % ATTN_TILE == 0 and ATTN_TILE % MOBA_BLOCK == 0 and seq // MOBA_BLOCK <= LANES

    (w_qk, w_vt, w_gt, w_lat, wqa, wqb, wk, wvt, w_up_p, conv_p, w_down_p) = _prep_weights(
        w_in, mla_w_uq, mla_w_ukv, w_up, conv_w, conv_b, w_down)
    w_o_b = w_o.astype(BF16)
    tab_q, tab_k, e_mat = _rope_constants(seq)
    bias_d, bias_p = _bias_blocks(rel_bias, BIAS_BLOCK)
    b_col = jnp.pad(b_forget, ((0, 0), (0, 8 - GROUP_HEADS)))[:, :, None]
    gain = jnp.tile(diff_subln, (1, GROUP_HEADS))[:, None, :]

    xf = x.reshape(batch * seq, d)
    xb = xf.astype(BF16)
    for l in range(depth):
        h_bf, v_t, gate_t, q_m, k_m, v_m = _in_proj(
            xb, w_lat[l], w_qk[l], w_vt[l], _ones_rows(3), w_gt[l], mla_q_norm[l][None], mla_kv_norm[l][None],
            wqa[l], wqb[l], wk[l], wvt[l], _ones_rows(1), e_mat, tab_q, tab_k, seq)
        cum = _fox_cumlog(gate_t, b_col[l], batch, seq)
        fox_o = _fox_attention(h_bf, v_t, cum.T, batch, seq, 0, 0)
        lam_init = 0.8 - 0.6 * math.exp(-0.3 * l)
        lam_c = jnp.concatenate([jnp.full((1, LANES), lam_init, F32),
                                 jnp.full((1, LANES), 1.0 - lam_init, F32),
                                 jnp.zeros((6, LANES), F32)], axis=0)
        diff_o = _diff_attention(h_bf, v_t, bias_d, bias_p, diff_lambda[l], lam_c, gain[l], batch, seq, 2, 1)
        moba_o = _moba_attention(h_bf, v_t, bias_d, bias_p, batch, seq, 4, 2)
        mla_o = _mla_attention(q_m, k_m, v_m, batch, seq)
        xf, xb = _out_proj((fox_o, diff_o, moba_o, mla_o), w_o_b[l], xf, ln1_g[l][None], ln1_b[l][None], alpha)
        xf, xb = _ffn(xb, xf, w_up_p[l], conv_p[l], w_down_p[l], ln2_g[l][None], ln2_b[l][None], alpha, seq)
    return xf.reshape(batch, seq, d)
```

```python
import functools
import math

import jax
import jax.numpy as jnp
from jax import lax
from jax.experimental import pallas as pl
from jax.experimental.pallas import tpu as pltpu

F32 = jnp.float32
BF16 = jnp.bfloat16

HEAD_DIM = 64
GROUP_HEADS = 4
GROUP_WIDTH = GROUP_HEADS * HEAD_DIM
LANES = 128
DIFF_QK_DIM = HEAD_DIM // 2
DIFF_SUBLN_EPS = 1e-5
MOBA_BLOCK = 256
MOBA_TOPK = 3
MLA_Q_LORA = 256
MLA_KV_LORA = 128
MLA_NOPE = 64
MLA_ROPE = 32
ROPE_THETA = 10000.0
REL_BUCKETS = 32
REL_MAX_DIST = 128
D_FF = 2752
FF_CHUNK = 256
D_FF_PAD = -(-D_FF // FF_CHUNK) * FF_CHUNK
FFN_LOOKAHEAD = 2
OUT_SUBTILE = 128
CONV_WIDTH = 3
CONV_HALO = 16
LN_EPS = 1e-5
RMS_EPS = 1e-6
NEG = -1e30
LOG2E = 1.4426950408889634

BIAS_BLOCK = 128
ATTN_TILE = 512
V_ROWS = LANES + 16
V_SLAB = (GROUP_HEADS // 2) * V_ROWS
SCORE_LOOKAHEAD = 3
SCORE_SLOTS = 4
VMEM_LIMIT = 56 * 1024 * 1024

_NT = (((1,), (1,)), ((), ()))


def _params(*sem):
    return pltpu.CompilerParams(dimension_semantics=sem, vmem_limit_bytes=VMEM_LIMIT)


def _cum_kernel(g_ref, b_ref, o_ref):
    x = g_ref[...] + b_ref[...]
    x = jnp.minimum(x, 0.0) - jnp.log(1.0 + jnp.exp(-jnp.abs(x)))
    n = x.shape[1]
    lane = lax.broadcasted_iota(jnp.int32, x.shape, 1)
    k = 1
    while k < n:
        x = x + jnp.where(lane >= k, pltpu.roll(x, k, 1), 0.0)
        k *= 2
    o_ref[...] = x * LOG2E


def _fox_cumlog(logits_t, b_col, batch, seq):
    return pl.pallas_call(
        _cum_kernel,
        grid=(batch,),
        in_specs=[pl.BlockSpec((8, seq), lambda b: (0, b)),
                  pl.BlockSpec((8, 1), lambda b: (0, 0))],
        out_specs=pl.BlockSpec((8, seq), lambda b: (0, b)),
        out_shape=jax.ShapeDtypeStruct(logits_t.shape, F32),
        compiler_params=_params("arbitrary"),
        name="fox_cumlog",
    )(logits_t, b_col)


def _bias_kernel(rb_ref, d_ref, p_ref, *, t):
    h = pl.program_id(0)
    i = lax.broadcasted_iota(jnp.int32, (t, t), 0)
    j = lax.broadcasted_iota(jnp.int32, (t, t), 1)
    max_exact = REL_BUCKETS // 2
    far = rb_ref[(REL_BUCKETS - 1) * 8 + h]
    for out_ref, off in ((d_ref, 0), (p_ref, t)):
        dist = j - i + off
        d_large = jnp.maximum(dist, max_exact).astype(F32)
        large = max_exact + (jnp.log(d_large / max_exact) / math.log(REL_MAX_DIST / max_exact)
                             * (REL_BUCKETS - max_exact)).astype(jnp.int32)
        large = jnp.minimum(large, REL_BUCKETS - 1)
        bucket = jnp.where(dist < max_exact, dist, large)
        acc = jnp.zeros((t, t), F32)
        for bkt in range(REL_BUCKETS - 1):
            acc = jnp.where(bucket == bkt, (rb_ref[bkt * 8 + h] - far) * LOG2E, acc)
        if off == 0:
            acc = jnp.where(dist >= 0, acc, NEG)
        out_ref[0] = acc


def _bias_blocks(rel_bias, t):
    n_heads = rel_bias.shape[1]
    shape = jax.ShapeDtypeStruct((n_heads, t, t), F32)
    spec = pl.BlockSpec((1, t, t), lambda h: (h, 0, 0))
    return pl.pallas_call(
        functools.partial(_bias_kernel, t=t),
        grid=(n_heads,),
        in_specs=[pl.BlockSpec(memory_space=pltpu.SMEM)],
        out_specs=(spec, spec),
        out_shape=(shape, shape),
        compiler_params=_params("arbitrary"),
        name="rel_bias_blocks",
    )(rel_bias.reshape(-1))


def _online_update(s_t, tile_max, v_t, carry):
    m, acc = carry
    m_new = jnp.maximum(m, tile_max)
    p = jnp.exp2((s_t - m_new).astype(BF16))
    acc = jnp.exp2(m - m_new) * acc + jnp.dot(v_t, p, preferred_element_type=F32)
    return m_new, acc


def _normalized(acc):
    return acc[:LANES] / acc[LANES:LANES + 1]


def _sweep(n, t, qi, raw, fix, values, near_bias, s_ref):
    la = SCORE_LOOKAHEAD
    slots = s_ref.shape[0]
    assert n % slots == 0 and la < slots

    def produce(c, j):
        s_t = raw(c, j)
        s_ref[c % slots] = s_t
        return jnp.max(s_t, axis=0, keepdims=True)

    def step(j, state, kind):
        carries, tops = state
        tops = dict(enumerate(tops))
        nxt = []
        out = []
        for c in range(n):
            if c + la < n:
                tops[c + la] = produce(c + la, j)
            elif kind != "diag":
                nxt.append(produce(c + la - n, j + 1))
            s_t, top = s_ref[c % slots], tops.pop(c)
            if kind != "far":
                s_t = fix(c, s_t, kind)
                top = jnp.max(s_t, axis=0, keepdims=True)
            out.append(_online_update(s_t, top, values(c, j), carries[c]))
        return tuple(out), (tuple(nxt) if nxt else state[1])

    def far_steps(count, state):
        def pair(i, s):
            return step(2 * i + 1, step(2 * i, s, "far"), "far")

        state = lax.fori_loop(0, lax.shift_right_logical(count, 1), pair, state)
        return lax.cond(jnp.bitwise_and(count, 1) == 1,
                        lambda s: step(count - 1, s, "far"), lambda s: s, state)

    state = (tuple(_init_carry(t) for _ in range(n)), tuple(produce(c, 0) for c in range(la)))
    if near_bias:
        state = far_steps(jnp.maximum(qi - 1, 0), state)
        state = lax.cond(qi >= 1, lambda s: step(qi - 1, s, "near"), lambda s: s, state)
    else:
        state = far_steps(qi, state)
    return step(qi, state, "diag")[0]


def _init_carry(t):
    return (jnp.full((1, t), NEG, F32), jnp.zeros((V_ROWS, t), F32))


def _lane_select(x, lo, hi):
    lane = lax.broadcasted_iota(jnp.int32, x.shape, 1)
    return jnp.where((lane >= lo) & (lane < hi), x, jnp.zeros_like(x))


def _merge_pair(a, b):
    row = lax.broadcasted_iota(jnp.int32, a.shape, 0)
    return jnp.where(row < HEAD_DIM, a, b)


def _causal_t(t):
    return (lax.broadcasted_iota(jnp.int32, (t, t), 0) <= lax.broadcasted_iota(jnp.int32, (t, t), 1))


def _band_bias(s_t, d_t, p_t, diag):
    n = s_t.shape[0] // BIAS_BLOCK
    rows = []
    for b in range(n):
        blocks = []
        for a in range(n):
            blk = s_t[b * BIAS_BLOCK:(b + 1) * BIAS_BLOCK, a * BIAS_BLOCK:(a + 1) * BIAS_BLOCK]
            if diag and b > a:
                blk = jnp.full_like(blk, NEG)
            elif diag and b == a:
                blk = blk + d_t
            elif (diag and b == a - 1) or (not diag and b == n - 1 and a == 0):
                blk = blk + p_t
            blocks.append(blk)
        rows.append(jnp.concatenate(blocks, axis=1))
    return jnp.concatenate(rows, axis=0)


def _attn_specs(t, seq, nq, cb, vrow):
    w = GROUP_WIDTH
    return [pl.BlockSpec((t, w), lambda b, i: (b * nq + i, cb)),
            pl.BlockSpec((seq, w), lambda b, i: (b, cb + 1)),
            pl.BlockSpec((V_SLAB, seq), lambda b, i: (vrow, b))]


def _attn_call(kernel_fn, name, batch, seq, t, in_specs, args, scratch=()):
    nq = seq // t
    return pl.pallas_call(
        kernel_fn,
        grid=(batch, nq),
        in_specs=in_specs,
        out_specs=pl.BlockSpec((t, GROUP_WIDTH), lambda b, i: (b * nq + i, 0)),
        out_shape=jax.ShapeDtypeStruct((batch * seq, GROUP_WIDTH), BF16),
        scratch_shapes=[pltpu.VMEM((SCORE_SLOTS, t, t), F32)] + list(scratch),
        compiler_params=_params("arbitrary", "arbitrary"),
        name=name,
    )(*args)


def _head_queries(q_ref, width):
    out = []
    for lo in range(0, GROUP_WIDTH, width):
        p = lo // LANES
        out.append(_lane_select(q_ref[:, p * LANES:(p + 1) * LANES], lo - p * LANES, lo - p * LANES + width))
    return out


def _fox_kernel(q_ref, k_ref, vt_ref, c_ref, o_ref, s_ref, *, t):
    qi = pl.program_id(1)
    causal = _causal_t(t)
    qms = _head_queries(q_ref, HEAD_DIM)

    def raw(h, j):
        sl = slice((h // 2) * LANES, (h // 2 + 1) * LANES)
        start = pl.multiple_of(j * t, t)
        s_t = lax.dot_general(k_ref[pl.ds(start, t), sl], qms[h], _NT, preferred_element_type=F32)
        return s_t - c_ref[pl.ds(start, t), h:h + 1]

    def fix(h, s_t, kind):
        return jnp.where(causal, s_t, NEG)

    def values(h, j):
        return vt_ref[(h // 2) * V_ROWS:(h // 2 + 1) * V_ROWS, pl.ds(pl.multiple_of(j * t, t), t)]

    carries = _sweep(GROUP_HEADS, t, qi, raw, fix, values, False, s_ref)
    outs = [_normalized(acc) for (_, acc) in carries]
    for p in range(GROUP_HEADS // 2):
        o_t = _merge_pair(outs[2 * p], outs[2 * p + 1])
        o_ref[:, p * LANES:(p + 1) * LANES] = o_t.T.astype(o_ref.dtype)


def _fox_attention(h_bf, v_t, cum_cols, batch, seq, cb, vrow, t=ATTN_TILE):
    nq = seq // t
    specs = _attn_specs(t, seq, nq, cb, vrow) + [pl.BlockSpec((seq, 8), lambda b, i: (b, 0))]
    return _attn_call(functools.partial(_fox_kernel, t=t), "fox_attn", batch, seq, t, specs,
                      (h_bf, h_bf, v_t, cum_cols))


def _diff_kernel(q_ref, k_ref, vt_ref, bd_ref, bp_ref, lam_ref, lc_ref, g_ref, o_ref, s_ref, *, t):
    qi = pl.program_id(1)
    lam_p = lam_ref[...]
    lam = (jnp.exp(jnp.sum(lam_p[0:1] * lam_p[1:2], axis=-1, keepdims=True))
           - jnp.exp(jnp.sum(lam_p[2:3] * lam_p[3:4], axis=-1, keepdims=True)) + lc_ref[0:1, 0:1])
    qms = _head_queries(q_ref, DIFF_QK_DIM)

    def raw(c, j):
        sl = slice((c // 4) * LANES, (c // 4 + 1) * LANES)
        return lax.dot_general(k_ref[pl.ds(pl.multiple_of(j * t, t), t), sl], qms[c], _NT,
                               preferred_element_type=F32)

    def fix(c, s_t, kind):
        return _band_bias(s_t, bd_ref[c // 2], bp_ref[c // 2], kind == "diag")

    def values(c, j):
        return vt_ref[(c // 4) * V_ROWS:(c // 4 + 1) * V_ROWS, pl.ds(pl.multiple_of(j * t, t), t)]

    carries = _sweep(2 * GROUP_HEADS, t, qi, raw, fix, values, True, s_ref)
    row = lax.broadcasted_iota(jnp.int32, (LANES, t), 0)
    outs = []
    for h in range(GROUP_HEADS):
        o = _normalized(carries[2 * h][1]) - lam * _normalized(carries[2 * h + 1][1])
        lo = (h % 2) * HEAD_DIM
        o = jnp.where((row >= lo) & (row < lo + HEAD_DIM), o, 0.0)
        ms = jnp.sum(o * o, axis=0, keepdims=True) * (1.0 / HEAD_DIM)
        outs.append(o * lax.rsqrt(ms + DIFF_SUBLN_EPS))
    for p in range(GROUP_HEADS // 2):
        sl = slice(p * LANES, (p + 1) * LANES)
        o_t = _merge_pair(outs[2 * p], outs[2 * p + 1])
        o_ref[:, sl] = (o_t.T * g_ref[:, sl] * lc_ref[1:2, 0:1]).astype(o_ref.dtype)


def _diff_attention(h_bf, v_t, bias_d, bias_p, lam_p, lam_c, gain, batch, seq, cb, vrow, t=ATTN_TILE):
    nq = seq // t
    bias_spec = pl.BlockSpec((GROUP_HEADS, BIAS_BLOCK, BIAS_BLOCK), lambda b, i: (0, 0, 0))
    full = lambda a: pl.BlockSpec(a.shape, lambda b, i: (0, 0))
    specs = _attn_specs(t, seq, nq, cb, vrow) + [bias_spec, bias_spec, full(lam_p), full(lam_c), full(gain)]
    return _attn_call(functools.partial(_diff_kernel, t=t), "diff_attn", batch, seq, t, specs,
                      (h_bf, h_bf, v_t, bias_d, bias_p, lam_p, lam_c, gain))


def _moba_kernel(q_ref, k_ref, vt_ref, bd_ref, bp_ref, o_ref, s_ref, kmean_ref, *, t, n_blocks):
    qi = pl.program_id(1)
    per_tile = t // MOBA_BLOCK

    @pl.when(qi == 0)
    def _():
        kmean_ref[...] = jnp.zeros_like(kmean_ref)
        for blk in range(n_blocks):
            kb = k_ref[blk * MOBA_BLOCK:(blk + 1) * MOBA_BLOCK, :].astype(F32)
            kmean_ref[blk:blk + 1, :] = jnp.sum(kb, axis=0, keepdims=True) * (1.0 / MOBA_BLOCK)

    blk_i = lax.broadcasted_iota(jnp.int32, (t, LANES), 1)
    blk_f = blk_i.astype(F32)
    row_i = lax.broadcasted_iota(jnp.int32, (t, LANES), 0)
    row_blk = sum((row_i >= n * MOBA_BLOCK).astype(jnp.int32) for n in range(1, per_tile))
    own = qi * per_tile + row_blk
    qms = _head_queries(q_ref, HEAD_DIM)
    q_cats = []
    for h in range(GROUP_HEADS):
        sl = slice((h // 2) * LANES, (h // 2 + 1) * LANES)
        gate = lax.dot_general(qms[h].astype(F32), kmean_ref[:, sl], _NT,
                               precision=lax.Precision.HIGHEST, preferred_element_type=F32)
        g = jnp.where(blk_i < own, gate, NEG)
        keep = jnp.where(blk_i == own, 1.0, 0.0)
        for _ in range(MOBA_TOPK):
            mx = jnp.max(g, axis=-1, keepdims=True)
            first = jnp.min(jnp.where(g == mx, blk_f, 1e9), axis=-1, keepdims=True)
            pick = jnp.where(mx > 0.5 * NEG, jnp.where(blk_f == first, 1.0, 0.0), 0.0)
            keep = jnp.maximum(keep, pick)
            g = jnp.where(pick > 0.0, NEG, g)
        q_cats.append(jnp.concatenate([qms[h], (1.0 - keep).astype(BF16)], axis=1))

    key_blk = blk_i - row_blk

    def raw(h, j):
        sl = slice((h // 2) * LANES, (h // 2 + 1) * LANES)
        reject = jnp.where(key_blk == j * per_tile, NEG, 0.0).astype(BF16)
        k_cat = jnp.concatenate([k_ref[pl.ds(pl.multiple_of(j * t, t), t), sl], reject], axis=1)
        return lax.dot_general(k_cat, q_cats[h], _NT, preferred_element_type=F32)

    def fix(h, s_t, kind):
        return _band_bias(s_t, bd_ref[h], bp_ref[h], kind == "diag")

    def values(h, j):
        return vt_ref[(h // 2) * V_ROWS:(h // 2 + 1) * V_ROWS, pl.ds(pl.multiple_of(j * t, t), t)]

    carries = _sweep(GROUP_HEADS, t, qi, raw, fix, values, True, s_ref)
    outs = [_normalized(acc) for (_, acc) in carries]
    for p in range(GROUP_HEADS // 2):
        o_t = _merge_pair(outs[2 * p], outs[2 * p + 1])
        o_ref[:, p * LANES:(p + 1) * LANES] = o_t.T.astype(o_ref.dtype)


def _moba_attention(h_bf, v_t, bias_d, bias_p, batch, seq, cb, vrow, t=ATTN_TILE):
    nq = seq // t
    bias_spec = pl.BlockSpec((GROUP_HEADS, BIAS_BLOCK, BIAS_BLOCK), lambda b, i: (1, 0, 0))
    specs = _attn_specs(t, seq, nq, cb, vrow) + [bias_spec, bias_spec]
    return _attn_call(functools.partial(_moba_kernel, t=t, n_blocks=seq // MOBA_BLOCK), "moba_attn",
                      batch, seq, t, specs, (h_bf, h_bf, v_t, bias_d, bias_p),
                      scratch=[pltpu.VMEM((LANES, GROUP_WIDTH), F32)])


def _in_proj_kernel(x_ref, wlat_ref, wqk_ref, wvt3_ref, ones3_ref, wgt_ref,
                    gq_ref, gkv_ref, wqa_ref, wqb_ref, wk_ref, wvt_ref, ones_ref, e_ref, tq_ref, tk_ref,
                    h_ref, vt3_ref, gate_ref, q_ref, k_ref, vt_ref):
    x = x_ref[...]
    c = jnp.dot(x, wlat_ref[...], preferred_element_type=F32)
    h_ref[...] = jnp.dot(x, wqk_ref[...], preferred_element_type=F32).astype(h_ref.dtype)
    v3 = lax.dot_general(wvt3_ref[...], x, _NT, preferred_element_type=F32) + ones3_ref[...]
    vt3_ref[...] = v3.astype(vt3_ref.dtype)
    gate_ref[...] = lax.dot_general(wgt_ref[...], x, _NT, preferred_element_type=F32)
    cq = c[:, :MLA_Q_LORA]
    ckv = c[:, MLA_Q_LORA:MLA_Q_LORA + MLA_KV_LORA]
    kr = c[:, MLA_Q_LORA + MLA_KV_LORA:]
    cq = cq * lax.rsqrt(jnp.mean(cq * cq, axis=-1, keepdims=True) + RMS_EPS) * gq_ref[...]
    ckv = ckv * lax.rsqrt(jnp.mean(ckv * ckv, axis=-1, keepdims=True) + RMS_EPS) * gkv_ref[...]
    cq = cq.astype(BF16)
    ckv = ckv.astype(BF16)
    tq = tq_ref[...]
    cos_q = jnp.concatenate([tq[:, :LANES]] * GROUP_HEADS, axis=1)
    sin_q = jnp.concatenate([tq[:, LANES:]] * GROUP_HEADS, axis=1)
    q = (jnp.dot(cq, wqa_ref[...], preferred_element_type=F32) * cos_q
         + jnp.dot(cq, wqb_ref[...], preferred_element_type=F32) * sin_q)
    q_ref[...] = (q * ((MLA_NOPE + MLA_ROPE) ** -0.5 * LOG2E)).astype(q_ref.dtype)
    k_rope = (kr * tk_ref[...]).astype(BF16)
    k = (jnp.dot(ckv, wk_ref[...], preferred_element_type=F32)
         + jnp.dot(k_rope, e_ref[...], preferred_element_type=F32))
    k_ref[...] = k.astype(k_ref.dtype)
    v_t = lax.dot_general(wvt_ref[...], ckv, _NT, preferred_element_type=F32) + ones_ref[...]
    vt_ref[...] = v_t.astype(vt_ref.dtype)


def _in_proj(xb, w_lat, w_qk, w_vt3, ones3, w_gt, gq, gkv, wqa, wqb, wk, wvt, ones, e_mat, tab_q, tab_k,
             seq, tm=512):
    m, d = xb.shape
    ns = seq // tm
    full = lambda a: pl.BlockSpec(a.shape, lambda i: (0, 0), pipeline_mode=pl.Buffered(1))
    rows = lambda w: pl.BlockSpec((tm, w), lambda i: (i, 0))
    cols = lambda r: pl.BlockSpec((r, tm), lambda i: (0, i))
    slot_w = GROUP_HEADS * LANES
    consts = (w_lat, w_qk, w_vt3, ones3, w_gt, gq, gkv, wqa, wqb, wk, wvt, ones, e_mat)
    return pl.pallas_call(
        _in_proj_kernel,
        grid=(m // tm,),
        in_specs=[rows(d)] + [full(a) for a in consts]
                 + [pl.BlockSpec((tm, tab_q.shape[1]), lambda i: (i % ns, 0)),
                    pl.BlockSpec((tm, tab_k.shape[1]), lambda i: (i % ns, 0))],
        out_specs=(rows(w_qk.shape[1]), cols(w_vt3.shape[0]), cols(w_gt.shape[0]),
                   rows(slot_w), rows(slot_w), cols(V_SLAB)),
        out_shape=(jax.ShapeDtypeStruct((m, w_qk.shape[1]), BF16),
                   jax.ShapeDtypeStruct((w_vt3.shape[0], m), BF16),
                   jax.ShapeDtypeStruct((w_gt.shape[0], m), F32),
                   jax.ShapeDtypeStruct((m, slot_w), BF16),
                   jax.ShapeDtypeStruct((m, slot_w), BF16),
                   jax.ShapeDtypeStruct((V_SLAB, m), BF16)),
        compiler_params=_params("arbitrary"),
        name="in_proj",
    )(xb, *consts, tab_q, tab_k)


def _mla_kernel(q_ref, k_ref, vt_ref, o_ref, s_ref, *, t):
    qi = pl.program_id(1)
    causal = _causal_t(t)

    def raw(h, j):
        hsl = slice(h * LANES, (h + 1) * LANES)
        return lax.dot_general(k_ref[pl.ds(pl.multiple_of(j * t, t), t), hsl], q_ref[:, hsl], _NT,
                               preferred_element_type=F32)

    def fix(h, s_t, kind):
        return jnp.where(causal, s_t, NEG)

    def values(h, j):
        return vt_ref[(h // 2) * V_ROWS:(h // 2 + 1) * V_ROWS, pl.ds(pl.multiple_of(j * t, t), t)]

    carries = _sweep(GROUP_HEADS, t, qi, raw, fix, values, False, s_ref)
    outs = [_normalized(acc) for (_, acc) in carries]
    for p in range(GROUP_HEADS // 2):
        o_t = _merge_pair(outs[2 * p], outs[2 * p + 1])
        o_ref[:, p * LANES:(p + 1) * LANES] = o_t.T.astype(o_ref.dtype)


def _mla_attention(q, k, v_t, batch, seq, t=ATTN_TILE):
    nq = seq // t
    slot_w = q.shape[1]
    specs = [pl.BlockSpec((t, slot_w), lambda b, i: (b * nq + i, 0)),
             pl.BlockSpec((seq, slot_w), lambda b, i: (b, 0)),
             pl.BlockSpec((V_SLAB, seq), lambda b, i: (0, b))]
    return _attn_call(functools.partial(_mla_kernel, t=t), "mla_attn", batch, seq, t, specs, (q, k, v_t))


def _layernorm(t, g, b):
    mu = jnp.mean(t, axis=-1, keepdims=True)
    d = t - mu
    var = jnp.mean(d * d, axis=-1, keepdims=True)
    return d * lax.rsqrt(var + LN_EPS) * g + b


def _out_kernel(o1_ref, o2_ref, o3_ref, o4_ref, w_ref, x_ref, g_ref, b_ref, of_ref, ob_ref, *, alpha):
    sub = OUT_SUBTILE
    n_sub = x_ref.shape[0] // sub

    def project(r):
        rows = slice(r * sub, (r + 1) * sub)
        o = jnp.concatenate([o_ref[rows, :] for o_ref in (o1_ref, o2_ref, o3_ref, o4_ref)], axis=1)
        return jnp.dot(o, w_ref[...], preferred_element_type=F32)

    mix_next = project(0)
    for r in range(n_sub):
        mix = mix_next
        if r + 1 < n_sub:
            mix_next = project(r + 1)
        rows = slice(r * sub, (r + 1) * sub)
        y = _layernorm(alpha * x_ref[rows, :] + mix, g_ref[...], b_ref[...])
        of_ref[rows, :] = y
        ob_ref[rows, :] = y.astype(ob_ref.dtype)


def _out_proj(outs, w_o, x, g, b, alpha, tm=512):
    m, d = x.shape
    o_spec = pl.BlockSpec((tm, GROUP_WIDTH), lambda i: (i, 0))
    row_spec = pl.BlockSpec((tm, d), lambda i: (i, 0))
    vec_spec = pl.BlockSpec((1, d), lambda i: (0, 0))
    return pl.pallas_call(
        functools.partial(_out_kernel, alpha=alpha),
        grid=(m // tm,),
        in_specs=[o_spec, o_spec, o_spec, o_spec,
                  pl.BlockSpec(w_o.shape, lambda i: (0, 0)), row_spec, vec_spec, vec_spec],
        out_specs=(row_spec, row_spec),
        out_shape=(jax.ShapeDtypeStruct((m, d), F32), jax.ShapeDtypeStruct((m, d), BF16)),
        compiler_params=_params("arbitrary"),
        name="out_proj_ln",
    )(*outs, w_o, x, g, b)


def _ffn_kernel(xb_ref, xh_ref, xf_ref, wup_ref, cp_ref, wd_ref, g_ref, b_ref,
                of_ref, ob_ref, *, alpha, tiles_per_seq, n_chunks):
    i = pl.program_id(0)
    halo = xh_ref[...]
    halo = jnp.where(i % tiles_per_seq == 0, jnp.zeros_like(halo), halo)
    xe = jnp.concatenate([halo, xb_ref[...]], axis=0)

    def cols(c, br):
        return slice(br * D_FF_PAD + c * FF_CHUNK, br * D_FF_PAD + (c + 1) * FF_CHUNK)

    def project(c):
        return [jnp.dot(xe, wup_ref[:, cols(c, br)], preferred_element_type=F32) for br in range(2)]

    def gated(c, us):
        ys = []
        for br, u in enumerate(us):
            cp = cp_ref[:, cols(c, br)]
            y = cp[2:3] * u + cp[1:2] * pltpu.roll(u, 1, 0) + cp[0:1] * pltpu.roll(u, 2, 0) + cp[3:4]
            ys.append(y[CONV_HALO:])
        return (ys[0] * jax.nn.sigmoid(ys[0]) * ys[1]).astype(BF16)

    acc = None
    ahead = [project(c) for c in range(FFN_LOOKAHEAD)]
    for c in range(n_chunks):
        if c + FFN_LOOKAHEAD < n_chunks:
            ahead.append(project(c + FFN_LOOKAHEAD))
        act = gated(c, ahead.pop(0))
        part = jnp.dot(act, wd_ref[c * FF_CHUNK:(c + 1) * FF_CHUNK, :], preferred_element_type=F32)
        acc = part if acc is None else acc + part
    y = _layernorm(alpha * xf_ref[...] + acc, g_ref[...], b_ref[...])
    of_ref[...] = y
    ob_ref[...] = y.astype(ob_ref.dtype)


def _ffn(xb, xf, w_up, conv_p, w_down, g, b, alpha, seq, tm=256):
    m, d = xf.shape
    n_chunks = D_FF_PAD // FF_CHUNK
    halo_blocks = tm // CONV_HALO
    row_spec = pl.BlockSpec((tm, d), lambda i: (i, 0))
    vec_spec = pl.BlockSpec((1, d), lambda i: (0, 0))
    resident = lambda a: pl.BlockSpec(a.shape, lambda i: (0, 0), pipeline_mode=pl.Buffered(1))
    return pl.pallas_call(
        functools.partial(_ffn_kernel, alpha=alpha, tiles_per_seq=seq // tm, n_chunks=n_chunks),
        grid=(m // tm,),
        in_specs=[row_spec,
                  pl.BlockSpec((CONV_HALO, d), lambda i: (jnp.maximum(i * halo_blocks - 1, 0), 0)),
                  row_spec, resident(w_up), resident(conv_p), resident(w_down), vec_spec, vec_spec],
        out_specs=(row_spec, row_spec),
        out_shape=(jax.ShapeDtypeStruct((m, d), F32), jax.ShapeDtypeStruct((m, d), BF16)),
        compiler_params=_params("arbitrary"),
        name="conv_ffn_ln",
    )(xb, xb, xf, w_up, conv_p, w_down, g, b)


def _swap_halves(w):
    half = w.shape[-1] // 2
    return jnp.concatenate([-w[..., half:], w[..., :half]], axis=-1)


def _value_slab(w_v):
    depth, k, _ = w_v.shape
    w = w_v.reshape(depth, k, GROUP_HEADS // 2, LANES)
    w = jnp.pad(w, ((0, 0), (0, 0), (0, 0), (0, V_ROWS - LANES)))
    return jnp.swapaxes(w.reshape(depth, k, V_SLAB), 1, 2)


def _ones_rows(n_slabs):
    row = jnp.arange(n_slabs * V_SLAB) % V_ROWS
    return (row == LANES).astype(F32)[:, None]


def _prep_weights(w_in, mla_w_uq, mla_w_ukv, w_up, conv_w, conv_b, w_down):
    depth, d, _ = w_in.shape
    gw = GROUP_WIDTH
    fox0 = 0
    diff0 = 3 * gw + GROUP_HEADS
    moba0 = diff0 + 3 * gw
    mla0 = moba0 + 3 * gw
    sm = HEAD_DIM ** -0.5 * LOG2E
    sd = DIFF_QK_DIM ** -0.5 * LOG2E
    w_qk = jnp.concatenate([
        w_in[:, :, fox0:fox0 + gw] * sm, w_in[:, :, fox0 + gw:fox0 + 2 * gw],
        w_in[:, :, diff0:diff0 + gw] * sd, w_in[:, :, diff0 + gw:diff0 + 2 * gw],
        w_in[:, :, moba0:moba0 + gw] * sm, w_in[:, :, moba0 + gw:moba0 + 2 * gw]], axis=-1).astype(BF16)
    w_vt = jnp.concatenate([_value_slab(w_in[:, :, g0 + 2 * gw:g0 + 3 * gw])
                            for g0 in (fox0, diff0, moba0)], axis=1).astype(BF16)
    w_gt = jnp.swapaxes(jnp.pad(w_in[:, :, 3 * gw:3 * gw + GROUP_HEADS],
                                ((0, 0), (0, 0), (0, 8 - GROUP_HEADS))), 1, 2).astype(BF16)
    kr0 = mla0 + MLA_Q_LORA + MLA_KV_LORA
    w_kr = w_in[:, :, kr0:kr0 + MLA_ROPE]
    w_lat = jnp.concatenate([w_in[:, :, mla0:kr0], w_kr, _swap_halves(w_kr),
                             jnp.zeros((depth, d, LANES - 2 * MLA_ROPE), F32)], axis=-1).astype(BF16)

    uq = mla_w_uq.reshape(depth, MLA_Q_LORA, GROUP_HEADS, MLA_NOPE + MLA_ROPE)
    pad = LANES - MLA_NOPE - MLA_ROPE
    wqa = jnp.pad(uq, ((0, 0), (0, 0), (0, 0), (0, pad)))
    wqb = jnp.concatenate([jnp.zeros_like(uq[..., :MLA_NOPE]), _swap_halves(uq[..., MLA_NOPE:]),
                           jnp.zeros(uq.shape[:-1] + (pad,), F32)], axis=-1)
    ukv = mla_w_ukv.reshape(depth, MLA_KV_LORA, GROUP_HEADS, MLA_NOPE + HEAD_DIM)
    wk = jnp.pad(ukv[..., :MLA_NOPE], ((0, 0), (0, 0), (0, 0), (0, LANES - MLA_NOPE)))
    flat = lambda a: a.reshape(depth, a.shape[1], -1).astype(BF16)
    wvt = _value_slab(flat(ukv[..., MLA_NOPE:])).astype(BF16)

    ffp = D_FF_PAD - D_FF
    padc = lambda a: jnp.pad(a, ((0, 0), (0, 0), (0, ffp)))
    w_up_p = jnp.concatenate([padc(w_up[:, :, :D_FF]), padc(w_up[:, :, D_FF:])], axis=-1).astype(BF16)
    conv = jnp.concatenate([conv_w, conv_b[:, None, :],
                            jnp.zeros((depth, 8 - CONV_WIDTH - 1, 2 * D_FF), F32)], axis=1)
    conv_p = jnp.concatenate([padc(conv[:, :, :D_FF]), padc(conv[:, :, D_FF:])], axis=-1)
    w_down_p = jnp.pad(w_down, ((0, 0), (0, ffp), (0, 0))).astype(BF16)
    return w_qk, w_vt, w_gt, w_lat, flat(wqa), flat(wqb), flat(wk), wvt, w_up_p, conv_p, w_down_p


def _rope_constants(seq):
    inv = ROPE_THETA ** (-jnp.arange(0, MLA_ROPE // 2, dtype=F32) * 2.0 / MLA_ROPE)
    ang = jnp.arange(seq, dtype=F32)[:, None] * inv[None, :]
    cos, sin = jnp.cos(ang), jnp.sin(ang)
    cos2 = jnp.concatenate([cos, cos], axis=1)
    sin2 = jnp.concatenate([sin, sin], axis=1)
    pad = jnp.zeros((seq, LANES - MLA_NOPE - MLA_ROPE), F32)
    tab_q = jnp.concatenate([jnp.ones((seq, MLA_NOPE), F32), cos2, pad,
                             jnp.zeros((seq, MLA_NOPE), F32), sin2, pad], axis=1)
    tab_k = jnp.concatenate([cos2, sin2, jnp.zeros((seq, LANES - 2 * MLA_ROPE), F32)], axis=1)
    r = jnp.arange(LANES)[:, None]
    c = jnp.arange(GROUP_HEADS * LANES)[None, :]
    e_mat = ((r < 2 * MLA_ROPE) & (c % LANES == MLA_NOPE + r % MLA_ROPE)).astype(BF16)
    return tab_q, tab_k, e_mat


def kernel(x, w_in, b_forget, diff_lambda, diff_subln, mla_q_norm, mla_kv_norm, mla_w_uq, mla_w_ukv,
           rel_bias, w_o, ln1_g, ln1_b, w_up, conv_w, conv_b, w_down, ln2_g, ln2_b):
    batch, seq, d = x.shape
    depth = w_in.shape[0]
    alpha = (2 * depth) ** 0.25
    assert seq % ATTN_TILE == 0 and ATTN_TILE % MOBA_BLOCK == 0 and seq // MOBA_BLOCK <= LANES

    (w_qk, w_vt, w_gt, w_lat, wqa, wqb, wk, wvt, w_up_p, conv_p, w_down_p) = _prep_weights(
        w_in, mla_w_uq, mla_w_ukv, w_up, conv_w, conv_b, w_down)
    w_o_b = w_o.astype(BF16)
    tab_q, tab_k, e_mat = _rope_constants(seq)
    bias_d, bias_p = _bias_blocks(rel_bias, BIAS_BLOCK)
    b_col = jnp.pad(b_forget, ((0, 0), (0, 8 - GROUP_HEADS)))[:, :, None]
    gain = jnp.tile(diff_subln, (1, GROUP_HEADS))[:, None, :]

    xf = x.reshape(batch * seq, d)
    xb = xf.astype(BF16)
    for l in range(depth):
        h_bf, v_t, gate_t, q_m, k_m, v_m = _in_proj(
            xb, w_lat[l], w_qk[l], w_vt[l], _ones_rows(3), w_gt[l], mla_q_norm[l][None], mla_kv_norm[l][None],
            wqa[l], wqb[l], wk[l], wvt[l], _ones_rows(1), e_mat, tab_q, tab_k, seq)
        cum = _fox_cumlog(gate_t, b_col[l], batch, seq)
        fox_o = _fox_attention(h_bf, v_t, cum.T, batch, seq, 0, 0)
        lam_init = 0.8 - 0.6 * math.exp(-0.3 * l)
        lam_c = jnp.concatenate([jnp.full((1, LANES), lam_init, F32),
                                 jnp.full((1, LANES), 1.0 - lam_init, F32),
                                 jnp.zeros((6, LANES), F32)], axis=0)
        diff_o = _diff_attention(h_bf, v_t, bias_d, bias_p, diff_lambda[l], lam_c, gain[l], batch, seq, 2, 1)
        moba_o = _moba_attention(h_bf, v_t, bias_d, bias_p, batch, seq, 4, 2)
        mla_o = _mla_attention(q_m, k_m, v_m, batch, seq)
        xf, xb = _out_proj((fox_o, diff_o, moba_o, mla_o), w_o_b[l], xf, ln1_g[l][None], ln1_b[l][None], alpha)
        xf, xb = _ffn(xb, xf, w_up_p[l], conv_p[l], w_down_p[l], ln2_g[l][None], ln2_b[l][None], alpha, seq)
    return xf.reshape(batch, seq, d)
```

```python
import functools
import math

import jax
import jax.numpy as jnp
from jax import lax
from jax.experimental import pallas as pl
from jax.experimental.pallas import tpu as pltpu

F32 = jnp.float32
BF16 = jnp.bfloat16

HEAD_DIM = 64
GROUP_HEADS = 4
GROUP_WIDTH = GROUP_HEADS * HEAD_DIM
LANES = 128
DIFF_QK_DIM = HEAD_DIM // 2
DIFF_SUBLN_EPS = 1e-5
MOBA_BLOCK = 256
MOBA_TOPK = 3
MLA_Q_LORA = 256
MLA_KV_LORA = 128
MLA_NOPE = 64
MLA_ROPE = 32
ROPE_THETA = 10000.0
REL_BUCKETS = 32
REL_MAX_DIST = 128
D_FF = 2752
FF_CHUNK = 256
D_FF_PAD = -(-D_FF // FF_CHUNK) * FF_CHUNK
FFN_LOOKAHEAD = 2
OUT_SUBTILE = 128
CONV_WIDTH = 3
CONV_HALO = 16
LN_EPS = 1e-5
RMS_EPS = 1e-6
NEG = -1e30
LOG2E = 1.4426950408889634

BIAS_BLOCK = 128
ATTN_TILE = 512
V_ROWS = LANES + 16
V_SLAB = (GROUP_HEADS // 2) * V_ROWS
SCORE_LOOKAHEAD = 3
FAR_UNROLL = 2
SCORE_SLOTS = 4
VMEM_LIMIT = 56 * 1024 * 1024

_NT = (((1,), (1,)), ((), ()))


def _params(*sem):
    return pltpu.CompilerParams(dimension_semantics=sem, vmem_limit_bytes=VMEM_LIMIT)


def _cum_kernel(g_ref, b_ref, o_ref):
    x = g_ref[...] + b_ref[...]
    x = jnp.minimum(x, 0.0) - jnp.log(1.0 + jnp.exp(-jnp.abs(x)))
    n = x.shape[1]
    lane = lax.broadcasted_iota(jnp.int32, x.shape, 1)
    k = 1
    while k < n:
        x = x + jnp.where(lane >= k, pltpu.roll(x, k, 1), 0.0)
        k *= 2
    o_ref[...] = x * LOG2E


def _fox_cumlog(logits_t, b_col, batch, seq):
    return pl.pallas_call(
        _cum_kernel,
        grid=(batch,),
        in_specs=[pl.BlockSpec((8, seq), lambda b: (0, b)),
                  pl.BlockSpec((8, 1), lambda b: (0, 0))],
        out_specs=pl.BlockSpec((8, seq), lambda b: (0, b)),
        out_shape=jax.ShapeDtypeStruct(logits_t.shape, F32),
        compiler_params=_params("arbitrary"),
        name="fox_cumlog",
    )(logits_t, b_col)


def _bias_kernel(rb_ref, d_ref, p_ref, *, t):
    h = pl.program_id(0)
    i = lax.broadcasted_iota(jnp.int32, (t, t), 0)
    j = lax.broadcasted_iota(jnp.int32, (t, t), 1)
    max_exact = REL_BUCKETS // 2
    far = rb_ref[(REL_BUCKETS - 1) * 8 + h]
    for out_ref, off in ((d_ref, 0), (p_ref, t)):
        dist = j - i + off
        d_large = jnp.maximum(dist, max_exact).astype(F32)
        large = max_exact + (jnp.log(d_large / max_exact) / math.log(REL_MAX_DIST / max_exact)
                             * (REL_BUCKETS - max_exact)).astype(jnp.int32)
        large = jnp.minimum(large, REL_BUCKETS - 1)
        bucket = jnp.where(dist < max_exact, dist, large)
        acc = jnp.zeros((t, t), F32)
        for bkt in range(REL_BUCKETS - 1):
            acc = jnp.where(bucket == bkt, (rb_ref[bkt * 8 + h] - far) * LOG2E, acc)
        if off == 0:
            acc = jnp.where(dist >= 0, acc, NEG)
        out_ref[0] = acc


def _bias_blocks(rel_bias, t):
    n_heads = rel_bias.shape[1]
    shape = jax.ShapeDtypeStruct((n_heads, t, t), F32)
    spec = pl.BlockSpec((1, t, t), lambda h: (h, 0, 0))
    return pl.pallas_call(
        functools.partial(_bias_kernel, t=t),
        grid=(n_heads,),
        in_specs=[pl.BlockSpec(memory_space=pltpu.SMEM)],
        out_specs=(spec, spec),
        out_shape=(shape, shape),
        compiler_params=_params("arbitrary"),
        name="rel_bias_blocks",
    )(rel_bias.reshape(-1))


def _online_update(s_t, tile_max, v_t, carry):
    m, acc = carry
    m_new = jnp.maximum(m, tile_max)
    p = jnp.exp2((s_t - m_new).astype(BF16))
    acc = jnp.exp2(m - m_new) * acc + jnp.dot(v_t, p, preferred_element_type=F32)
    return m_new, acc


def _normalized(acc):
    return acc[:LANES] / acc[LANES:LANES + 1]


def _sweep(n, t, qi, raw, fix, values, near_bias, s_ref):
    la = SCORE_LOOKAHEAD
    slots = s_ref.shape[0]
    assert n % slots == 0 and la < slots

    def produce(c, j):
        s_t = raw(c, j)
        s_ref[c % slots] = s_t
        return jnp.max(s_t, axis=0, keepdims=True)

    def step(j, state, kind):
        carries, tops = state
        tops = dict(enumerate(tops))
        nxt = []
        out = []
        for c in range(n):
            if c + la < n:
                tops[c + la] = produce(c + la, j)
            elif kind != "diag":
                nxt.append(produce(c + la - n, j + 1))
            s_t, top = s_ref[c % slots], tops.pop(c)
            if kind != "far":
                s_t = fix(c, s_t, kind)
                top = jnp.max(s_t, axis=0, keepdims=True)
            out.append(_online_update(s_t, top, values(c, j), carries[c]))
        return tuple(out), (tuple(nxt) if nxt else state[1])

    def far_steps(count, state):
        def group(i, s):
            for u in range(FAR_UNROLL):
                s = step(FAR_UNROLL * i + u, s, "far")
            return s

        assert FAR_UNROLL & (FAR_UNROLL - 1) == 0
        groups = lax.shift_right_logical(count, FAR_UNROLL.bit_length() - 1)
        state = lax.fori_loop(0, groups, group, state)
        return lax.fori_loop(groups * FAR_UNROLL, count, lambda j, s: step(j, s, "far"), state)

    state = (tuple(_init_carry(t) for _ in range(n)), tuple(produce(c, 0) for c in range(la)))
    if near_bias:
        state = far_steps(jnp.maximum(qi - 1, 0), state)
        state = lax.cond(qi >= 1, lambda s: step(qi - 1, s, "near"), lambda s: s, state)
    else:
        state = far_steps(qi, state)
    return step(qi, state, "diag")[0]


def _init_carry(t):
    return (jnp.full((1, t), NEG, F32), jnp.zeros((V_ROWS, t), F32))


def _lane_select(x, lo, hi):
    lane = lax.broadcasted_iota(jnp.int32, x.shape, 1)
    return jnp.where((lane >= lo) & (lane < hi), x, jnp.zeros_like(x))


def _merge_pair(a, b):
    row = lax.broadcasted_iota(jnp.int32, a.shape, 0)
    return jnp.where(row < HEAD_DIM, a, b)


def _causal_t(t):
    return (lax.broadcasted_iota(jnp.int32, (t, t), 0) <= lax.broadcasted_iota(jnp.int32, (t, t), 1))


def _band_bias(s_t, d_t, p_t, diag):
    n = s_t.shape[0] // BIAS_BLOCK
    rows = []
    for b in range(n):
        blocks = []
        for a in range(n):
            blk = s_t[b * BIAS_BLOCK:(b + 1) * BIAS_BLOCK, a * BIAS_BLOCK:(a + 1) * BIAS_BLOCK]
            if diag and b > a:
                blk = jnp.full_like(blk, NEG)
            elif diag and b == a:
                blk = blk + d_t
            elif (diag and b == a - 1) or (not diag and b == n - 1 and a == 0):
                blk = blk + p_t
            blocks.append(blk)
        rows.append(jnp.concatenate(blocks, axis=1))
    return jnp.concatenate(rows, axis=0)


def _attn_specs(t, seq, nq, cb, vrow):
    w = GROUP_WIDTH
    return [pl.BlockSpec((t, w), lambda b, i: (b * nq + i, cb)),
            pl.BlockSpec((seq, w), lambda b, i: (b, cb + 1)),
            pl.BlockSpec((V_SLAB, seq), lambda b, i: (vrow, b))]


def _attn_call(kernel_fn, name, batch, seq, t, in_specs, args, scratch=()):
    nq = seq // t
    return pl.pallas_call(
        kernel_fn,
        grid=(batch, nq),
        in_specs=in_specs,
        out_specs=pl.BlockSpec((t, GROUP_WIDTH), lambda b, i: (b * nq + i, 0)),
        out_shape=jax.ShapeDtypeStruct((batch * seq, GROUP_WIDTH), BF16),
        scratch_shapes=[pltpu.VMEM((SCORE_SLOTS, t, t), F32)] + list(scratch),
        compiler_params=_params("arbitrary", "arbitrary"),
        name=name,
    )(*args)


def _head_queries(q_ref, width):
    out = []
    for lo in range(0, GROUP_WIDTH, width):
        p = lo // LANES
        out.append(_lane_select(q_ref[:, p * LANES:(p + 1) * LANES], lo - p * LANES, lo - p * LANES + width))
    return out


def _fox_kernel(q_ref, k_ref, vt_ref, c_ref, o_ref, s_ref, *, t):
    qi = pl.program_id(1)
    causal = _causal_t(t)
    qms = _head_queries(q_ref, HEAD_DIM)

    def raw(h, j):
        sl = slice((h // 2) * LANES, (h // 2 + 1) * LANES)
        start = pl.multiple_of(j * t, t)
        s_t = lax.dot_general(k_ref[pl.ds(start, t), sl], qms[h], _NT, preferred_element_type=F32)
        return s_t - c_ref[pl.ds(start, t), h:h + 1]

    def fix(h, s_t, kind):
        return jnp.where(causal, s_t, NEG)

    def values(h, j):
        return vt_ref[(h // 2) * V_ROWS:(h // 2 + 1) * V_ROWS, pl.ds(pl.multiple_of(j * t, t), t)]

    carries = _sweep(GROUP_HEADS, t, qi, raw, fix, values, False, s_ref)
    outs = [_normalized(acc) for (_, acc) in carries]
    for p in range(GROUP_HEADS // 2):
        o_t = _merge_pair(outs[2 * p], outs[2 * p + 1])
        o_ref[:, p * LANES:(p + 1) * LANES] = o_t.T.astype(o_ref.dtype)


def _fox_attention(h_bf, v_t, cum_cols, batch, seq, cb, vrow, t=ATTN_TILE):
    nq = seq // t
    specs = _attn_specs(t, seq, nq, cb, vrow) + [pl.BlockSpec((seq, 8), lambda b, i: (b, 0))]
    return _attn_call(functools.partial(_fox_kernel, t=t), "fox_attn", batch, seq, t, specs,
                      (h_bf, h_bf, v_t, cum_cols))


def _diff_kernel(q_ref, k_ref, vt_ref, bd_ref, bp_ref, lam_ref, lc_ref, g_ref, o_ref, s_ref, *, t):
    qi = pl.program_id(1)
    lam_p = lam_ref[...]
    lam = (jnp.exp(jnp.sum(lam_p[0:1] * lam_p[1:2], axis=-1, keepdims=True))
           - jnp.exp(jnp.sum(lam_p[2:3] * lam_p[3:4], axis=-1, keepdims=True)) + lc_ref[0:1, 0:1])
    qms = _head_queries(q_ref, DIFF_QK_DIM)

    def raw(c, j):
        sl = slice((c // 4) * LANES, (c // 4 + 1) * LANES)
        return lax.dot_general(k_ref[pl.ds(pl.multiple_of(j * t, t), t), sl], qms[c], _NT,
                               preferred_element_type=F32)

    def fix(c, s_t, kind):
        return _band_bias(s_t, bd_ref[c // 2], bp_ref[c // 2], kind == "diag")

    def values(c, j):
        return vt_ref[(c // 4) * V_ROWS:(c // 4 + 1) * V_ROWS, pl.ds(pl.multiple_of(j * t, t), t)]

    carries = _sweep(2 * GROUP_HEADS, t, qi, raw, fix, values, True, s_ref)
    row = lax.broadcasted_iota(jnp.int32, (LANES, t), 0)
    outs = []
    for h in range(GROUP_HEADS):
        o = _normalized(carries[2 * h][1]) - lam * _normalized(carries[2 * h + 1][1])
        lo = (h % 2) * HEAD_DIM
        o = jnp.where((row >= lo) & (row < lo + HEAD_DIM), o, 0.0)
        ms = jnp.sum(o * o, axis=0, keepdims=True) * (1.0 / HEAD_DIM)
        outs.append(o * lax.rsqrt(ms + DIFF_SUBLN_EPS))
    for p in range(GROUP_HEADS // 2):
        sl = slice(p * LANES, (p + 1) * LANES)
        o_t = _merge_pair(outs[2 * p], outs[2 * p + 1])
        o_ref[:, sl] = (o_t.T * g_ref[:, sl] * lc_ref[1:2, 0:1]).astype(o_ref.dtype)


def _diff_attention(h_bf, v_t, bias_d, bias_p, lam_p, lam_c, gain, batch, seq, cb, vrow, t=ATTN_TILE):
    nq = seq // t
    bias_spec = pl.BlockSpec((GROUP_HEADS, BIAS_BLOCK, BIAS_BLOCK), lambda b, i: (0, 0, 0))
    full = lambda a: pl.BlockSpec(a.shape, lambda b, i: (0, 0))
    specs = _attn_specs(t, seq, nq, cb, vrow) + [bias_spec, bias_spec, full(lam_p), full(lam_c), full(gain)]
    return _attn_call(functools.partial(_diff_kernel, t=t), "diff_attn", batch, seq, t, specs,
                      (h_bf, h_bf, v_t, bias_d, bias_p, lam_p, lam_c, gain))


def _moba_kernel(q_ref, k_ref, vt_ref, bd_ref, bp_ref, o_ref, s_ref, kmean_ref, *, t, n_blocks):
    qi = pl.program_id(1)
    per_tile = t // MOBA_BLOCK

    @pl.when(qi == 0)
    def _():
        kmean_ref[...] = jnp.zeros_like(kmean_ref)
        for blk in range(n_blocks):
            kb = k_ref[blk * MOBA_BLOCK:(blk + 1) * MOBA_BLOCK, :].astype(F32)
            kmean_ref[blk:blk + 1, :] = jnp.sum(kb, axis=0, keepdims=True) * (1.0 / MOBA_BLOCK)

    def tile_block(pos):
        return sum((pos >= n * MOBA_BLOCK).astype(jnp.int32) for n in range(1, per_tile))

    nb = -(-n_blocks // 8) * 8
    blk_t = lax.broadcasted_iota(jnp.int32, (nb, t), 0)
    blk_tf = blk_t.astype(F32)
    own_t = qi * per_tile + tile_block(lax.broadcasted_iota(jnp.int32, (nb, t), 1))
    qms = _head_queries(q_ref, HEAD_DIM)
    q_cats = []
    for h in range(GROUP_HEADS):
        km = kmean_ref[:, (h // 2) * LANES:(h // 2 + 1) * LANES]
        km_hi = km.astype(BF16)
        km_lo = (km - km_hi.astype(F32)).astype(BF16)
        gate = (lax.dot_general(km_hi, qms[h], _NT, preferred_element_type=F32)
                + lax.dot_general(km_lo, qms[h], _NT, preferred_element_type=F32))[:nb]
        g = jnp.where(blk_t < own_t, gate, NEG)
        keep = jnp.where(blk_t == own_t, 1.0, 0.0)
        for _ in range(MOBA_TOPK):
            mx = jnp.max(g, axis=0, keepdims=True)
            first = jnp.min(jnp.where(g == mx, blk_tf, 1e9), axis=0, keepdims=True)
            pick = jnp.where(mx > 0.5 * NEG, jnp.where(blk_tf == first, 1.0, 0.0), 0.0)
            keep = jnp.maximum(keep, pick)
            g = jnp.where(pick > 0.0, NEG, g)
        drop = jnp.concatenate([1.0 - keep, jnp.zeros((LANES - nb, t), F32)], axis=0)
        q_cats.append(jnp.concatenate([qms[h], drop.T.astype(BF16)], axis=1))

    blk_i = lax.broadcasted_iota(jnp.int32, (t, LANES), 1)
    key_blk = blk_i - tile_block(lax.broadcasted_iota(jnp.int32, (t, LANES), 0))

    def raw(h, j):
        sl = slice((h // 2) * LANES, (h // 2 + 1) * LANES)
        reject = jnp.where(key_blk == j * per_tile, NEG, 0.0).astype(BF16)
        k_cat = jnp.concatenate([k_ref[pl.ds(pl.multiple_of(j * t, t), t), sl], reject], axis=1)
        return lax.dot_general(k_cat, q_cats[h], _NT, preferred_element_type=F32)

    def fix(h, s_t, kind):
        return _band_bias(s_t, bd_ref[h], bp_ref[h], kind == "diag")

    def values(h, j):
        return vt_ref[(h // 2) * V_ROWS:(h // 2 + 1) * V_ROWS, pl.ds(pl.multiple_of(j * t, t), t)]

    carries = _sweep(GROUP_HEADS, t, qi, raw, fix, values, True, s_ref)
    outs = [_normalized(acc) for (_, acc) in carries]
    for p in range(GROUP_HEADS // 2):
        o_t = _merge_pair(outs[2 * p], outs[2 * p + 1])
        o_ref[:, p * LANES:(p + 1) * LANES] = o_t.T.astype(o_ref.dtype)


def _moba_attention(h_bf, v_t, bias_d, bias_p, batch, seq, cb, vrow, t=ATTN_TILE):
    nq = seq // t
    bias_spec = pl.BlockSpec((GROUP_HEADS, BIAS_BLOCK, BIAS_BLOCK), lambda b, i: (1, 0, 0))
    specs = _attn_specs(t, seq, nq, cb, vrow) + [bias_spec, bias_spec]
    return _attn_call(functools.partial(_moba_kernel, t=t, n_blocks=seq // MOBA_BLOCK), "moba_attn",
                      batch, seq, t, specs, (h_bf, h_bf, v_t, bias_d, bias_p),
                      scratch=[pltpu.VMEM((LANES, GROUP_WIDTH), F32)])


def _in_proj_kernel(x_ref, wlat_ref, wqk_ref, wvt3_ref, ones3_ref, wgt_ref,
                    gq_ref, gkv_ref, wqa_ref, wqb_ref, wk_ref, wvt_ref, ones_ref, e_ref, tq_ref, tk_ref,
                    h_ref, vt3_ref, gate_ref, q_ref, k_ref, vt_ref):
    x = x_ref[...]
    c = jnp.dot(x, wlat_ref[...], preferred_element_type=F32)
    h_ref[...] = jnp.dot(x, wqk_ref[...], preferred_element_type=F32).astype(h_ref.dtype)
    v3 = lax.dot_general(wvt3_ref[...], x, _NT, preferred_element_type=F32) + ones3_ref[...]
    vt3_ref[...] = v3.astype(vt3_ref.dtype)
    gate_ref[...] = lax.dot_general(wgt_ref[...], x, _NT, preferred_element_type=F32)
    cq = c[:, :MLA_Q_LORA]
    ckv = c[:, MLA_Q_LORA:MLA_Q_LORA + MLA_KV_LORA]
    kr = c[:, MLA_Q_LORA + MLA_KV_LORA:]
    cq = cq * lax.rsqrt(jnp.mean(cq * cq, axis=-1, keepdims=True) + RMS_EPS) * gq_ref[...]
    ckv = ckv * lax.rsqrt(jnp.mean(ckv * ckv, axis=-1, keepdims=True) + RMS_EPS) * gkv_ref[...]
    cq = cq.astype(BF16)
    ckv = ckv.astype(BF16)
    tq = tq_ref[...]
    cos_q = jnp.concatenate([tq[:, :LANES]] * GROUP_HEADS, axis=1)
    sin_q = jnp.concatenate([tq[:, LANES:]] * GROUP_HEADS, axis=1)
    q = (jnp.dot(cq, wqa_ref[...], preferred_element_type=F32) * cos_q
         + jnp.dot(cq, wqb_ref[...], preferred_element_type=F32) * sin_q)
    q_ref[...] = (q * ((MLA_NOPE + MLA_ROPE) ** -0.5 * LOG2E)).astype(q_ref.dtype)
    k_rope = (kr * tk_ref[...]).astype(BF16)
    k = (jnp.dot(ckv, wk_ref[...], preferred_element_type=F32)
         + jnp.dot(k_rope, e_ref[...], preferred_element_type=F32))
    k_ref[...] = k.astype(k_ref.dtype)
    v_t = lax.dot_general(wvt_ref[...], ckv, _NT, preferred_element_type=F32) + ones_ref[...]
    vt_ref[...] = v_t.astype(vt_ref.dtype)


def _in_proj(xb, w_lat, w_qk, w_vt3, ones3, w_gt, gq, gkv, wqa, wqb, wk, wvt, ones, e_mat, tab_q, tab_k,
             seq, tm=512):
    m, d = xb.shape
    ns = seq // tm
    full = lambda a: pl.BlockSpec(a.shape, lambda i: (0, 0), pipeline_mode=pl.Buffered(1))
    rows = lambda w: pl.BlockSpec((tm, w), lambda i: (i, 0))
    cols = lambda r: pl.BlockSpec((r, tm), lambda i: (0, i))
    slot_w = GROUP_HEADS * LANES
    consts = (w_lat, w_qk, w_vt3, ones3, w_gt, gq, gkv, wqa, wqb, wk, wvt, ones, e_mat)
    return pl.pallas_call(
        _in_proj_kernel,
        grid=(m // tm,),
        in_specs=[rows(d)] + [full(a) for a in consts]
                 + [pl.BlockSpec((tm, tab_q.shape[1]), lambda i: (i % ns, 0)),
                    pl.BlockSpec((tm, tab_k.shape[1]), lambda i: (i % ns, 0))],
        out_specs=(rows(w_qk.shape[1]), cols(w_vt3.shape[0]), cols(w_gt.shape[0]),
                   rows(slot_w), rows(slot_w), cols(V_SLAB)),
        out_shape=(jax.ShapeDtypeStruct((m, w_qk.shape[1]), BF16),
                   jax.ShapeDtypeStruct((w_vt3.shape[0], m), BF16),
                   jax.ShapeDtypeStruct((w_gt.shape[0], m), F32),
                   jax.ShapeDtypeStruct((m, slot_w), BF16),
                   jax.ShapeDtypeStruct((m, slot_w), BF16),
                   jax.ShapeDtypeStruct((V_SLAB, m), BF16)),
        compiler_params=_params("arbitrary"),
        name="in_proj",
    )(xb, *consts, tab_q, tab_k)


def _mla_kernel(q_ref, k_ref, vt_ref, o_ref, s_ref, *, t):
    qi = pl.program_id(1)
    causal = _causal_t(t)

    def raw(h, j):
        hsl = slice(h * LANES, (h + 1) * LANES)
        return lax.dot_general(k_ref[pl.ds(pl.multiple_of(j * t, t), t), hsl], q_ref[:, hsl], _NT,
                               preferred_element_type=F32)

    def fix(h, s_t, kind):
        return jnp.where(causal, s_t, NEG)

    def values(h, j):
        return vt_ref[(h // 2) * V_ROWS:(h // 2 + 1) * V_ROWS, pl.ds(pl.multiple_of(j * t, t), t)]

    carries = _sweep(GROUP_HEADS, t, qi, raw, fix, values, False, s_ref)
    outs = [_normalized(acc) for (_, acc) in carries]
    for p in range(GROUP_HEADS // 2):
        o_t = _merge_pair(outs[2 * p], outs[2 * p + 1])
        o_ref[:, p * LANES:(p + 1) * LANES] = o_t.T.astype(o_ref.dtype)


def _mla_attention(q, k, v_t, batch, seq, t=ATTN_TILE):
    nq = seq // t
    slot_w = q.shape[1]
    specs = [pl.BlockSpec((t, slot_w), lambda b, i: (b * nq + i, 0)),
             pl.BlockSpec((seq, slot_w), lambda b, i: (b, 0)),
             pl.BlockSpec((V_SLAB, seq), lambda b, i: (0, b))]
    return _attn_call(functools.partial(_mla_kernel, t=t), "mla_attn", batch, seq, t, specs, (q, k, v_t))


def _layernorm(t, g, b):
    mu = jnp.mean(t, axis=-1, keepdims=True)
    d = t - mu
    var = jnp.mean(d * d, axis=-1, keepdims=True)
    return d * lax.rsqrt(var + LN_EPS) * g + b


def _out_kernel(o1_ref, o2_ref, o3_ref, o4_ref, w_ref, x_ref, g_ref, b_ref, of_ref, ob_ref, *, alpha):
    sub = OUT_SUBTILE
    n_sub = x_ref.shape[0] // sub

    def project(r):
        rows = slice(r * sub, (r + 1) * sub)
        o = jnp.concatenate([o_ref[rows, :] for o_ref in (o1_ref, o2_ref, o3_ref, o4_ref)], axis=1)
        return jnp.dot(o, w_ref[...], preferred_element_type=F32)

    mix_next = project(0)
    for r in range(n_sub):
        mix = mix_next
        if r + 1 < n_sub:
            mix_next = project(r + 1)
        rows = slice(r * sub, (r + 1) * sub)
        y = _layernorm(alpha * x_ref[rows, :] + mix, g_ref[...], b_ref[...])
        of_ref[rows, :] = y
        ob_ref[rows, :] = y.astype(ob_ref.dtype)


def _out_proj(outs, w_o, x, g, b, alpha, tm=512):
    m, d = x.shape
    o_spec = pl.BlockSpec((tm, GROUP_WIDTH), lambda i: (i, 0))
    row_spec = pl.BlockSpec((tm, d), lambda i: (i, 0))
    vec_spec = pl.BlockSpec((1, d), lambda i: (0, 0))
    return pl.pallas_call(
        functools.partial(_out_kernel, alpha=alpha),
        grid=(m // tm,),
        in_specs=[o_spec, o_spec, o_spec, o_spec,
                  pl.BlockSpec(w_o.shape, lambda i: (0, 0)), row_spec, vec_spec, vec_spec],
        out_specs=(row_spec, row_spec),
        out_shape=(jax.ShapeDtypeStruct((m, d), F32), jax.ShapeDtypeStruct((m, d), BF16)),
        compiler_params=_params("arbitrary"),
        name="out_proj_ln",
    )(*outs, w_o, x, g, b)


def _ffn_kernel(xb_ref, xh_ref, xf_ref, wup_ref, cp_ref, wd_ref, g_ref, b_ref,
                of_ref, ob_ref, *, alpha, tiles_per_seq, n_chunks):
    i = pl.program_id(0)
    halo = xh_ref[...]
    halo = jnp.where(i % tiles_per_seq == 0, jnp.zeros_like(halo), halo)
    xe = jnp.concatenate([halo, xb_ref[...]], axis=0)

    def cols(c, br):
        return slice(br * D_FF_PAD + c * FF_CHUNK, br * D_FF_PAD + (c + 1) * FF_CHUNK)

    def project(c):
        return [jnp.dot(xe, wup_ref[:, cols(c, br)], preferred_element_type=F32) for br in range(2)]

    def gated(c, us):
        ys = []
        for br, u in enumerate(us):
            cp = cp_ref[:, cols(c, br)]
            y = cp[2:3] * u + cp[1:2] * pltpu.roll(u, 1, 0) + cp[0:1] * pltpu.roll(u, 2, 0) + cp[3:4]
            ys.append(y[CONV_HALO:])
        return (ys[0] * jax.nn.sigmoid(ys[0]) * ys[1]).astype(BF16)

    acc = None
    ahead = [project(c) for c in range(FFN_LOOKAHEAD)]
    for c in range(n_chunks):
        if c + FFN_LOOKAHEAD < n_chunks:
            ahead.append(project(c + FFN_LOOKAHEAD))
        act = gated(c, ahead.pop(0))
        part = jnp.dot(act, wd_ref[c * FF_CHUNK:(c + 1) * FF_CHUNK, :], preferred_element_type=F32)
        acc = part if acc is None else acc + part
    y = _layernorm(alpha * xf_ref[...] + acc, g_ref[...], b_ref[...])
    of_ref[...] = y
    ob_ref[...] = y.astype(ob_ref.dtype)


def _ffn(xb, xf, w_up, conv_p, w_down, g, b, alpha, seq, tm=256):
    m, d = xf.shape
    n_chunks = D_FF_PAD // FF_CHUNK
    halo_blocks = tm // CONV_HALO
    row_spec = pl.BlockSpec((tm, d), lambda i: (i, 0))
    vec_spec = pl.BlockSpec((1, d), lambda i: (0, 0))
    resident = lambda a: pl.BlockSpec(a.shape, lambda i: (0, 0), pipeline_mode=pl.Buffered(1))
    return pl.pallas_call(
        functools.partial(_ffn_kernel, alpha=alpha, tiles_per_seq=seq // tm, n_chunks=n_chunks),
        grid=(m // tm,),
        in_specs=[row_spec,
                  pl.BlockSpec((CONV_HALO, d), lambda i: (jnp.maximum(i * halo_blocks - 1, 0), 0)),
                  row_spec, resident(w_up), resident(conv_p), resident(w_down), vec_spec, vec_spec],
        out_specs=(row_spec, row_spec),
        out_shape=(jax.ShapeDtypeStruct((m, d), F32), jax.ShapeDtypeStruct((m, d), BF16)),
        compiler_params=_params("arbitrary"),
        name="conv_ffn_ln",
    )(xb, xb, xf, w_up, conv_p, w_down, g, b)


def _swap_halves(w):
    half = w.shape[-1] // 2
    return jnp.concatenate([-w[..., half:], w[..., :half]], axis=-1)


def _value_slab(w_v):
    depth, k, _ = w_v.shape
    w = w_v.reshape(depth, k, GROUP_HEADS // 2, LANES)
    w = jnp.pad(w, ((0, 0), (0, 0), (0, 0), (0, V_ROWS - LANES)))
    return jnp.swapaxes(w.reshape(depth, k, V_SLAB), 1, 2)


def _ones_rows(n_slabs):
    row = jnp.arange(n_slabs * V_SLAB) % V_ROWS
    return (row == LANES).astype(F32)[:, None]


def _prep_weights(w_in, mla_w_uq, mla_w_ukv, w_up, conv_w, conv_b, w_down):
    depth, d, _ = w_in.shape
    gw = GROUP_WIDTH
    fox0 = 0
    diff0 = 3 * gw + GROUP_HEADS
    moba0 = diff0 + 3 * gw
    mla0 = moba0 + 3 * gw
    sm = HEAD_DIM ** -0.5 * LOG2E
    sd = DIFF_QK_DIM ** -0.5 * LOG2E
    w_qk = jnp.concatenate([
        w_in[:, :, fox0:fox0 + gw] * sm, w_in[:, :, fox0 + gw:fox0 + 2 * gw],
        w_in[:, :, diff0:diff0 + gw] * sd, w_in[:, :, diff0 + gw:diff0 + 2 * gw],
        w_in[:, :, moba0:moba0 + gw] * sm, w_in[:, :, moba0 + gw:moba0 + 2 * gw]], axis=-1).astype(BF16)
    w_vt = jnp.concatenate([_value_slab(w_in[:, :, g0 + 2 * gw:g0 + 3 * gw])
                            for g0 in (fox0, diff0, moba0)], axis=1).astype(BF16)
    w_gt = jnp.swapaxes(jnp.pad(w_in[:, :, 3 * gw:3 * gw + GROUP_HEADS],
                                ((0, 0), (0, 0), (0, 8 - GROUP_HEADS))), 1, 2).astype(BF16)
    kr0 = mla0 + MLA_Q_LORA + MLA_KV_LORA
    w_kr = w_in[:, :, kr0:kr0 + MLA_ROPE]
    w_lat = jnp.concatenate([w_in[:, :, mla0:kr0], w_kr, _swap_halves(w_kr),
                             jnp.zeros((depth, d, LANES - 2 * MLA_ROPE), F32)], axis=-1).astype(BF16)

    uq = mla_w_uq.reshape(depth, MLA_Q_LORA, GROUP_HEADS, MLA_NOPE + MLA_ROPE)
    pad = LANES - MLA_NOPE - MLA_ROPE
    wqa = jnp.pad(uq, ((0, 0), (0, 0), (0, 0), (0, pad)))
    wqb = jnp.concatenate([jnp.zeros_like(uq[..., :MLA_NOPE]), _swap_halves(uq[..., MLA_NOPE:]),
                           jnp.zeros(uq.shape[:-1] + (pad,), F32)], axis=-1)
    ukv = mla_w_ukv.reshape(depth, MLA_KV_LORA, GROUP_HEADS, MLA_NOPE + HEAD_DIM)
    wk = jnp.pad(ukv[..., :MLA_NOPE], ((0, 0), (0, 0), (0, 0), (0, LANES - MLA_NOPE)))
    flat = lambda a: a.reshape(depth, a.shape[1], -1).astype(BF16)
    wvt = _value_slab(flat(ukv[..., MLA_NOPE:])).astype(BF16)

    ffp = D_FF_PAD - D_FF
    padc = lambda a: jnp.pad(a, ((0, 0), (0, 0), (0, ffp)))
    w_up_p = jnp.concatenate([padc(w_up[:, :, :D_FF]), padc(w_up[:, :, D_FF:])], axis=-1).astype(BF16)
    conv = jnp.concatenate([conv_w, conv_b[:, None, :],
                            jnp.zeros((depth, 8 - CONV_WIDTH - 1, 2 * D_FF), F32)], axis=1)
    conv_p = jnp.concatenate([padc(conv[:, :, :D_FF]), padc(conv[:, :, D_FF:])], axis=-1)
    w_down_p = jnp.pad(w_down, ((0, 0), (0, ffp), (0, 0))).astype(BF16)
    return w_qk, w_vt, w_gt, w_lat, flat(wqa), flat(wqb), flat(wk), wvt, w_up_p, conv_p, w_down_p


def _rope_constants(seq):
    inv = ROPE_THETA ** (-jnp.arange(0, MLA_ROPE // 2, dtype=F32) * 2.0 / MLA_ROPE)
    ang = jnp.arange(seq, dtype=F32)[:, None] * inv[None, :]
    cos, sin = jnp.cos(ang), jnp.sin(ang)
    cos2 = jnp.concatenate([cos, cos], axis=1)
    sin2 = jnp.concatenate([sin, sin], axis=1)
    pad = jnp.zeros((seq, LANES - MLA_NOPE - MLA_ROPE), F32)
    tab_q = jnp.concatenate([jnp.ones((seq, MLA_NOPE), F32), cos2, pad,
                             jnp.zeros((seq, MLA_NOPE), F32), sin2, pad], axis=1)
    tab_k = jnp.concatenate([cos2, sin2, jnp.zeros((seq, LANES - 2 * MLA_ROPE), F32)], axis=1)
    r = jnp.arange(LANES)[:, None]
    c = jnp.arange(GROUP_HEADS * LANES)[None, :]
    e_mat = ((r < 2 * MLA_ROPE) & (c % LANES == MLA_NOPE + r % MLA_ROPE)).astype(BF16)
    return tab_q, tab_k, e_mat


def kernel(x, w_in, b_forget, diff_lambda, diff_subln, mla_q_norm, mla_kv_norm, mla_w_uq, mla_w_ukv,
           rel_bias, w_o, ln1_g, ln1_b, w_up, conv_w, conv_b, w_down, ln2_g, ln2_b):
    batch, seq, d = x.shape
    depth = w_in.shape[0]
    alpha = (2 * depth) ** 0.25
    assert seq % ATTN_TILE == 0 and ATTN_TILE % MOBA_BLOCK == 0 and seq // MOBA_BLOCK <= LANES

    (w_qk, w_vt, w_gt, w_lat, wqa, wqb, wk, wvt, w_up_p, conv_p, w_down_p) = _prep_weights(
        w_in, mla_w_uq, mla_w_ukv, w_up, conv_w, conv_b, w_down)
    w_o_b = w_o.astype(BF16)
    tab_q, tab_k, e_mat = _rope_constants(seq)
    bias_d, bias_p = _bias_blocks(rel_bias, BIAS_BLOCK)
    b_col = jnp.pad(b_forget, ((0, 0), (0, 8 - GROUP_HEADS)))[:, :, None]
    gain = jnp.tile(diff_subln, (1, GROUP_HEADS))[:, None, :]

    xf = x.reshape(batch * seq, d)
    xb = xf.astype(BF16)
    for l in range(depth):
        h_bf, v_t, gate_t, q_m, k_m, v_m = _in_proj(
            xb, w_lat[l], w_qk[l], w_vt[l], _ones_rows(3), w_gt[l], mla_q_norm[l][None], mla_kv_norm[l][None],
            wqa[l], wqb[l], wk[l], wvt[l], _ones_rows(1), e_mat, tab_q, tab_k, seq)
        cum = _fox_cumlog(gate_t, b_col[l], batch, seq)
        fox_o = _fox_attention(h_bf, v_t, cum.T, batch, seq, 0, 0)
        lam_init = 0.8 - 0.6 * math.exp(-0.3 * l)
        lam_c = jnp.concatenate([jnp.full((1, LANES), lam_init, F32),
                                 jnp.full((1, LANES), 1.0 - lam_init, F32),
                                 jnp.zeros((6, LANES), F32)], axis=0)
        diff_o = _diff_attention(h_bf, v_t, bias_d, bias_p, diff_lambda[l], lam_c, gain[l], batch, seq, 2, 1)
        moba_o = _moba_attention(h_bf, v_t, bias_d, bias_p, batch, seq, 4, 2)
        mla_o = _mla_attention(q_m, k_m, v_m, batch, seq)
        xf, xb = _out_proj((fox_o, diff_o, moba_o, mla_o), w_o_b[l], xf, ln1_g[l][None], ln1_b[l][None], alpha)
        xf, xb = _ffn(xb, xf, w_up_p[l], conv_p[l], w_down_p[l], ln2_g[l][None], ln2_b[l][None], alpha, seq)
    return xf.reshape(batch, seq, d)
```

```python
import functools
import math

import jax
import jax.numpy as jnp
from jax import lax
from jax.experimental import pallas as pl
from jax.experimental.pallas import tpu as pltpu

F32 = jnp.float32
BF16 = jnp.bfloat16

HEAD_DIM = 64
GROUP_HEADS = 4
GROUP_WIDTH = GROUP_HEADS * HEAD_DIM
LANES = 128
DIFF_QK_DIM = HEAD_DIM // 2
DIFF_SUBLN_EPS = 1e-5
MOBA_BLOCK = 256
MOBA_TOPK = 3
MLA_Q_LORA = 256
MLA_KV_LORA = 128
MLA_NOPE = 64
MLA_ROPE = 32
ROPE_THETA = 10000.0
REL_BUCKETS = 32
REL_MAX_DIST = 128
D_FF = 2752
FF_CHUNK = 256
D_FF_PAD = -(-D_FF // FF_CHUNK) * FF_CHUNK
FFN_LOOKAHEAD = 2
OUT_SUBTILE = 128
CONV_WIDTH = 3
CONV_HALO = 16
LN_EPS = 1e-5
RMS_EPS = 1e-6
NEG = -1e30
LOG2E = 1.4426950408889634

BIAS_BLOCK = 128
ATTN_TILE = 512
V_ROWS = HEAD_DIM + 16
V_SLAB = GROUP_HEADS * V_ROWS
SCORE_LOOKAHEAD = 3
FAR_UNROLL = 2
SCORE_SLOTS = 4
VMEM_LIMIT = 56 * 1024 * 1024

_NT = (((1,), (1,)), ((), ()))


def _params(*sem):
    return pltpu.CompilerParams(dimension_semantics=sem, vmem_limit_bytes=VMEM_LIMIT)


def _cum_kernel(g_ref, b_ref, o_ref):
    x = g_ref[...] + b_ref[...]
    x = jnp.minimum(x, 0.0) - jnp.log(1.0 + jnp.exp(-jnp.abs(x)))
    n = x.shape[1]
    lane = lax.broadcasted_iota(jnp.int32, x.shape, 1)
    k = 1
    while k < n:
        x = x + jnp.where(lane >= k, pltpu.roll(x, k, 1), 0.0)
        k *= 2
    o_ref[...] = x * LOG2E


def _fox_cumlog(logits_t, b_col, batch, seq):
    return pl.pallas_call(
        _cum_kernel,
        grid=(batch,),
        in_specs=[pl.BlockSpec((8, seq), lambda b: (0, b)),
                  pl.BlockSpec((8, 1), lambda b: (0, 0))],
        out_specs=pl.BlockSpec((8, seq), lambda b: (0, b)),
        out_shape=jax.ShapeDtypeStruct(logits_t.shape, F32),
        compiler_params=_params("arbitrary"),
        name="fox_cumlog",
    )(logits_t, b_col)


def _bias_kernel(rb_ref, d_ref, p_ref, *, t):
    h = pl.program_id(0)
    i = lax.broadcasted_iota(jnp.int32, (t, t), 0)
    j = lax.broadcasted_iota(jnp.int32, (t, t), 1)
    max_exact = REL_BUCKETS // 2
    far = rb_ref[(REL_BUCKETS - 1) * 8 + h]
    for out_ref, off in ((d_ref, 0), (p_ref, t)):
        dist = j - i + off
        d_large = jnp.maximum(dist, max_exact).astype(F32)
        large = max_exact + (jnp.log(d_large / max_exact) / math.log(REL_MAX_DIST / max_exact)
                             * (REL_BUCKETS - max_exact)).astype(jnp.int32)
        large = jnp.minimum(large, REL_BUCKETS - 1)
        bucket = jnp.where(dist < max_exact, dist, large)
        acc = jnp.zeros((t, t), F32)
        for bkt in range(REL_BUCKETS - 1):
            acc = jnp.where(bucket == bkt, (rb_ref[bkt * 8 + h] - far) * LOG2E, acc)
        if off == 0:
            acc = jnp.where(dist >= 0, acc, NEG)
        out_ref[0] = acc


def _bias_blocks(rel_bias, t):
    n_heads = rel_bias.shape[1]
    shape = jax.ShapeDtypeStruct((n_heads, t, t), F32)
    spec = pl.BlockSpec((1, t, t), lambda h: (h, 0, 0))
    return pl.pallas_call(
        functools.partial(_bias_kernel, t=t),
        grid=(n_heads,),
        in_specs=[pl.BlockSpec(memory_space=pltpu.SMEM)],
        out_specs=(spec, spec),
        out_shape=(shape, shape),
        compiler_params=_params("arbitrary"),
        name="rel_bias_blocks",
    )(rel_bias.reshape(-1))


def _online_update(s_t, tile_max, v_t, carry):
    m, acc = carry
    m_new = jnp.maximum(m, tile_max)
    p = jnp.exp2((s_t - m_new).astype(BF16))
    acc = jnp.exp2(m - m_new) * acc + jnp.dot(v_t, p, preferred_element_type=F32)
    return m_new, acc


def _normalized(acc):
    return acc[:HEAD_DIM] / acc[HEAD_DIM:HEAD_DIM + 1]


def _sweep(n, t, qi, raw, fix, values, near_bias, s_ref):
    la = SCORE_LOOKAHEAD
    slots = s_ref.shape[0]
    assert n % slots == 0 and la < slots

    def produce(c, j):
        s_t = raw(c, j)
        s_ref[c % slots] = s_t
        return jnp.max(s_t, axis=0, keepdims=True)

    def step(j, state, kind):
        carries, tops = state
        tops = dict(enumerate(tops))
        nxt = []
        out = []
        for c in range(n):
            if c + la < n:
                tops[c + la] = produce(c + la, j)
            elif kind != "diag":
                nxt.append(produce(c + la - n, j + 1))
            s_t, top = s_ref[c % slots], tops.pop(c)
            if kind != "far":
                s_t = fix(c, s_t, kind)
                top = jnp.max(s_t, axis=0, keepdims=True)
            out.append(_online_update(s_t, top, values(c, j), carries[c]))
        return tuple(out), (tuple(nxt) if nxt else state[1])

    def far_steps(count, state):
        def group(i, s):
            for u in range(FAR_UNROLL):
                s = step(FAR_UNROLL * i + u, s, "far")
            return s

        assert FAR_UNROLL & (FAR_UNROLL - 1) == 0
        groups = lax.shift_right_logical(count, FAR_UNROLL.bit_length() - 1)
        state = lax.fori_loop(0, groups, group, state)
        return lax.fori_loop(groups * FAR_UNROLL, count, lambda j, s: step(j, s, "far"), state)

    state = (tuple(_init_carry(t) for _ in range(n)), tuple(produce(c, 0) for c in range(la)))
    if near_bias:
        state = far_steps(jnp.maximum(qi - 1, 0), state)
        state = lax.cond(qi >= 1, lambda s: step(qi - 1, s, "near"), lambda s: s, state)
    else:
        state = far_steps(qi, state)
    return step(qi, state, "diag")[0]


def _init_carry(t):
    return (jnp.full((1, t), NEG, F32), jnp.zeros((V_ROWS, t), F32))


def _lane_select(x, lo, hi):
    lane = lax.broadcasted_iota(jnp.int32, x.shape, 1)
    return jnp.where((lane >= lo) & (lane < hi), x, jnp.zeros_like(x))


def _merge_pair(a, b):
    return jnp.concatenate([a, b], axis=0)


def _causal_t(t):
    return (lax.broadcasted_iota(jnp.int32, (t, t), 0) <= lax.broadcasted_iota(jnp.int32, (t, t), 1))


def _band_bias(s_t, d_t, p_t, diag):
    n = s_t.shape[0] // BIAS_BLOCK
    rows = []
    for b in range(n):
        blocks = []
        for a in range(n):
            blk = s_t[b * BIAS_BLOCK:(b + 1) * BIAS_BLOCK, a * BIAS_BLOCK:(a + 1) * BIAS_BLOCK]
            if diag and b > a:
                blk = jnp.full_like(blk, NEG)
            elif diag and b == a:
                blk = blk + d_t
            elif (diag and b == a - 1) or (not diag and b == n - 1 and a == 0):
                blk = blk + p_t
            blocks.append(blk)
        rows.append(jnp.concatenate(blocks, axis=1))
    return jnp.concatenate(rows, axis=0)


def _attn_specs(t, seq, nq, cb, vrow):
    w = GROUP_WIDTH
    return [pl.BlockSpec((t, w), lambda b, i: (b * nq + i, cb)),
            pl.BlockSpec((seq, w), lambda b, i: (b, cb + 1)),
            pl.BlockSpec((V_SLAB, seq), lambda b, i: (vrow, b))]


def _attn_call(kernel_fn, name, batch, seq, t, in_specs, args, scratch=()):
    nq = seq // t
    return pl.pallas_call(
        kernel_fn,
        grid=(batch, nq),
        in_specs=in_specs,
        out_specs=pl.BlockSpec((t, GROUP_WIDTH), lambda b, i: (b * nq + i, 0)),
        out_shape=jax.ShapeDtypeStruct((batch * seq, GROUP_WIDTH), BF16),
        scratch_shapes=[pltpu.VMEM((SCORE_SLOTS, t, t), F32)] + list(scratch),
        compiler_params=_params("arbitrary", "arbitrary"),
        name=name,
    )(*args)


def _head_queries(q_ref, width):
    out = []
    for lo in range(0, GROUP_WIDTH, width):
        p = lo // LANES
        out.append(_lane_select(q_ref[:, p * LANES:(p + 1) * LANES], lo - p * LANES, lo - p * LANES + width))
    return out


def _fox_kernel(q_ref, k_ref, vt_ref, c_ref, o_ref, s_ref, *, t):
    qi = pl.program_id(1)
    causal = _causal_t(t)
    qms = _head_queries(q_ref, HEAD_DIM)

    def raw(h, j):
        sl = slice((h // 2) * LANES, (h // 2 + 1) * LANES)
        start = pl.multiple_of(j * t, t)
        s_t = lax.dot_general(k_ref[pl.ds(start, t), sl], qms[h], _NT, preferred_element_type=F32)
        return s_t - c_ref[pl.ds(start, t), h:h + 1]

    def fix(h, s_t, kind):
        return jnp.where(causal, s_t, NEG)

    def values(h, j):
        return vt_ref[h * V_ROWS:(h + 1) * V_ROWS, pl.ds(pl.multiple_of(j * t, t), t)]

    carries = _sweep(GROUP_HEADS, t, qi, raw, fix, values, False, s_ref)
    outs = [_normalized(acc) for (_, acc) in carries]
    for p in range(GROUP_HEADS // 2):
        o_t = _merge_pair(outs[2 * p], outs[2 * p + 1])
        o_ref[:, p * LANES:(p + 1) * LANES] = o_t.T.astype(o_ref.dtype)


def _fox_attention(h_bf, v_t, cum_cols, batch, seq, cb, vrow, t=ATTN_TILE):
    nq = seq // t
    specs = _attn_specs(t, seq, nq, cb, vrow) + [pl.BlockSpec((seq, 8), lambda b, i: (b, 0))]
    return _attn_call(functools.partial(_fox_kernel, t=t), "fox_attn", batch, seq, t, specs,
                      (h_bf, h_bf, v_t, cum_cols))


def _diff_kernel(q_ref, k_ref, vt_ref, bd_ref, bp_ref, lam_ref, lc_ref, g_ref, o_ref, s_ref, *, t):
    qi = pl.program_id(1)
    lam_p = lam_ref[...]
    lam = (jnp.exp(jnp.sum(lam_p[0:1] * lam_p[1:2], axis=-1, keepdims=True))
           - jnp.exp(jnp.sum(lam_p[2:3] * lam_p[3:4], axis=-1, keepdims=True)) + lc_ref[0:1, 0:1])
    qms = _head_queries(q_ref, DIFF_QK_DIM)

    def raw(c, j):
        sl = slice((c // 4) * LANES, (c // 4 + 1) * LANES)
        return lax.dot_general(k_ref[pl.ds(pl.multiple_of(j * t, t), t), sl], qms[c], _NT,
                               preferred_element_type=F32)

    def fix(c, s_t, kind):
        return _band_bias(s_t, bd_ref[c // 2], bp_ref[c // 2], kind == "diag")

    def values(c, j):
        return vt_ref[(c // 2) * V_ROWS:(c // 2 + 1) * V_ROWS, pl.ds(pl.multiple_of(j * t, t), t)]

    carries = _sweep(2 * GROUP_HEADS, t, qi, raw, fix, values, True, s_ref)
    outs = []
    for h in range(GROUP_HEADS):
        o = _normalized(carries[2 * h][1]) - lam * _normalized(carries[2 * h + 1][1])
        ms = jnp.sum(o * o, axis=0, keepdims=True) * (1.0 / HEAD_DIM)
        outs.append(o * lax.rsqrt(ms + DIFF_SUBLN_EPS))
    for p in range(GROUP_HEADS // 2):
        sl = slice(p * LANES, (p + 1) * LANES)
        o_t = _merge_pair(outs[2 * p], outs[2 * p + 1])
        o_ref[:, sl] = (o_t.T * g_ref[:, sl] * lc_ref[1:2, 0:1]).astype(o_ref.dtype)


def _diff_attention(h_bf, v_t, bias_d, bias_p, lam_p, lam_c, gain, batch, seq, cb, vrow, t=ATTN_TILE):
    nq = seq // t
    bias_spec = pl.BlockSpec((GROUP_HEADS, BIAS_BLOCK, BIAS_BLOCK), lambda b, i: (0, 0, 0))
    full = lambda a: pl.BlockSpec(a.shape, lambda b, i: (0, 0))
    specs = _attn_specs(t, seq, nq, cb, vrow) + [bias_spec, bias_spec, full(lam_p), full(lam_c), full(gain)]
    return _attn_call(functools.partial(_diff_kernel, t=t), "diff_attn", batch, seq, t, specs,
                      (h_bf, h_bf, v_t, bias_d, bias_p, lam_p, lam_c, gain))


def _moba_kernel(q_ref, k_ref, vt_ref, bd_ref, bp_ref, o_ref, s_ref, kmean_ref, *, t, n_blocks):
    qi = pl.program_id(1)
    per_tile = t // MOBA_BLOCK

    @pl.when(qi == 0)
    def _():
        kmean_ref[...] = jnp.zeros_like(kmean_ref)
        for blk in range(n_blocks):
            kb = k_ref[blk * MOBA_BLOCK:(blk + 1) * MOBA_BLOCK, :].astype(F32)
            kmean_ref[blk:blk + 1, :] = jnp.sum(kb, axis=0, keepdims=True) * (1.0 / MOBA_BLOCK)

    def tile_block(pos):
        return sum((pos >= n * MOBA_BLOCK).astype(jnp.int32) for n in range(1, per_tile))

    nb = -(-n_blocks // 8) * 8
    blk_t = lax.broadcasted_iota(jnp.int32, (nb, t), 0)
    blk_tf = blk_t.astype(F32)
    own_t = qi * per_tile + tile_block(lax.broadcasted_iota(jnp.int32, (nb, t), 1))
    qms = _head_queries(q_ref, HEAD_DIM)
    q_cats = []
    for h in range(GROUP_HEADS):
        km = kmean_ref[:, (h // 2) * LANES:(h // 2 + 1) * LANES]
        km_hi = km.astype(BF16)
        km_lo = (km - km_hi.astype(F32)).astype(BF16)
        gate = (lax.dot_general(km_hi, qms[h], _NT, preferred_element_type=F32)
                + lax.dot_general(km_lo, qms[h], _NT, preferred_element_type=F32))[:nb]
        g = jnp.where(blk_t < own_t, gate, NEG)
        keep = jnp.where(blk_t == own_t, 1.0, 0.0)
        for _ in range(MOBA_TOPK):
            mx = jnp.max(g, axis=0, keepdims=True)
            first = jnp.min(jnp.where(g == mx, blk_tf, 1e9), axis=0, keepdims=True)
            pick = jnp.where(mx > 0.5 * NEG, jnp.where(blk_tf == first, 1.0, 0.0), 0.0)
            keep = jnp.maximum(keep, pick)
            g = jnp.where(pick > 0.0, NEG, g)
        drop = jnp.concatenate([1.0 - keep, jnp.zeros((LANES - nb, t), F32)], axis=0)
        q_cats.append(jnp.concatenate([qms[h], drop.T.astype(BF16)], axis=1))

    blk_i = lax.broadcasted_iota(jnp.int32, (t, LANES), 1)
    key_blk = blk_i - tile_block(lax.broadcasted_iota(jnp.int32, (t, LANES), 0))

    def raw(h, j):
        sl = slice((h // 2) * LANES, (h // 2 + 1) * LANES)
        reject = jnp.where(key_blk == j * per_tile, NEG, 0.0).astype(BF16)
        k_cat = jnp.concatenate([k_ref[pl.ds(pl.multiple_of(j * t, t), t), sl], reject], axis=1)
        return lax.dot_general(k_cat, q_cats[h], _NT, preferred_element_type=F32)

    def fix(h, s_t, kind):
        return _band_bias(s_t, bd_ref[h], bp_ref[h], kind == "diag")

    def values(h, j):
        return vt_ref[h * V_ROWS:(h + 1) * V_ROWS, pl.ds(pl.multiple_of(j * t, t), t)]

    carries = _sweep(GROUP_HEADS, t, qi, raw, fix, values, True, s_ref)
    outs = [_normalized(acc) for (_, acc) in carries]
    for p in range(GROUP_HEADS // 2):
        o_t = _merge_pair(outs[2 * p], outs[2 * p + 1])
        o_ref[:, p * LANES:(p + 1) * LANES] = o_t.T.astype(o_ref.dtype)


def _moba_attention(h_bf, v_t, bias_d, bias_p, batch, seq, cb, vrow, t=ATTN_TILE):
    nq = seq // t
    bias_spec = pl.BlockSpec((GROUP_HEADS, BIAS_BLOCK, BIAS_BLOCK), lambda b, i: (1, 0, 0))
    specs = _attn_specs(t, seq, nq, cb, vrow) + [bias_spec, bias_spec]
    return _attn_call(functools.partial(_moba_kernel, t=t, n_blocks=seq // MOBA_BLOCK), "moba_attn",
                      batch, seq, t, specs, (h_bf, h_bf, v_t, bias_d, bias_p),
                      scratch=[pltpu.VMEM((LANES, GROUP_WIDTH), F32)])


def _in_proj_kernel(x_ref, wlat_ref, wqk_ref, wvt3_ref, ones3_ref, wgt_ref,
                    gq_ref, gkv_ref, wqa_ref, wqb_ref, wk_ref, wvt_ref, ones_ref, e_ref, tq_ref, tk_ref,
                    h_ref, vt3_ref, gate_ref, q_ref, k_ref, vt_ref):
    x = x_ref[...]
    c = jnp.dot(x, wlat_ref[...], preferred_element_type=F32)
    h_ref[...] = jnp.dot(x, wqk_ref[...], preferred_element_type=F32).astype(h_ref.dtype)
    v3 = lax.dot_general(wvt3_ref[...], x, _NT, preferred_element_type=F32) + ones3_ref[...]
    vt3_ref[...] = v3.astype(vt3_ref.dtype)
    gate_ref[...] = lax.dot_general(wgt_ref[...], x, _NT, preferred_element_type=F32)
    cq = c[:, :MLA_Q_LORA]
    ckv = c[:, MLA_Q_LORA:MLA_Q_LORA + MLA_KV_LORA]
    kr = c[:, MLA_Q_LORA + MLA_KV_LORA:]
    cq = cq * lax.rsqrt(jnp.mean(cq * cq, axis=-1, keepdims=True) + RMS_EPS) * gq_ref[...]
    ckv = ckv * lax.rsqrt(jnp.mean(ckv * ckv, axis=-1, keepdims=True) + RMS_EPS) * gkv_ref[...]
    cq = cq.astype(BF16)
    ckv = ckv.astype(BF16)
    tq = tq_ref[...]
    cos_q = jnp.concatenate([tq[:, :LANES]] * GROUP_HEADS, axis=1)
    sin_q = jnp.concatenate([tq[:, LANES:]] * GROUP_HEADS, axis=1)
    q = (jnp.dot(cq, wqa_ref[...], preferred_element_type=F32) * cos_q
         + jnp.dot(cq, wqb_ref[...], preferred_element_type=F32) * sin_q)
    q_ref[...] = (q * ((MLA_NOPE + MLA_ROPE) ** -0.5 * LOG2E)).astype(q_ref.dtype)
    k_rope = (kr * tk_ref[...]).astype(BF16)
    k = (jnp.dot(ckv, wk_ref[...], preferred_element_type=F32)
         + jnp.dot(k_rope, e_ref[...], preferred_element_type=F32))
    k_ref[...] = k.astype(k_ref.dtype)
    v_t = lax.dot_general(wvt_ref[...], ckv, _NT, preferred_element_type=F32) + ones_ref[...]
    vt_ref[...] = v_t.astype(vt_ref.dtype)


def _in_proj(xb, w_lat, w_qk, w_vt3, ones3, w_gt, gq, gkv, wqa, wqb, wk, wvt, ones, e_mat, tab_q, tab_k,
             seq, tm=512):
    m, d = xb.shape
    ns = seq // tm
    full = lambda a: pl.BlockSpec(a.shape, lambda i: (0, 0), pipeline_mode=pl.Buffered(1))
    rows = lambda w: pl.BlockSpec((tm, w), lambda i: (i, 0))
    cols = lambda r: pl.BlockSpec((r, tm), lambda i: (0, i))
    slot_w = GROUP_HEADS * LANES
    consts = (w_lat, w_qk, w_vt3, ones3, w_gt, gq, gkv, wqa, wqb, wk, wvt, ones, e_mat)
    return pl.pallas_call(
        _in_proj_kernel,
        grid=(m // tm,),
        in_specs=[rows(d)] + [full(a) for a in consts]
                 + [pl.BlockSpec((tm, tab_q.shape[1]), lambda i: (i % ns, 0)),
                    pl.BlockSpec((tm, tab_k.shape[1]), lambda i: (i % ns, 0))],
        out_specs=(rows(w_qk.shape[1]), cols(w_vt3.shape[0]), cols(w_gt.shape[0]),
                   rows(slot_w), rows(slot_w), cols(V_SLAB)),
        out_shape=(jax.ShapeDtypeStruct((m, w_qk.shape[1]), BF16),
                   jax.ShapeDtypeStruct((w_vt3.shape[0], m), BF16),
                   jax.ShapeDtypeStruct((w_gt.shape[0], m), F32),
                   jax.ShapeDtypeStruct((m, slot_w), BF16),
                   jax.ShapeDtypeStruct((m, slot_w), BF16),
                   jax.ShapeDtypeStruct((V_SLAB, m), BF16)),
        compiler_params=_params("arbitrary"),
        name="in_proj",
    )(xb, *consts, tab_q, tab_k)


def _mla_kernel(q_ref, k_ref, vt_ref, o_ref, s_ref, *, t):
    qi = pl.program_id(1)
    causal = _causal_t(t)

    def raw(h, j):
        hsl = slice(h * LANES, (h + 1) * LANES)
        return lax.dot_general(k_ref[pl.ds(pl.multiple_of(j * t, t), t), hsl], q_ref[:, hsl], _NT,
                               preferred_element_type=F32)

    def fix(h, s_t, kind):
        return jnp.where(causal, s_t, NEG)

    def values(h, j):
        return vt_ref[h * V_ROWS:(h + 1) * V_ROWS, pl.ds(pl.multiple_of(j * t, t), t)]

    carries = _sweep(GROUP_HEADS, t, qi, raw, fix, values, False, s_ref)
    outs = [_normalized(acc) for (_, acc) in carries]
    for p in range(GROUP_HEADS // 2):
        o_t = _merge_pair(outs[2 * p], outs[2 * p + 1])
        o_ref[:, p * LANES:(p + 1) * LANES] = o_t.T.astype(o_ref.dtype)


def _mla_attention(q, k, v_t, batch, seq, t=ATTN_TILE):
    nq = seq // t
    slot_w = q.shape[1]
    specs = [pl.BlockSpec((t, slot_w), lambda b, i: (b * nq + i, 0)),
             pl.BlockSpec((seq, slot_w), lambda b, i: (b, 0)),
             pl.BlockSpec((V_SLAB, seq), lambda b, i: (0, b))]
    return _attn_call(functools.partial(_mla_kernel, t=t), "mla_attn", batch, seq, t, specs, (q, k, v_t))


def _layernorm(t, g, b):
    mu = jnp.mean(t, axis=-1, keepdims=True)
    d = t - mu
    var = jnp.mean(d * d, axis=-1, keepdims=True)
    return d * lax.rsqrt(var + LN_EPS) * g + b


def _out_kernel(o1_ref, o2_ref, o3_ref, o4_ref, w_ref, x_ref, g_ref, b_ref, of_ref, ob_ref, *, alpha):
    sub = OUT_SUBTILE
    n_sub = x_ref.shape[0] // sub

    def project(r):
        rows = slice(r * sub, (r + 1) * sub)
        o = jnp.concatenate([o_ref[rows, :] for o_ref in (o1_ref, o2_ref, o3_ref, o4_ref)], axis=1)
        return jnp.dot(o, w_ref[...], preferred_element_type=F32)

    mix_next = project(0)
    for r in range(n_sub):
        mix = mix_next
        if r + 1 < n_sub:
            mix_next = project(r + 1)
        rows = slice(r * sub, (r + 1) * sub)
        y = _layernorm(alpha * x_ref[rows, :] + mix, g_ref[...], b_ref[...])
        of_ref[rows, :] = y
        ob_ref[rows, :] = y.astype(ob_ref.dtype)


def _out_proj(outs, w_o, x, g, b, alpha, tm=512):
    m, d = x.shape
    o_spec = pl.BlockSpec((tm, GROUP_WIDTH), lambda i: (i, 0))
    row_spec = pl.BlockSpec((tm, d), lambda i: (i, 0))
    vec_spec = pl.BlockSpec((1, d), lambda i: (0, 0))
    return pl.pallas_call(
        functools.partial(_out_kernel, alpha=alpha),
        grid=(m // tm,),
        in_specs=[o_spec, o_spec, o_spec, o_spec,
                  pl.BlockSpec(w_o.shape, lambda i: (0, 0)), row_spec, vec_spec, vec_spec],
        out_specs=(row_spec, row_spec),
        out_shape=(jax.ShapeDtypeStruct((m, d), F32), jax.ShapeDtypeStruct((m, d), BF16)),
        compiler_params=_params("arbitrary"),
        name="out_proj_ln",
    )(*outs, w_o, x, g, b)


def _ffn_kernel(xb_ref, xh_ref, xf_ref, wup_ref, cp_ref, wd_ref, g_ref, b_ref,
                of_ref, ob_ref, *, alpha, tiles_per_seq, n_chunks):
    i = pl.program_id(0)
    halo = xh_ref[...]
    halo = jnp.where(i % tiles_per_seq == 0, jnp.zeros_like(halo), halo)
    xe = jnp.concatenate([halo, xb_ref[...]], axis=0)

    def cols(c, br):
        return slice(br * D_FF_PAD + c * FF_CHUNK, br * D_FF_PAD + (c + 1) * FF_CHUNK)

    def project(c):
        return [jnp.dot(xe, wup_ref[:, cols(c, br)], preferred_element_type=F32) for br in range(2)]

    def gated(c, us):
        ys = []
        for br, u in enumerate(us):
            cp = cp_ref[:, cols(c, br)]
            y = cp[2:3] * u + cp[1:2] * pltpu.roll(u, 1, 0) + cp[0:1] * pltpu.roll(u, 2, 0) + cp[3:4]
            ys.append(y[CONV_HALO:])
        return (ys[0] * jax.nn.sigmoid(ys[0]) * ys[1]).astype(BF16)

    acc = None
    ahead = [project(c) for c in range(FFN_LOOKAHEAD)]
    for c in range(n_chunks):
        if c + FFN_LOOKAHEAD < n_chunks:
            ahead.append(project(c + FFN_LOOKAHEAD))
        act = gated(c, ahead.pop(0))
        part = jnp.dot(act, wd_ref[c * FF_CHUNK:(c + 1) * FF_CHUNK, :], preferred_element_type=F32)
        acc = part if acc is None else acc + part
    y = _layernorm(alpha * xf_ref[...] + acc, g_ref[...], b_ref[...])
    of_ref[...] = y
    ob_ref[...] = y.astype(ob_ref.dtype)


def _ffn(xb, xf, w_up, conv_p, w_down, g, b, alpha, seq, tm=256):
    m, d = xf.shape
    n_chunks = D_FF_PAD // FF_CHUNK
    halo_blocks = tm // CONV_HALO
    row_spec = pl.BlockSpec((tm, d), lambda i: (i, 0))
    vec_spec = pl.BlockSpec((1, d), lambda i: (0, 0))
    resident = lambda a: pl.BlockSpec(a.shape, lambda i: (0, 0), pipeline_mode=pl.Buffered(1))
    return pl.pallas_call(
        functools.partial(_ffn_kernel, alpha=alpha, tiles_per_seq=seq // tm, n_chunks=n_chunks),
        grid=(m // tm,),
        in_specs=[row_spec,
                  pl.BlockSpec((CONV_HALO, d), lambda i: (jnp.maximum(i * halo_blocks - 1, 0), 0)),
                  row_spec, resident(w_up), resident(conv_p), resident(w_down), vec_spec, vec_spec],
        out_specs=(row_spec, row_spec),
        out_shape=(jax.ShapeDtypeStruct((m, d), F32), jax.ShapeDtypeStruct((m, d), BF16)),
        compiler_params=_params("arbitrary"),
        name="conv_ffn_ln",
    )(xb, xb, xf, w_up, conv_p, w_down, g, b)


def _swap_halves(w):
    half = w.shape[-1] // 2
    return jnp.concatenate([-w[..., half:], w[..., :half]], axis=-1)


def _value_slab(w_v):
    depth, k, _ = w_v.shape
    w = w_v.reshape(depth, k, GROUP_HEADS, HEAD_DIM)
    w = jnp.pad(w, ((0, 0), (0, 0), (0, 0), (0, V_ROWS - HEAD_DIM)))
    return jnp.swapaxes(w.reshape(depth, k, V_SLAB), 1, 2)


def _ones_rows(n_slabs):
    row = jnp.arange(n_slabs * V_SLAB) % V_ROWS
    return (row == HEAD_DIM).astype(F32)[:, None]


def _prep_weights(w_in, mla_w_uq, mla_w_ukv, w_up, conv_w, conv_b, w_down):
    depth, d, _ = w_in.shape
    gw = GROUP_WIDTH
    fox0 = 0
    diff0 = 3 * gw + GROUP_HEADS
    moba0 = diff0 + 3 * gw
    mla0 = moba0 + 3 * gw
    sm = HEAD_DIM ** -0.5 * LOG2E
    sd = DIFF_QK_DIM ** -0.5 * LOG2E
    w_qk = jnp.concatenate([
        w_in[:, :, fox0:fox0 + gw] * sm, w_in[:, :, fox0 + gw:fox0 + 2 * gw],
        w_in[:, :, diff0:diff0 + gw] * sd, w_in[:, :, diff0 + gw:diff0 + 2 * gw],
        w_in[:, :, moba0:moba0 + gw] * sm, w_in[:, :, moba0 + gw:moba0 + 2 * gw]], axis=-1).astype(BF16)
    w_vt = jnp.concatenate([_value_slab(w_in[:, :, g0 + 2 * gw:g0 + 3 * gw])
                            for g0 in (fox0, diff0, moba0)], axis=1).astype(BF16)
    w_gt = jnp.swapaxes(jnp.pad(w_in[:, :, 3 * gw:3 * gw + GROUP_HEADS],
                                ((0, 0), (0, 0), (0, 8 - GROUP_HEADS))), 1, 2).astype(BF16)
    kr0 = mla0 + MLA_Q_LORA + MLA_KV_LORA
    w_kr = w_in[:, :, kr0:kr0 + MLA_ROPE]
    w_lat = jnp.concatenate([w_in[:, :, mla0:kr0], w_kr, _swap_halves(w_kr),
                             jnp.zeros((depth, d, LANES - 2 * MLA_ROPE), F32)], axis=-1).astype(BF16)

    uq = mla_w_uq.reshape(depth, MLA_Q_LORA, GROUP_HEADS, MLA_NOPE + MLA_ROPE)
    pad = LANES - MLA_NOPE - MLA_ROPE
    wqa = jnp.pad(uq, ((0, 0), (0, 0), (0, 0), (0, pad)))
    wqb = jnp.concatenate([jnp.zeros_like(uq[..., :MLA_NOPE]), _swap_halves(uq[..., MLA_NOPE:]),
                           jnp.zeros(uq.shape[:-1] + (pad,), F32)], axis=-1)
    ukv = mla_w_ukv.reshape(depth, MLA_KV_LORA, GROUP_HEADS, MLA_NOPE + HEAD_DIM)
    wk = jnp.pad(ukv[..., :MLA_NOPE], ((0, 0), (0, 0), (0, 0), (0, LANES - MLA_NOPE)))
    flat = lambda a: a.reshape(depth, a.shape[1], -1).astype(BF16)
    wvt = _value_slab(flat(ukv[..., MLA_NOPE:])).astype(BF16)

    ffp = D_FF_PAD - D_FF
    padc = lambda a: jnp.pad(a, ((0, 0), (0, 0), (0, ffp)))
    w_up_p = jnp.concatenate([padc(w_up[:, :, :D_FF]), padc(w_up[:, :, D_FF:])], axis=-1).astype(BF16)
    conv = jnp.concatenate([conv_w, conv_b[:, None, :],
                            jnp.zeros((depth, 8 - CONV_WIDTH - 1, 2 * D_FF), F32)], axis=1)
    conv_p = jnp.concatenate([padc(conv[:, :, :D_FF]), padc(conv[:, :, D_FF:])], axis=-1)
    w_down_p = jnp.pad(w_down, ((0, 0), (0, ffp), (0, 0))).astype(BF16)
    return w_qk, w_vt, w_gt, w_lat, flat(wqa), flat(wqb), flat(wk), wvt, w_up_p, conv_p, w_down_p


def _rope_constants(seq):
    inv = ROPE_THETA ** (-jnp.arange(0, MLA_ROPE // 2, dtype=F32) * 2.0 / MLA_ROPE)
    ang = jnp.arange(seq, dtype=F32)[:, None] * inv[None, :]
    cos, sin = jnp.cos(ang), jnp.sin(ang)
    cos2 = jnp.concatenate([cos, cos], axis=1)
    sin2 = jnp.concatenate([sin, sin], axis=1)
    pad = jnp.zeros((seq, LANES - MLA_NOPE - MLA_ROPE), F32)
    tab_q = jnp.concatenate([jnp.ones((seq, MLA_NOPE), F32), cos2, pad,
                             jnp.zeros((seq, MLA_NOPE), F32), sin2, pad], axis=1)
    tab_k = jnp.concatenate([cos2, sin2, jnp.zeros((seq, LANES - 2 * MLA_ROPE), F32)], axis=1)
    r = jnp.arange(LANES)[:, None]
    c = jnp.arange(GROUP_HEADS * LANES)[None, :]
    e_mat = ((r < 2 * MLA_ROPE) & (c % LANES == MLA_NOPE + r % MLA_ROPE)).astype(BF16)
    return tab_q, tab_k, e_mat


def kernel(x, w_in, b_forget, diff_lambda, diff_subln, mla_q_norm, mla_kv_norm, mla_w_uq, mla_w_ukv,
           rel_bias, w_o, ln1_g, ln1_b, w_up, conv_w, conv_b, w_down, ln2_g, ln2_b):
    batch, seq, d = x.shape
    depth = w_in.shape[0]
    alpha = (2 * depth) ** 0.25
    assert seq % ATTN_TILE == 0 and ATTN_TILE % MOBA_BLOCK == 0 and seq // MOBA_BLOCK <= LANES

    (w_qk, w_vt, w_gt, w_lat, wqa, wqb, wk, wvt, w_up_p, conv_p, w_down_p) = _prep_weights(
        w_in, mla_w_uq, mla_w_ukv, w_up, conv_w, conv_b, w_down)
    w_o_b = w_o.astype(BF16)
    tab_q, tab_k, e_mat = _rope_constants(seq)
    bias_d, bias_p = _bias_blocks(rel_bias, BIAS_BLOCK)
    b_col = jnp.pad(b_forget, ((0, 0), (0, 8 - GROUP_HEADS)))[:, :, None]
    gain = jnp.tile(diff_subln, (1, GROUP_HEADS))[:, None, :]

    xf = x.reshape(batch * seq, d)
    xb = xf.astype(BF16)
    for l in range(depth):
        h_bf, v_t, gate_t, q_m, k_m, v_m = _in_proj(
            xb, w_lat[l], w_qk[l], w_vt[l], _ones_rows(3), w_gt[l], mla_q_norm[l][None], mla_kv_norm[l][None],
            wqa[l], wqb[l], wk[l], wvt[l], _ones_rows(1), e_mat, tab_q, tab_k, seq)
        cum = _fox_cumlog(gate_t, b_col[l], batch, seq)
        fox_o = _fox_attention(h_bf, v_t, cum.T, batch, seq, 0, 0)
        lam_init = 0.8 - 0.6 * math.exp(-0.3 * l)
        lam_c = jnp.concatenate([jnp.full((1, LANES), lam_init, F32),
                                 jnp.full((1, LANES), 1.0 - lam_init, F32),
                                 jnp.zeros((6, LANES), F32)], axis=0)
        diff_o = _diff_attention(h_bf, v_t, bias_d, bias_p, diff_lambda[l], lam_c, gain[l], batch, seq, 2, 1)
        moba_o = _moba_attention(h_bf, v_t, bias_d, bias_p, batch, seq, 4, 2)
        mla_o = _mla_attention(q_m, k_m, v_m, batch, seq)
        xf, xb = _out_proj((fox_o, diff_o, moba_o, mla_o), w_o_b[l], xf, ln1_g[l][None], ln1_b[l][None], alpha)
        xf, xb = _ffn(xb, xf, w_up_p[l], conv_p[l], w_down_p[l], ln2_g[l][None], ln2_b[l][None], alpha, seq)
    return xf.reshape(batch, seq, d)
```

```python
import functools
import math

import jax
import jax.numpy as jnp
from jax import lax
from jax.experimental import pallas as pl
from jax.experimental.pallas import tpu as pltpu

F32 = jnp.float32
BF16 = jnp.bfloat16

HEAD_DIM = 64
GROUP_HEADS = 4
GROUP_WIDTH = GROUP_HEADS * HEAD_DIM
LANES = 128
DIFF_QK_DIM = HEAD_DIM // 2
DIFF_SUBLN_EPS = 1e-5
MOBA_BLOCK = 256
MOBA_TOPK = 3
MLA_Q_LORA = 256
MLA_KV_LORA = 128
MLA_NOPE = 64
MLA_ROPE = 32
ROPE_THETA = 10000.0
REL_BUCKETS = 32
REL_MAX_DIST = 128
D_FF = 2752
FF_CHUNK = 256
D_FF_PAD = -(-D_FF // FF_CHUNK) * FF_CHUNK
FFN_LOOKAHEAD = 2
OUT_SUBTILE = 128
CONV_WIDTH = 3
CONV_HALO = 16
LN_EPS = 1e-5
RMS_EPS = 1e-6
NEG = -1e30
LOG2E = 1.4426950408889634

BIAS_BLOCK = 128
ATTN_TILE = 512
V_ROWS = LANES + 16
V_SLAB = (GROUP_HEADS // 2) * V_ROWS
SCORE_LOOKAHEAD = 3
FAR_UNROLL = 2
SCORE_SLOTS = 4
VMEM_LIMIT = 56 * 1024 * 1024

_NT = (((1,), (1,)), ((), ()))


def _params(*sem):
    return pltpu.CompilerParams(dimension_semantics=sem, vmem_limit_bytes=VMEM_LIMIT)


def _cum_kernel(g_ref, b_ref, o_ref):
    x = g_ref[...] + b_ref[...]
    x = jnp.minimum(x, 0.0) - jnp.log(1.0 + jnp.exp(-jnp.abs(x)))
    n = x.shape[1]
    lane = lax.broadcasted_iota(jnp.int32, x.shape, 1)
    k = 1
    while k < n:
        x = x + jnp.where(lane >= k, pltpu.roll(x, k, 1), 0.0)
        k *= 2
    o_ref[...] = x * LOG2E


def _fox_cumlog(logits_t, b_col, batch, seq):
    return pl.pallas_call(
        _cum_kernel,
        grid=(batch,),
        in_specs=[pl.BlockSpec((8, seq), lambda b: (0, b)),
                  pl.BlockSpec((8, 1), lambda b: (0, 0))],
        out_specs=pl.BlockSpec((8, seq), lambda b: (0, b)),
        out_shape=jax.ShapeDtypeStruct(logits_t.shape, F32),
        compiler_params=_params("arbitrary"),
        name="fox_cumlog",
    )(logits_t, b_col)


def _bias_kernel(rb_ref, d_ref, p_ref, *, t):
    h = pl.program_id(0)
    i = lax.broadcasted_iota(jnp.int32, (t, t), 0)
    j = lax.broadcasted_iota(jnp.int32, (t, t), 1)
    max_exact = REL_BUCKETS // 2
    far = rb_ref[(REL_BUCKETS - 1) * 8 + h]
    for out_ref, off in ((d_ref, 0), (p_ref, t)):
        dist = j - i + off
        d_large = jnp.maximum(dist, max_exact).astype(F32)
        large = max_exact + (jnp.log(d_large / max_exact) / math.log(REL_MAX_DIST / max_exact)
                             * (REL_BUCKETS - max_exact)).astype(jnp.int32)
        large = jnp.minimum(large, REL_BUCKETS - 1)
        bucket = jnp.where(dist < max_exact, dist, large)
        acc = jnp.zeros((t, t), F32)
        for bkt in range(REL_BUCKETS - 1):
            acc = jnp.where(bucket == bkt, (rb_ref[bkt * 8 + h] - far) * LOG2E, acc)
        if off == 0:
            acc = jnp.where(dist >= 0, acc, NEG)
        out_ref[0] = acc


def _bias_blocks(rel_bias, t):
    n_heads = rel_bias.shape[1]
    shape = jax.ShapeDtypeStruct((n_heads, t, t), F32)
    spec = pl.BlockSpec((1, t, t), lambda h: (h, 0, 0))
    return pl.pallas_call(
        functools.partial(_bias_kernel, t=t),
        grid=(n_heads,),
        in_specs=[pl.BlockSpec(memory_space=pltpu.SMEM)],
        out_specs=(spec, spec),
        out_shape=(shape, shape),
        compiler_params=_params("arbitrary"),
        name="rel_bias_blocks",
    )(rel_bias.reshape(-1))


def _online_update(s_t, tile_max, v_t, carry):
    m, acc = carry
    m_new = jnp.maximum(m, tile_max)
    p = jnp.exp2((s_t - m_new).astype(BF16))
    acc = jnp.exp2(m - m_new) * acc + jnp.dot(v_t, p, preferred_element_type=F32)
    return m_new, acc


def _normalized(acc):
    return acc[:LANES] / acc[LANES:LANES + 1]


def _sweep(n, t, qi, raw, fix, values, near_bias, s_ref):
    la = SCORE_LOOKAHEAD
    slots = s_ref.shape[0]
    assert n % slots == 0 and la < slots

    def produce(c, j):
        s_t = raw(c, j)
        s_ref[c % slots] = s_t
        return jnp.max(s_t, axis=0, keepdims=True)

    def step(j, state, kind):
        carries, tops = state
        tops = dict(enumerate(tops))
        nxt = []
        out = []
        for c in range(n):
            if c + la < n:
                tops[c + la] = produce(c + la, j)
            elif kind != "diag":
                nxt.append(produce(c + la - n, j + 1))
            s_t, top = s_ref[c % slots], tops.pop(c)
            if kind != "far":
                s_t = fix(c, s_t, kind)
                top = jnp.max(s_t, axis=0, keepdims=True)
            out.append(_online_update(s_t, top, values(c, j), carries[c]))
        return tuple(out), (tuple(nxt) if nxt else state[1])

    def far_steps(count, state):
        def group(i, s):
            for u in range(FAR_UNROLL):
                s = step(FAR_UNROLL * i + u, s, "far")
            return s

        assert FAR_UNROLL & (FAR_UNROLL - 1) == 0
        groups = lax.shift_right_logical(count, FAR_UNROLL.bit_length() - 1)
        state = lax.fori_loop(0, groups, group, state)
        return lax.fori_loop(groups * FAR_UNROLL, count, lambda j, s: step(j, s, "far"), state)

    state = (tuple(_init_carry(t) for _ in range(n)), tuple(produce(c, 0) for c in range(la)))
    if near_bias:
        state = far_steps(jnp.maximum(qi - 1, 0), state)
        state = lax.cond(qi >= 1, lambda s: step(qi - 1, s, "near"), lambda s: s, state)
    else:
        state = far_steps(qi, state)
    return step(qi, state, "diag")[0]


def _init_carry(t):
    return (jnp.full((1, t), NEG, F32), jnp.zeros((V_ROWS, t), F32))


def _lane_select(x, lo, hi):
    lane = lax.broadcasted_iota(jnp.int32, x.shape, 1)
    return jnp.where((lane >= lo) & (lane < hi), x, jnp.zeros_like(x))


def _merge_pair(a, b):
    row = lax.broadcasted_iota(jnp.int32, a.shape, 0)
    return jnp.where(row < HEAD_DIM, a, b)


def _causal_t(t):
    return (lax.broadcasted_iota(jnp.int32, (t, t), 0) <= lax.broadcasted_iota(jnp.int32, (t, t), 1))


def _band_bias(s_t, d_t, p_t, diag):
    n = s_t.shape[0] // BIAS_BLOCK
    rows = []
    for b in range(n):
        blocks = []
        for a in range(n):
            blk = s_t[b * BIAS_BLOCK:(b + 1) * BIAS_BLOCK, a * BIAS_BLOCK:(a + 1) * BIAS_BLOCK]
            if diag and b > a:
                blk = jnp.full_like(blk, NEG)
            elif diag and b == a:
                blk = blk + d_t
            elif (diag and b == a - 1) or (not diag and b == n - 1 and a == 0):
                blk = blk + p_t
            blocks.append(blk)
        rows.append(jnp.concatenate(blocks, axis=1))
    return jnp.concatenate(rows, axis=0)


def _attn_specs(t, seq, nq, cb, vrow):
    w = GROUP_WIDTH
    return [pl.BlockSpec((t, w), lambda b, i: (b * nq + i, cb)),
            pl.BlockSpec((seq, w), lambda b, i: (b, cb + 1)),
            pl.BlockSpec((V_SLAB, seq), lambda b, i: (vrow, b))]


def _attn_call(kernel_fn, name, batch, seq, t, in_specs, args, scratch=()):
    nq = seq // t
    return pl.pallas_call(
        kernel_fn,
        grid=(batch, nq),
        in_specs=in_specs,
        out_specs=pl.BlockSpec((t, GROUP_WIDTH), lambda b, i: (b * nq + i, 0)),
        out_shape=jax.ShapeDtypeStruct((batch * seq, GROUP_WIDTH), BF16),
        scratch_shapes=[pltpu.VMEM((SCORE_SLOTS, t, t), F32)] + list(scratch),
        compiler_params=_params("arbitrary", "arbitrary"),
        name=name,
    )(*args)


def _head_queries(q_ref, width):
    out = []
    for lo in range(0, GROUP_WIDTH, width):
        p = lo // LANES
        out.append(_lane_select(q_ref[:, p * LANES:(p + 1) * LANES], lo - p * LANES, lo - p * LANES + width))
    return out


def _fox_kernel(q_ref, k_ref, vt_ref, c_ref, o_ref, s_ref, *, t):
    qi = pl.program_id(1)
    causal = _causal_t(t)
    qms = _head_queries(q_ref, HEAD_DIM)

    def raw(h, j):
        sl = slice((h // 2) * LANES, (h // 2 + 1) * LANES)
        start = pl.multiple_of(j * t, t)
        s_t = lax.dot_general(k_ref[pl.ds(start, t), sl], qms[h], _NT, preferred_element_type=F32)
        return s_t - c_ref[pl.ds(start, t), h:h + 1]

    def fix(h, s_t, kind):
        return jnp.where(causal, s_t, NEG)

    def values(h, j):
        return vt_ref[(h // 2) * V_ROWS:(h // 2 + 1) * V_ROWS, pl.ds(pl.multiple_of(j * t, t), t)]

    carries = _sweep(GROUP_HEADS, t, qi, raw, fix, values, False, s_ref)
    outs = [_normalized(acc) for (_, acc) in carries]
    for p in range(GROUP_HEADS // 2):
        o_t = _merge_pair(outs[2 * p], outs[2 * p + 1])
        o_ref[:, p * LANES:(p + 1) * LANES] = o_t.T.astype(o_ref.dtype)


def _fox_attention(h_bf, v_t, cum_cols, batch, seq, cb, vrow, t=ATTN_TILE):
    nq = seq // t
    specs = _attn_specs(t, seq, nq, cb, vrow) + [pl.BlockSpec((seq, 8), lambda b, i: (b, 0))]
    return _attn_call(functools.partial(_fox_kernel, t=t), "fox_attn", batch, seq, t, specs,
                      (h_bf, h_bf, v_t, cum_cols))


def _diff_kernel(q_ref, k_ref, vt_ref, bd_ref, bp_ref, lam_ref, lc_ref, g_ref, o_ref, s_ref, *, t):
    qi = pl.program_id(1)
    lam_p = lam_ref[...]
    lam = (jnp.exp(jnp.sum(lam_p[0:1] * lam_p[1:2], axis=-1, keepdims=True))
           - jnp.exp(jnp.sum(lam_p[2:3] * lam_p[3:4], axis=-1, keepdims=True)) + lc_ref[0:1, 0:1])
    qms = _head_queries(q_ref, DIFF_QK_DIM)

    def raw(c, j):
        sl = slice((c // 4) * LANES, (c // 4 + 1) * LANES)
        return lax.dot_general(k_ref[pl.ds(pl.multiple_of(j * t, t), t), sl], qms[c], _NT,
                               preferred_element_type=F32)

    def fix(c, s_t, kind):
        return _band_bias(s_t, bd_ref[c // 2], bp_ref[c // 2], kind == "diag")

    def values(c, j):
        return vt_ref[(c // 4) * V_ROWS:(c // 4 + 1) * V_ROWS, pl.ds(pl.multiple_of(j * t, t), t)]

    carries = _sweep(2 * GROUP_HEADS, t, qi, raw, fix, values, True, s_ref)
    row = lax.broadcasted_iota(jnp.int32, (LANES, t), 0)
    outs = []
    for h in range(GROUP_HEADS):
        o = _normalized(carries[2 * h][1]) - lam * _normalized(carries[2 * h + 1][1])
        lo = (h % 2) * HEAD_DIM
        o = jnp.where((row >= lo) & (row < lo + HEAD_DIM), o, 0.0)
        ms = jnp.sum(o * o, axis=0, keepdims=True) * (1.0 / HEAD_DIM)
        outs.append(o * lax.rsqrt(ms + DIFF_SUBLN_EPS))
    for p in range(GROUP_HEADS // 2):
        sl = slice(p * LANES, (p + 1) * LANES)
        o_t = _merge_pair(outs[2 * p], outs[2 * p + 1])
        o_ref[:, sl] = (o_t.T * g_ref[:, sl] * lc_ref[1:2, 0:1]).astype(o_ref.dtype)


def _diff_attention(h_bf, v_t, bias_d, bias_p, lam_p, lam_c, gain, batch, seq, cb, vrow, t=ATTN_TILE):
    nq = seq // t
    bias_spec = pl.BlockSpec((GROUP_HEADS, BIAS_BLOCK, BIAS_BLOCK), lambda b, i: (0, 0, 0))
    full = lambda a: pl.BlockSpec(a.shape, lambda b, i: (0, 0))
    specs = _attn_specs(t, seq, nq, cb, vrow) + [bias_spec, bias_spec, full(lam_p), full(lam_c), full(gain)]
    return _attn_call(functools.partial(_diff_kernel, t=t), "diff_attn", batch, seq, t, specs,
                      (h_bf, h_bf, v_t, bias_d, bias_p, lam_p, lam_c, gain))


def _moba_kernel(q_ref, k_ref, vt_ref, bd_ref, bp_ref, o_ref, s_ref, kmean_ref, *, t, n_blocks):
    qi = pl.program_id(1)
    per_tile = t // MOBA_BLOCK

    @pl.when(qi == 0)
    def _():
        kmean_ref[...] = jnp.zeros_like(kmean_ref)
        for blk in range(n_blocks):
            kb = k_ref[blk * MOBA_BLOCK:(blk + 1) * MOBA_BLOCK, :].astype(F32)
            kmean_ref[blk:blk + 1, :] = jnp.sum(kb, axis=0, keepdims=True) * (1.0 / MOBA_BLOCK)

    def tile_block(pos):
        return sum((pos >= n * MOBA_BLOCK).astype(jnp.int32) for n in range(1, per_tile))

    nb = -(-n_blocks // 8) * 8
    blk_t = lax.broadcasted_iota(jnp.int32, (nb, t), 0)
    blk_tf = blk_t.astype(F32)
    own_t = qi * per_tile + tile_block(lax.broadcasted_iota(jnp.int32, (nb, t), 1))
    qms = _head_queries(q_ref, HEAD_DIM)
    q_cats = []
    for h in range(GROUP_HEADS):
        km = kmean_ref[:, (h // 2) * LANES:(h // 2 + 1) * LANES]
        km_hi = km.astype(BF16)
        km_lo = (km - km_hi.astype(F32)).astype(BF16)
        gate = (lax.dot_general(km_hi, qms[h], _NT, preferred_element_type=F32)
                + lax.dot_general(km_lo, qms[h], _NT, preferred_element_type=F32))[:nb]
        g = jnp.where(blk_t < own_t, gate, NEG)
        keep = jnp.where(blk_t == own_t, 1.0, 0.0)
        for _ in range(MOBA_TOPK):
            mx = jnp.max(g, axis=0, keepdims=True)
            first = jnp.min(jnp.where(g == mx, blk_tf, 1e9), axis=0, keepdims=True)
            pick = jnp.where(mx > 0.5 * NEG, jnp.where(blk_tf == first, 1.0, 0.0), 0.0)
            keep = jnp.maximum(keep, pick)
            g = jnp.where(pick > 0.0, NEG, g)
        drop = jnp.concatenate([1.0 - keep, jnp.zeros((LANES - nb, t), F32)], axis=0)
        q_cats.append(jnp.concatenate([qms[h], drop.T.astype(BF16)], axis=1))

    blk_i = lax.broadcasted_iota(jnp.int32, (t, LANES), 1)
    key_blk = blk_i - tile_block(lax.broadcasted_iota(jnp.int32, (t, LANES), 0))

    def raw(h, j):
        sl = slice((h // 2) * LANES, (h // 2 + 1) * LANES)
        reject = jnp.where(key_blk == j * per_tile, NEG, 0.0).astype(BF16)
        k_cat = jnp.concatenate([k_ref[pl.ds(pl.multiple_of(j * t, t), t), sl], reject], axis=1)
        return lax.dot_general(k_cat, q_cats[h], _NT, preferred_element_type=F32)

    def fix(h, s_t, kind):
        return _band_bias(s_t, bd_ref[h], bp_ref[h], kind == "diag")

    def values(h, j):
        return vt_ref[(h // 2) * V_ROWS:(h // 2 + 1) * V_ROWS, pl.ds(pl.multiple_of(j * t, t), t)]

    carries = _sweep(GROUP_HEADS, t, qi, raw, fix, values, True, s_ref)
    outs = [_normalized(acc) for (_, acc) in carries]
    for p in range(GROUP_HEADS // 2):
        o_t = _merge_pair(outs[2 * p], outs[2 * p + 1])
        o_ref[:, p * LANES:(p + 1) * LANES] = o_t.T.astype(o_ref.dtype)


def _moba_attention(h_bf, v_t, bias_d, bias_p, batch, seq, cb, vrow, t=ATTN_TILE):
    nq = seq // t
    bias_spec = pl.BlockSpec((GROUP_HEADS, BIAS_BLOCK, BIAS_BLOCK), lambda b, i: (1, 0, 0))
    specs = _attn_specs(t, seq, nq, cb, vrow) + [bias_spec, bias_spec]
    return _attn_call(functools.partial(_moba_kernel, t=t, n_blocks=seq // MOBA_BLOCK), "moba_attn",
                      batch, seq, t, specs, (h_bf, h_bf, v_t, bias_d, bias_p),
                      scratch=[pltpu.VMEM((LANES, GROUP_WIDTH), F32)])


def _in_proj_kernel(x_ref, wlat_ref, wqk_ref, wvt3_ref, ones3_ref, wgt_ref,
                    gq_ref, gkv_ref, wqa_ref, wqb_ref, wk_ref, wvt_ref, ones_ref, e_ref, tq_ref, tk_ref,
                    h_ref, vt3_ref, gate_ref, q_ref, k_ref, vt_ref):
    x = x_ref[...]
    c = jnp.dot(x, wlat_ref[...], preferred_element_type=F32)
    h_ref[...] = jnp.dot(x, wqk_ref[...], preferred_element_type=F32).astype(h_ref.dtype)
    v3 = lax.dot_general(wvt3_ref[...], x, _NT, preferred_element_type=F32) + ones3_ref[...]
    vt3_ref[...] = v3.astype(vt3_ref.dtype)
    gate_ref[...] = lax.dot_general(wgt_ref[...], x, _NT, preferred_element_type=F32)
    cq = c[:, :MLA_Q_LORA]
    ckv = c[:, MLA_Q_LORA:MLA_Q_LORA + MLA_KV_LORA]
    kr = c[:, MLA_Q_LORA + MLA_KV_LORA:]
    cq = cq * lax.rsqrt(jnp.mean(cq * cq, axis=-1, keepdims=True) + RMS_EPS) * gq_ref[...]
    ckv = ckv * lax.rsqrt(jnp.mean(ckv * ckv, axis=-1, keepdims=True) + RMS_EPS) * gkv_ref[...]
    cq = cq.astype(BF16)
    ckv = ckv.astype(BF16)
    tq = tq_ref[...]
    cos_q = jnp.concatenate([tq[:, :LANES]] * GROUP_HEADS, axis=1)
    sin_q = jnp.concatenate([tq[:, LANES:]] * GROUP_HEADS, axis=1)
    q = (jnp.dot(cq, wqa_ref[...], preferred_element_type=F32) * cos_q
         + jnp.dot(cq, wqb_ref[...], preferred_element_type=F32) * sin_q)
    q_ref[...] = (q * ((MLA_NOPE + MLA_ROPE) ** -0.5 * LOG2E)).astype(q_ref.dtype)
    k_rope = (kr * tk_ref[...]).astype(BF16)
    k = (jnp.dot(ckv, wk_ref[...], preferred_element_type=F32)
         + jnp.dot(k_rope, e_ref[...], preferred_element_type=F32))
    k_ref[...] = k.astype(k_ref.dtype)
    v_t = lax.dot_general(wvt_ref[...], ckv, _NT, preferred_element_type=F32) + ones_ref[...]
    vt_ref[...] = v_t.astype(vt_ref.dtype)


def _resident(a, layer):
    if a.ndim == 2:
        return pl.BlockSpec(a.shape, lambda *_: (0, 0), pipeline_mode=pl.Buffered(1))
    return pl.BlockSpec((None,) + a.shape[1:], lambda *_: (layer, 0, 0), pipeline_mode=pl.Buffered(1))


def _in_proj(layer, xb, w_lat, w_qk, w_vt3, ones3, w_gt, gq, gkv, wqa, wqb, wk, wvt, ones, e_mat, tab_q, tab_k,
             seq, tm=512):
    m, d = xb.shape
    ns = seq // tm
    rows = lambda w: pl.BlockSpec((tm, w), lambda i: (i, 0))
    cols = lambda r: pl.BlockSpec((r, tm), lambda i: (0, i))
    slot_w = GROUP_HEADS * LANES
    consts = (w_lat, w_qk, w_vt3, ones3, w_gt, gq, gkv, wqa, wqb, wk, wvt, ones, e_mat)
    return pl.pallas_call(
        _in_proj_kernel,
        grid=(m // tm,),
        in_specs=[rows(d)] + [_resident(a, layer) for a in consts]
                 + [pl.BlockSpec((tm, tab_q.shape[1]), lambda i: (i % ns, 0)),
                    pl.BlockSpec((tm, tab_k.shape[1]), lambda i: (i % ns, 0))],
        out_specs=(rows(w_qk.shape[-1]), cols(w_vt3.shape[-2]), cols(w_gt.shape[-2]),
                   rows(slot_w), rows(slot_w), cols(V_SLAB)),
        out_shape=(jax.ShapeDtypeStruct((m, w_qk.shape[-1]), BF16),
                   jax.ShapeDtypeStruct((w_vt3.shape[-2], m), BF16),
                   jax.ShapeDtypeStruct((w_gt.shape[-2], m), F32),
                   jax.ShapeDtypeStruct((m, slot_w), BF16),
                   jax.ShapeDtypeStruct((m, slot_w), BF16),
                   jax.ShapeDtypeStruct((V_SLAB, m), BF16)),
        compiler_params=_params("arbitrary"),
        name="in_proj",
    )(xb, *consts, tab_q, tab_k)


def _mla_kernel(q_ref, k_ref, vt_ref, o_ref, s_ref, *, t):
    qi = pl.program_id(1)
    causal = _causal_t(t)

    def raw(h, j):
        hsl = slice(h * LANES, (h + 1) * LANES)
        return lax.dot_general(k_ref[pl.ds(pl.multiple_of(j * t, t), t), hsl], q_ref[:, hsl], _NT,
                               preferred_element_type=F32)

    def fix(h, s_t, kind):
        return jnp.where(causal, s_t, NEG)

    def values(h, j):
        return vt_ref[(h // 2) * V_ROWS:(h // 2 + 1) * V_ROWS, pl.ds(pl.multiple_of(j * t, t), t)]

    carries = _sweep(GROUP_HEADS, t, qi, raw, fix, values, False, s_ref)
    outs = [_normalized(acc) for (_, acc) in carries]
    for p in range(GROUP_HEADS // 2):
        o_t = _merge_pair(outs[2 * p], outs[2 * p + 1])
        o_ref[:, p * LANES:(p + 1) * LANES] = o_t.T.astype(o_ref.dtype)


def _mla_attention(q, k, v_t, batch, seq, t=ATTN_TILE):
    nq = seq // t
    slot_w = q.shape[1]
    specs = [pl.BlockSpec((t, slot_w), lambda b, i: (b * nq + i, 0)),
             pl.BlockSpec((seq, slot_w), lambda b, i: (b, 0)),
             pl.BlockSpec((V_SLAB, seq), lambda b, i: (0, b))]
    return _attn_call(functools.partial(_mla_kernel, t=t), "mla_attn", batch, seq, t, specs, (q, k, v_t))


def _layernorm(t, g, b):
    mu = jnp.mean(t, axis=-1, keepdims=True)
    d = t - mu
    var = jnp.mean(d * d, axis=-1, keepdims=True)
    return d * lax.rsqrt(var + LN_EPS) * g + b


def _out_kernel(o1_ref, o2_ref, o3_ref, o4_ref, w_ref, x_ref, g_ref, b_ref, of_ref, ob_ref, *, alpha):
    sub = OUT_SUBTILE
    n_sub = x_ref.shape[0] // sub

    def project(r):
        rows = slice(r * sub, (r + 1) * sub)
        o = jnp.concatenate([o_ref[rows, :] for o_ref in (o1_ref, o2_ref, o3_ref, o4_ref)], axis=1)
        return jnp.dot(o, w_ref[...], preferred_element_type=F32)

    mix_next = project(0)
    for r in range(n_sub):
        mix = mix_next
        if r + 1 < n_sub:
            mix_next = project(r + 1)
        rows = slice(r * sub, (r + 1) * sub)
        y = _layernorm(alpha * x_ref[rows, :] + mix, g_ref[...], b_ref[...])
        of_ref[rows, :] = y
        ob_ref[rows, :] = y.astype(ob_ref.dtype)


def _out_proj(layer, outs, w_o, x, g, b, alpha, tm=512):
    m, d = x.shape
    o_spec = pl.BlockSpec((tm, GROUP_WIDTH), lambda i: (i, 0))
    row_spec = pl.BlockSpec((tm, d), lambda i: (i, 0))
    vec_spec = pl.BlockSpec((1, d), lambda i: (0, 0))
    return pl.pallas_call(
        functools.partial(_out_kernel, alpha=alpha),
        grid=(m // tm,),
        in_specs=[o_spec, o_spec, o_spec, o_spec,
                  _resident(w_o, layer), row_spec, vec_spec, vec_spec],
        out_specs=(row_spec, row_spec),
        out_shape=(jax.ShapeDtypeStruct((m, d), F32), jax.ShapeDtypeStruct((m, d), BF16)),
        compiler_params=_params("arbitrary"),
        name="out_proj_ln",
    )(*outs, w_o, x, g, b)


def _ffn_kernel(xb_ref, xh_ref, xf_ref, wup_ref, cp_ref, wd_ref, g_ref, b_ref,
                of_ref, ob_ref, *, alpha, tiles_per_seq, n_chunks):
    i = pl.program_id(0)
    halo = xh_ref[...]
    halo = jnp.where(i % tiles_per_seq == 0, jnp.zeros_like(halo), halo)
    xe = jnp.concatenate([halo, xb_ref[...]], axis=0)

    def cols(c, br):
        return slice(br * D_FF_PAD + c * FF_CHUNK, br * D_FF_PAD + (c + 1) * FF_CHUNK)

    def project(c):
        return [jnp.dot(xe, wup_ref[:, cols(c, br)], preferred_element_type=F32) for br in range(2)]

    def gated(c, us):
        ys = []
        for br, u in enumerate(us):
            cp = cp_ref[:, cols(c, br)]
            y = cp[2:3] * u + cp[1:2] * pltpu.roll(u, 1, 0) + cp[0:1] * pltpu.roll(u, 2, 0) + cp[3:4]
            ys.append(y[CONV_HALO:])
        return (ys[0] * jax.nn.sigmoid(ys[0]) * ys[1]).astype(BF16)

    acc = None
    ahead = [project(c) for c in range(FFN_LOOKAHEAD)]
    for c in range(n_chunks):
        if c + FFN_LOOKAHEAD < n_chunks:
            ahead.append(project(c + FFN_LOOKAHEAD))
        act = gated(c, ahead.pop(0))
        part = jnp.dot(act, wd_ref[c * FF_CHUNK:(c + 1) * FF_CHUNK, :], preferred_element_type=F32)
        acc = part if acc is None else acc + part
    y = _layernorm(alpha * xf_ref[...] + acc, g_ref[...], b_ref[...])
    of_ref[...] = y
    ob_ref[...] = y.astype(ob_ref.dtype)


def _ffn(layer, xb, xf, w_up, conv_p, w_down, g, b, alpha, seq, tm=256):
    m, d = xf.shape
    n_chunks = D_FF_PAD // FF_CHUNK
    halo_blocks = tm // CONV_HALO
    row_spec = pl.BlockSpec((tm, d), lambda i: (i, 0))
    vec_spec = pl.BlockSpec((1, d), lambda i: (0, 0))
    resident = lambda a: _resident(a, layer)
    return pl.pallas_call(
        functools.partial(_ffn_kernel, alpha=alpha, tiles_per_seq=seq // tm, n_chunks=n_chunks),
        grid=(m // tm,),
        in_specs=[row_spec,
                  pl.BlockSpec((CONV_HALO, d), lambda i: (jnp.maximum(i * halo_blocks - 1, 0), 0)),
                  row_spec, resident(w_up), resident(conv_p), resident(w_down), vec_spec, vec_spec],
        out_specs=(row_spec, row_spec),
        out_shape=(jax.ShapeDtypeStruct((m, d), F32), jax.ShapeDtypeStruct((m, d), BF16)),
        compiler_params=_params("arbitrary"),
        name="conv_ffn_ln",
    )(xb, xb, xf, w_up, conv_p, w_down, g, b)


def _swap_halves(w):
    half = w.shape[-1] // 2
    return jnp.concatenate([-w[..., half:], w[..., :half]], axis=-1)


def _value_slab(w_v):
    depth, k, _ = w_v.shape
    w = w_v.reshape(depth, k, GROUP_HEADS // 2, LANES)
    w = jnp.pad(w, ((0, 0), (0, 0), (0, 0), (0, V_ROWS - LANES)))
    return jnp.swapaxes(w.reshape(depth, k, V_SLAB), 1, 2)


def _ones_rows(n_slabs):
    row = jnp.arange(n_slabs * V_SLAB) % V_ROWS
    return (row == LANES).astype(F32)[:, None]


def _prep_weights(w_in, mla_w_uq, mla_w_ukv, w_up, conv_w, conv_b, w_down):
    depth, d, _ = w_in.shape
    gw = GROUP_WIDTH
    fox0 = 0
    diff0 = 3 * gw + GROUP_HEADS
    moba0 = diff0 + 3 * gw
    mla0 = moba0 + 3 * gw
    sm = HEAD_DIM ** -0.5 * LOG2E
    sd = DIFF_QK_DIM ** -0.5 * LOG2E
    w_qk = jnp.concatenate([
        w_in[:, :, fox0:fox0 + gw] * sm, w_in[:, :, fox0 + gw:fox0 + 2 * gw],
        w_in[:, :, diff0:diff0 + gw] * sd, w_in[:, :, diff0 + gw:diff0 + 2 * gw],
        w_in[:, :, moba0:moba0 + gw] * sm, w_in[:, :, moba0 + gw:moba0 + 2 * gw]], axis=-1).astype(BF16)
    w_vt = jnp.concatenate([_value_slab(w_in[:, :, g0 + 2 * gw:g0 + 3 * gw])
                            for g0 in (fox0, diff0, moba0)], axis=1).astype(BF16)
    w_gt = jnp.swapaxes(jnp.pad(w_in[:, :, 3 * gw:3 * gw + GROUP_HEADS],
                                ((0, 0), (0, 0), (0, 8 - GROUP_HEADS))), 1, 2).astype(BF16)
    kr0 = mla0 + MLA_Q_LORA + MLA_KV_LORA
    w_kr = w_in[:, :, kr0:kr0 + MLA_ROPE]
    w_lat = jnp.concatenate([w_in[:, :, mla0:kr0], w_kr, _swap_halves(w_kr),
                             jnp.zeros((depth, d, LANES - 2 * MLA_ROPE), F32)], axis=-1).astype(BF16)

    uq = mla_w_uq.reshape(depth, MLA_Q_LORA, GROUP_HEADS, MLA_NOPE + MLA_ROPE)
    pad = LANES - MLA_NOPE - MLA_ROPE
    wqa = jnp.pad(uq, ((0, 0), (0, 0), (0, 0), (0, pad)))
    wqb = jnp.concatenate([jnp.zeros_like(uq[..., :MLA_NOPE]), _swap_halves(uq[..., MLA_NOPE:]),
                           jnp.zeros(uq.shape[:-1] + (pad,), F32)], axis=-1)
    ukv = mla_w_ukv.reshape(depth, MLA_KV_LORA, GROUP_HEADS, MLA_NOPE + HEAD_DIM)
    wk = jnp.pad(ukv[..., :MLA_NOPE], ((0, 0), (0, 0), (0, 0), (0, LANES - MLA_NOPE)))
    flat = lambda a: a.reshape(depth, a.shape[1], -1).astype(BF16)
    wvt = _value_slab(flat(ukv[..., MLA_NOPE:])).astype(BF16)

    ffp = D_FF_PAD - D_FF
    padc = lambda a: jnp.pad(a, ((0, 0), (0, 0), (0, ffp)))
    w_up_p = jnp.concatenate([padc(w_up[:, :, :D_FF]), padc(w_up[:, :, D_FF:])], axis=-1).astype(BF16)
    conv = jnp.concatenate([conv_w, conv_b[:, None, :],
                            jnp.zeros((depth, 8 - CONV_WIDTH - 1, 2 * D_FF), F32)], axis=1)
    conv_p = jnp.concatenate([padc(conv[:, :, :D_FF]), padc(conv[:, :, D_FF:])], axis=-1)
    w_down_p = jnp.pad(w_down, ((0, 0), (0, ffp), (0, 0))).astype(BF16)
    return w_qk, w_vt, w_gt, w_lat, flat(wqa), flat(wqb), flat(wk), wvt, w_up_p, conv_p, w_down_p


def _rope_constants(seq):
    inv = ROPE_THETA ** (-jnp.arange(0, MLA_ROPE // 2, dtype=F32) * 2.0 / MLA_ROPE)
    ang = jnp.arange(seq, dtype=F32)[:, None] * inv[None, :]
    cos, sin = jnp.cos(ang), jnp.sin(ang)
    cos2 = jnp.concatenate([cos, cos], axis=1)
    sin2 = jnp.concatenate([sin, sin], axis=1)
    pad = jnp.zeros((seq, LANES - MLA_NOPE - MLA_ROPE), F32)
    tab_q = jnp.concatenate([jnp.ones((seq, MLA_NOPE), F32), cos2, pad,
                             jnp.zeros((seq, MLA_NOPE), F32), sin2, pad], axis=1)
    tab_k = jnp.concatenate([cos2, sin2, jnp.zeros((seq, LANES - 2 * MLA_ROPE), F32)], axis=1)
    r = jnp.arange(LANES)[:, None]
    c = jnp.arange(GROUP_HEADS * LANES)[None, :]
    e_mat = ((r < 2 * MLA_ROPE) & (c % LANES == MLA_NOPE + r % MLA_ROPE)).astype(BF16)
    return tab_q, tab_k, e_mat


def kernel(x, w_in, b_forget, diff_lambda, diff_subln, mla_q_norm, mla_kv_norm, mla_w_uq, mla_w_ukv,
           rel_bias, w_o, ln1_g, ln1_b, w_up, conv_w, conv_b, w_down, ln2_g, ln2_b):
    batch, seq, d = x.shape
    depth = w_in.shape[0]
    alpha = (2 * depth) ** 0.25
    assert seq % ATTN_TILE == 0 and ATTN_TILE % MOBA_BLOCK == 0 and seq // MOBA_BLOCK <= LANES

    (w_qk, w_vt, w_gt, w_lat, wqa, wqb, wk, wvt, w_up_p, conv_p, w_down_p) = _prep_weights(
        w_in, mla_w_uq, mla_w_ukv, w_up, conv_w, conv_b, w_down)
    w_o_b = w_o.astype(BF16)
    tab_q, tab_k, e_mat = _rope_constants(seq)
    bias_d, bias_p = _bias_blocks(rel_bias, BIAS_BLOCK)
    b_col = jnp.pad(b_forget, ((0, 0), (0, 8 - GROUP_HEADS)))[:, :, None]
    gain = jnp.tile(diff_subln, (1, GROUP_HEADS))[:, None, :]

    xf = x.reshape(batch * seq, d)
    xb = xf.astype(BF16)
    for l in range(depth):
        h_bf, v_t, gate_t, q_m, k_m, v_m = _in_proj(
            l, xb, w_lat, w_qk, w_vt, _ones_rows(3), w_gt, mla_q_norm[l][None], mla_kv_norm[l][None],
            wqa, wqb, wk, wvt, _ones_rows(1), e_mat, tab_q, tab_k, seq)
        cum = _fox_cumlog(gate_t, b_col[l], batch, seq)
        fox_o = _fox_attention(h_bf, v_t, cum.T, batch, seq, 0, 0)
        lam_init = 0.8 - 0.6 * math.exp(-0.3 * l)
        lam_c = jnp.concatenate([jnp.full((1, LANES), lam_init, F32),
                                 jnp.full((1, LANES), 1.0 - lam_init, F32),
                                 jnp.zeros((6, LANES), F32)], axis=0)
        diff_o = _diff_attention(h_bf, v_t, bias_d, bias_p, diff_lambda[l], lam_c, gain[l], batch, seq, 2, 1)
        moba_o = _moba_attention(h_bf, v_t, bias_d, bias_p, batch, seq, 4, 2)
        mla_o = _mla_attention(q_m, k_m, v_m, batch, seq)
        xf, xb = _out_proj(l, (fox_o, diff_o, moba_o, mla_o), w_o_b, xf, ln1_g[l][None], ln1_b[l][None], alpha)
        xf, xb = _ffn(l, xb, xf, w_up_p, conv_p, w_down_p, ln2_g[l][None], ln2_b[l][None], alpha, seq)
    return xf.reshape(batch, seq, d)
```

```python
import functools
import math

import jax
import jax.numpy as jnp
from jax import lax
from jax.experimental import pallas as pl
from jax.experimental.pallas import tpu as pltpu

F32 = jnp.float32
BF16 = jnp.bfloat16

HEAD_DIM = 64
GROUP_HEADS = 4
GROUP_WIDTH = GROUP_HEADS * HEAD_DIM
LANES = 128
DIFF_QK_DIM = HEAD_DIM // 2
DIFF_SUBLN_EPS = 1e-5
MOBA_BLOCK = 256
MOBA_TOPK = 3
MLA_Q_LORA = 256
MLA_KV_LORA = 128
MLA_NOPE = 64
MLA_ROPE = 32
ROPE_THETA = 10000.0
REL_BUCKETS = 32
REL_MAX_DIST = 128
D_FF = 2752
FF_CHUNK = 256
D_FF_PAD = -(-D_FF // FF_CHUNK) * FF_CHUNK
FFN_LOOKAHEAD = 2
OUT_SUBTILE = 128
CONV_WIDTH = 3
CONV_HALO = 16
LN_EPS = 1e-5
RMS_EPS = 1e-6
NEG = -1e30
LOG2E = 1.4426950408889634

BIAS_BLOCK = 128
ATTN_TILE = 512
V_ROWS = LANES + 16
V_SLAB = (GROUP_HEADS // 2) * V_ROWS
SCORE_LOOKAHEAD = 3
FAR_UNROLL = 4
SCORE_SLOTS = 4
VMEM_LIMIT = 56 * 1024 * 1024

_NT = (((1,), (1,)), ((), ()))
_TN = (((0,), (1,)), ((), ()))


def _params(*sem):
    return pltpu.CompilerParams(dimension_semantics=sem, vmem_limit_bytes=VMEM_LIMIT)


def _cum_kernel(g_ref, b_ref, o_ref):
    x = g_ref[...] + b_ref[...]
    x = jnp.minimum(x, 0.0) - jnp.log(1.0 + jnp.exp(-jnp.abs(x)))
    n = x.shape[1]
    lane = lax.broadcasted_iota(jnp.int32, x.shape, 1)
    k = 1
    while k < n:
        x = x + jnp.where(lane >= k, pltpu.roll(x, k, 1), 0.0)
        k *= 2
    o_ref[...] = x * LOG2E


def _fox_cumlog(logits_t, b_col, batch, seq):
    return pl.pallas_call(
        _cum_kernel,
        grid=(batch,),
        in_specs=[pl.BlockSpec((8, seq), lambda b: (0, b)),
                  pl.BlockSpec((8, 1), lambda b: (0, 0))],
        out_specs=pl.BlockSpec((8, seq), lambda b: (0, b)),
        out_shape=jax.ShapeDtypeStruct(logits_t.shape, F32),
        compiler_params=_params("arbitrary"),
        name="fox_cumlog",
    )(logits_t, b_col)


def _bias_kernel(rb_ref, d_ref, p_ref, *, t):
    h = pl.program_id(0)
    i = lax.broadcasted_iota(jnp.int32, (t, t), 0)
    j = lax.broadcasted_iota(jnp.int32, (t, t), 1)
    max_exact = REL_BUCKETS // 2
    far = rb_ref[(REL_BUCKETS - 1) * 8 + h]
    for out_ref, off in ((d_ref, 0), (p_ref, t)):
        dist = j - i + off
        d_large = jnp.maximum(dist, max_exact).astype(F32)
        large = max_exact + (jnp.log(d_large / max_exact) / math.log(REL_MAX_DIST / max_exact)
                             * (REL_BUCKETS - max_exact)).astype(jnp.int32)
        large = jnp.minimum(large, REL_BUCKETS - 1)
        bucket = jnp.where(dist < max_exact, dist, large)
        acc = jnp.zeros((t, t), F32)
        for bkt in range(REL_BUCKETS - 1):
            acc = jnp.where(bucket == bkt, (rb_ref[bkt * 8 + h] - far) * LOG2E, acc)
        if off == 0:
            acc = jnp.where(dist >= 0, acc, NEG)
        out_ref[0] = acc


def _bias_blocks(rel_bias, t):
    n_heads = rel_bias.shape[1]
    shape = jax.ShapeDtypeStruct((n_heads, t, t), F32)
    spec = pl.BlockSpec((1, t, t), lambda h: (h, 0, 0))
    return pl.pallas_call(
        functools.partial(_bias_kernel, t=t),
        grid=(n_heads,),
        in_specs=[pl.BlockSpec(memory_space=pltpu.SMEM)],
        out_specs=(spec, spec),
        out_shape=(shape, shape),
        compiler_params=_params("arbitrary"),
        name="rel_bias_blocks",
    )(rel_bias.reshape(-1))


def _online_update(s_t, tile_max, v_t, carry):
    m, acc = carry
    m_new = jnp.maximum(m, tile_max)
    p = jnp.exp2((s_t - m_new).astype(BF16))
    acc = jnp.exp2(m - m_new) * acc + jnp.dot(v_t, p, preferred_element_type=F32)
    return m_new, acc


def _normalized(acc):
    return acc[:LANES] / acc[LANES:LANES + 1]


def _sweep(n, t, qi, raw, fix, values, near_bias, s_ref):
    la = SCORE_LOOKAHEAD
    slots = s_ref.shape[0]
    assert n % slots == 0 and la < slots

    def produce(c, j):
        s_t = raw(c, j)
        s_ref[c % slots] = s_t
        return jnp.max(s_t, axis=0, keepdims=True)

    def step(j, state, kind):
        carries, tops = state
        tops = dict(enumerate(tops))
        nxt = []
        out = []
        for c in range(n):
            if c + la < n:
                tops[c + la] = produce(c + la, j)
            elif kind != "diag":
                nxt.append(produce(c + la - n, j + 1))
            s_t, top = s_ref[c % slots], tops.pop(c)
            if kind != "far":
                s_t = fix(c, s_t, kind)
                top = jnp.max(s_t, axis=0, keepdims=True)
            out.append(_online_update(s_t, top, values(c, j), carries[c]))
        return tuple(out), (tuple(nxt) if nxt else state[1])

    def far_steps(count, state):
        def group(i, s):
            for u in range(FAR_UNROLL):
                s = step(FAR_UNROLL * i + u, s, "far")
            return s

        assert FAR_UNROLL & (FAR_UNROLL - 1) == 0
        groups = lax.shift_right_logical(count, FAR_UNROLL.bit_length() - 1)
        state = lax.fori_loop(0, groups, group, state)
        return lax.fori_loop(groups * FAR_UNROLL, count, lambda j, s: step(j, s, "far"), state)

    state = (tuple(_init_carry(t) for _ in range(n)), tuple(produce(c, 0) for c in range(la)))
    if near_bias:
        state = far_steps(jnp.maximum(qi - 1, 0), state)
        state = lax.cond(qi >= 1, lambda s: step(qi - 1, s, "near"), lambda s: s, state)
    else:
        state = far_steps(qi, state)
    return step(qi, state, "diag")[0]


def _init_carry(t):
    return (jnp.full((1, t), NEG, F32), jnp.zeros((V_ROWS, t), F32))


def _lane_select(x, lo, hi):
    lane = lax.broadcasted_iota(jnp.int32, x.shape, 1)
    return jnp.where((lane >= lo) & (lane < hi), x, jnp.zeros_like(x))


def _merge_pair(a, b):
    row = lax.broadcasted_iota(jnp.int32, a.shape, 0)
    return jnp.where(row < HEAD_DIM, a, b)


def _causal_t(t):
    return (lax.broadcasted_iota(jnp.int32, (t, t), 0) <= lax.broadcasted_iota(jnp.int32, (t, t), 1))


def _band_bias(s_t, d_t, p_t, diag):
    n = s_t.shape[0] // BIAS_BLOCK
    rows = []
    for b in range(n):
        blocks = []
        for a in range(n):
            blk = s_t[b * BIAS_BLOCK:(b + 1) * BIAS_BLOCK, a * BIAS_BLOCK:(a + 1) * BIAS_BLOCK]
            if diag and b > a:
                blk = jnp.full_like(blk, NEG)
            elif diag and b == a:
                blk = blk + d_t
            elif (diag and b == a - 1) or (not diag and b == n - 1 and a == 0):
                blk = blk + p_t
            blocks.append(blk)
        rows.append(jnp.concatenate(blocks, axis=1))
    return jnp.concatenate(rows, axis=0)


def _attn_specs(t, seq, nq, cb, vrow):
    w = GROUP_WIDTH
    return [pl.BlockSpec((t, w), lambda b, i: (b * nq + i, cb)),
            pl.BlockSpec((seq, w), lambda b, i: (b, cb + 1)),
            pl.BlockSpec((V_SLAB, seq), lambda b, i: (vrow, b))]


def _attn_call(kernel_fn, name, batch, seq, t, in_specs, args, scratch=()):
    nq = seq // t
    return pl.pallas_call(
        kernel_fn,
        grid=(batch, nq),
        in_specs=in_specs,
        out_specs=pl.BlockSpec((t, GROUP_WIDTH), lambda b, i: (b * nq + i, 0)),
        out_shape=jax.ShapeDtypeStruct((batch * seq, GROUP_WIDTH), BF16),
        scratch_shapes=[pltpu.VMEM((SCORE_SLOTS, t, t), F32)] + list(scratch),
        compiler_params=_params("arbitrary", "arbitrary"),
        name=name,
    )(*args)


def _head_queries(q_ref, width):
    out = []
    for lo in range(0, GROUP_WIDTH, width):
        p = lo // LANES
        out.append(_lane_select(q_ref[:, p * LANES:(p + 1) * LANES], lo - p * LANES, lo - p * LANES + width))
    return out


def _fox_kernel(q_ref, k_ref, vt_ref, c_ref, o_ref, s_ref, *, t):
    qi = pl.program_id(1)
    causal = _causal_t(t)
    qms = _head_queries(q_ref, HEAD_DIM)

    def raw(h, j):
        sl = slice((h // 2) * LANES, (h // 2 + 1) * LANES)
        start = pl.multiple_of(j * t, t)
        s_t = lax.dot_general(k_ref[pl.ds(start, t), sl], qms[h], _NT, preferred_element_type=F32)
        return s_t - c_ref[pl.ds(start, t), h:h + 1]

    def fix(h, s_t, kind):
        return jnp.where(causal, s_t, NEG)

    def values(h, j):
        return vt_ref[(h // 2) * V_ROWS:(h // 2 + 1) * V_ROWS, pl.ds(pl.multiple_of(j * t, t), t)]

    carries = _sweep(GROUP_HEADS, t, qi, raw, fix, values, False, s_ref)
    outs = [_normalized(acc) for (_, acc) in carries]
    for p in range(GROUP_HEADS // 2):
        o_t = _merge_pair(outs[2 * p], outs[2 * p + 1])
        o_ref[:, p * LANES:(p + 1) * LANES] = o_t.T.astype(o_ref.dtype)


def _fox_attention(h_bf, v_t, cum_cols, batch, seq, cb, vrow, t=ATTN_TILE):
    nq = seq // t
    specs = _attn_specs(t, seq, nq, cb, vrow) + [pl.BlockSpec((seq, 8), lambda b, i: (b, 0))]
    return _attn_call(functools.partial(_fox_kernel, t=t), "fox_attn", batch, seq, t, specs,
                      (h_bf, h_bf, v_t, cum_cols))


def _diff_kernel(q_ref, k_ref, vt_ref, bd_ref, bp_ref, lam_ref, lc_ref, g_ref, o_ref, s_ref, *, t):
    qi = pl.program_id(1)
    lam_p = lam_ref[...]
    lam = (jnp.exp(jnp.sum(lam_p[0:1] * lam_p[1:2], axis=-1, keepdims=True))
           - jnp.exp(jnp.sum(lam_p[2:3] * lam_p[3:4], axis=-1, keepdims=True)) + lc_ref[0:1, 0:1])
    qms = _head_queries(q_ref, DIFF_QK_DIM)

    def raw(c, j):
        sl = slice((c // 4) * LANES, (c // 4 + 1) * LANES)
        return lax.dot_general(k_ref[pl.ds(pl.multiple_of(j * t, t), t), sl], qms[c], _NT,
                               preferred_element_type=F32)

    def fix(c, s_t, kind):
        return _band_bias(s_t, bd_ref[c // 2], bp_ref[c // 2], kind == "diag")

    def values(c, j):
        return vt_ref[(c // 4) * V_ROWS:(c // 4 + 1) * V_ROWS, pl.ds(pl.multiple_of(j * t, t), t)]

    carries = _sweep(2 * GROUP_HEADS, t, qi, raw, fix, values, True, s_ref)
    row = lax.broadcasted_iota(jnp.int32, (LANES, t), 0)
    outs = []
    for h in range(GROUP_HEADS):
        o = _normalized(carries[2 * h][1]) - lam * _normalized(carries[2 * h + 1][1])
        lo = (h % 2) * HEAD_DIM
        o = jnp.where((row >= lo) & (row < lo + HEAD_DIM), o, 0.0)
        ms = jnp.sum(o * o, axis=0, keepdims=True) * (1.0 / HEAD_DIM)
        outs.append(o * lax.rsqrt(ms + DIFF_SUBLN_EPS))
    for p in range(GROUP_HEADS // 2):
        sl = slice(p * LANES, (p + 1) * LANES)
        o_t = _merge_pair(outs[2 * p], outs[2 * p + 1])
        o_ref[:, sl] = (o_t.T * g_ref[:, sl] * lc_ref[1:2, 0:1]).astype(o_ref.dtype)


def _diff_attention(h_bf, v_t, bias_d, bias_p, lam_p, lam_c, gain, batch, seq, cb, vrow, t=ATTN_TILE):
    nq = seq // t
    bias_spec = pl.BlockSpec((GROUP_HEADS, BIAS_BLOCK, BIAS_BLOCK), lambda b, i: (0, 0, 0))
    full = lambda a: pl.BlockSpec(a.shape, lambda b, i: (0, 0))
    specs = _attn_specs(t, seq, nq, cb, vrow) + [bias_spec, bias_spec, full(lam_p), full(lam_c), full(gain)]
    return _attn_call(functools.partial(_diff_kernel, t=t), "diff_attn", batch, seq, t, specs,
                      (h_bf, h_bf, v_t, bias_d, bias_p, lam_p, lam_c, gain))


def _moba_kernel(q_ref, k_ref, vt_ref, bd_ref, bp_ref, o_ref, s_ref, kmean_ref, *, t, n_blocks):
    qi = pl.program_id(1)
    per_tile = t // MOBA_BLOCK

    @pl.when(qi == 0)
    def _():
        kmean_ref[...] = jnp.zeros_like(kmean_ref)
        for blk in range(n_blocks):
            kb = k_ref[blk * MOBA_BLOCK:(blk + 1) * MOBA_BLOCK, :].astype(F32)
            kmean_ref[blk:blk + 1, :] = jnp.sum(kb, axis=0, keepdims=True) * (1.0 / MOBA_BLOCK)

    def tile_block(pos):
        return sum((pos >= n * MOBA_BLOCK).astype(jnp.int32) for n in range(1, per_tile))

    nb = -(-n_blocks // 8) * 8
    blk_t = lax.broadcasted_iota(jnp.int32, (nb, t), 0)
    blk_tf = blk_t.astype(F32)
    own_t = qi * per_tile + tile_block(lax.broadcasted_iota(jnp.int32, (nb, t), 1))
    qms = _head_queries(q_ref, HEAD_DIM)
    q_cats = []
    for h in range(GROUP_HEADS):
        km = kmean_ref[:, (h // 2) * LANES:(h // 2 + 1) * LANES]
        km_hi = km.astype(BF16)
        km_lo = (km - km_hi.astype(F32)).astype(BF16)
        gate = (lax.dot_general(km_hi, qms[h], _NT, preferred_element_type=F32)
                + lax.dot_general(km_lo, qms[h], _NT, preferred_element_type=F32))[:nb]
        g = jnp.where(blk_t < own_t, gate, NEG)
        keep = jnp.where(blk_t == own_t, 1.0, 0.0)
        for _ in range(MOBA_TOPK):
            mx = jnp.max(g, axis=0, keepdims=True)
            first = jnp.min(jnp.where(g == mx, blk_tf, 1e9), axis=0, keepdims=True)
            pick = jnp.where(mx > 0.5 * NEG, jnp.where(blk_tf == first, 1.0, 0.0), 0.0)
            keep = jnp.maximum(keep, pick)
            g = jnp.where(pick > 0.0, NEG, g)
        drop = jnp.concatenate([1.0 - keep, jnp.zeros((LANES - nb, t), F32)], axis=0)
        q_cats.append(jnp.concatenate([qms[h], drop.T.astype(BF16)], axis=1))

    blk_i = lax.broadcasted_iota(jnp.int32, (t, LANES), 1)
    key_blk = blk_i - tile_block(lax.broadcasted_iota(jnp.int32, (t, LANES), 0))

    def raw(h, j):
        sl = slice((h // 2) * LANES, (h // 2 + 1) * LANES)
        reject = jnp.where(key_blk == j * per_tile, NEG, 0.0).astype(BF16)
        k_cat = jnp.concatenate([k_ref[pl.ds(pl.multiple_of(j * t, t), t), sl], reject], axis=1)
        return lax.dot_general(k_cat, q_cats[h], _NT, preferred_element_type=F32)

    def fix(h, s_t, kind):
        return _band_bias(s_t, bd_ref[h], bp_ref[h], kind == "diag")

    def values(h, j):
        return vt_ref[(h // 2) * V_ROWS:(h // 2 + 1) * V_ROWS, pl.ds(pl.multiple_of(j * t, t), t)]

    carries = _sweep(GROUP_HEADS, t, qi, raw, fix, values, True, s_ref)
    outs = [_normalized(acc) for (_, acc) in carries]
    for p in range(GROUP_HEADS // 2):
        o_t = _merge_pair(outs[2 * p], outs[2 * p + 1])
        o_ref[:, p * LANES:(p + 1) * LANES] = o_t.T.astype(o_ref.dtype)


def _moba_attention(h_bf, v_t, bias_d, bias_p, batch, seq, cb, vrow, t=ATTN_TILE):
    nq = seq // t
    bias_spec = pl.BlockSpec((GROUP_HEADS, BIAS_BLOCK, BIAS_BLOCK), lambda b, i: (1, 0, 0))
    specs = _attn_specs(t, seq, nq, cb, vrow) + [bias_spec, bias_spec]
    return _attn_call(functools.partial(_moba_kernel, t=t, n_blocks=seq // MOBA_BLOCK), "moba_attn",
                      batch, seq, t, specs, (h_bf, h_bf, v_t, bias_d, bias_p),
                      scratch=[pltpu.VMEM((LANES, GROUP_WIDTH), F32)])


def _in_proj_kernel(x_ref, wlat_ref, wqk_ref, wvt3_ref, ones3_ref, wgt_ref,
                    gq_ref, gkv_ref, wqa_ref, wqb_ref, wk_ref, wvt_ref, ones_ref, e_ref, tq_ref, tk_ref,
                    h_ref, vt3_ref, gate_ref, q_ref, k_ref, vt_ref):
    x = x_ref[...]
    c = jnp.dot(x, wlat_ref[...], preferred_element_type=F32)
    h_ref[...] = jnp.dot(x, wqk_ref[...], preferred_element_type=F32).astype(h_ref.dtype)
    v3 = lax.dot_general(wvt3_ref[...], x, _TN, preferred_element_type=F32) + ones3_ref[...]
    vt3_ref[...] = v3.astype(vt3_ref.dtype)
    gate_ref[...] = lax.dot_general(wgt_ref[...], x, _TN, preferred_element_type=F32)
    cq = c[:, :MLA_Q_LORA]
    ckv = c[:, MLA_Q_LORA:MLA_Q_LORA + MLA_KV_LORA]
    kr = c[:, MLA_Q_LORA + MLA_KV_LORA:]
    cq = cq * lax.rsqrt(jnp.mean(cq * cq, axis=-1, keepdims=True) + RMS_EPS) * gq_ref[...]
    ckv = ckv * lax.rsqrt(jnp.mean(ckv * ckv, axis=-1, keepdims=True) + RMS_EPS) * gkv_ref[...]
    cq = cq.astype(BF16)
    ckv = ckv.astype(BF16)
    tq = tq_ref[...]
    cos_q = jnp.concatenate([tq[:, :LANES]] * GROUP_HEADS, axis=1)
    sin_q = jnp.concatenate([tq[:, LANES:]] * GROUP_HEADS, axis=1)
    q = (jnp.dot(cq, wqa_ref[...], preferred_element_type=F32) * cos_q
         + jnp.dot(cq, wqb_ref[...], preferred_element_type=F32) * sin_q)
    q_ref[...] = (q * ((MLA_NOPE + MLA_ROPE) ** -0.5 * LOG2E)).astype(q_ref.dtype)
    k_rope = (kr * tk_ref[...]).astype(BF16)
    k = (jnp.dot(ckv, wk_ref[...], preferred_element_type=F32)
         + jnp.dot(k_rope, e_ref[...], preferred_element_type=F32))
    k_ref[...] = k.astype(k_ref.dtype)
    v_t = lax.dot_general(wvt_ref[...], ckv, _TN, preferred_element_type=F32) + ones_ref[...]
    vt_ref[...] = v_t.astype(vt_ref.dtype)


def _resident(a, layer):
    if a.ndim == 2:
        return pl.BlockSpec(a.shape, lambda *_: (0, 0), pipeline_mode=pl.Buffered(1))
    return pl.BlockSpec((None,) + a.shape[1:], lambda *_: (layer, 0, 0), pipeline_mode=pl.Buffered(1))


def _in_proj(layer, xb, w_lat, w_qk, w_vt3, ones3, w_gt, gq, gkv, wqa, wqb, wk, wvt, ones, e_mat, tab_q, tab_k,
             seq, tm=512):
    m, d = xb.shape
    ns = seq // tm
    rows = lambda w: pl.BlockSpec((tm, w), lambda i: (i, 0))
    cols = lambda r: pl.BlockSpec((r, tm), lambda i: (0, i))
    slot_w = GROUP_HEADS * LANES
    consts = (w_lat, w_qk, w_vt3, ones3, w_gt, gq, gkv, wqa, wqb, wk, wvt, ones, e_mat)
    return pl.pallas_call(
        _in_proj_kernel,
        grid=(m // tm,),
        in_specs=[rows(d)] + [_resident(a, layer) for a in consts]
                 + [pl.BlockSpec((tm, tab_q.shape[1]), lambda i: (i % ns, 0)),
                    pl.BlockSpec((tm, tab_k.shape[1]), lambda i: (i % ns, 0))],
        out_specs=(rows(w_qk.shape[-1]), cols(w_vt3.shape[-1]), cols(w_gt.shape[-1]),
                   rows(slot_w), rows(slot_w), cols(V_SLAB)),
        out_shape=(jax.ShapeDtypeStruct((m, w_qk.shape[-1]), BF16),
                   jax.ShapeDtypeStruct((w_vt3.shape[-1], m), BF16),
                   jax.ShapeDtypeStruct((w_gt.shape[-1], m), F32),
                   jax.ShapeDtypeStruct((m, slot_w), BF16),
                   jax.ShapeDtypeStruct((m, slot_w), BF16),
                   jax.ShapeDtypeStruct((V_SLAB, m), BF16)),
        compiler_params=_params("arbitrary"),
        name="in_proj",
    )(xb, *consts, tab_q, tab_k)


def _mla_kernel(q_ref, k_ref, vt_ref, o_ref, s_ref, *, t):
    qi = pl.program_id(1)
    causal = _causal_t(t)

    def raw(h, j):
        hsl = slice(h * LANES, (h + 1) * LANES)
        return lax.dot_general(k_ref[pl.ds(pl.multiple_of(j * t, t), t), hsl], q_ref[:, hsl], _NT,
                               preferred_element_type=F32)

    def fix(h, s_t, kind):
        return jnp.where(causal, s_t, NEG)

    def values(h, j):
        return vt_ref[(h // 2) * V_ROWS:(h // 2 + 1) * V_ROWS, pl.ds(pl.multiple_of(j * t, t), t)]

    carries = _sweep(GROUP_HEADS, t, qi, raw, fix, values, False, s_ref)
    outs = [_normalized(acc) for (_, acc) in carries]
    for p in range(GROUP_HEADS // 2):
        o_t = _merge_pair(outs[2 * p], outs[2 * p + 1])
        o_ref[:, p * LANES:(p + 1) * LANES] = o_t.T.astype(o_ref.dtype)


def _mla_attention(q, k, v_t, batch, seq, t=ATTN_TILE):
    nq = seq // t
    slot_w = q.shape[1]
    specs = [pl.BlockSpec((t, slot_w), lambda b, i: (b * nq + i, 0)),
             pl.BlockSpec((seq, slot_w), lambda b, i: (b, 0)),
             pl.BlockSpec((V_SLAB, seq), lambda b, i: (0, b))]
    return _attn_call(functools.partial(_mla_kernel, t=t), "mla_attn", batch, seq, t, specs, (q, k, v_t))


def _layernorm(t, g, b):
    mu = jnp.mean(t, axis=-1, keepdims=True)
    d = t - mu
    var = jnp.mean(d * d, axis=-1, keepdims=True)
    return d * lax.rsqrt(var + LN_EPS) * g + b


def _out_kernel(o1_ref, o2_ref, o3_ref, o4_ref, w_ref, x_ref, g_ref, b_ref, of_ref, ob_ref, *, alpha):
    sub = OUT_SUBTILE
    n_sub = x_ref.shape[0] // sub

    def project(r):
        rows = slice(r * sub, (r + 1) * sub)
        o = jnp.concatenate([o_ref[rows, :] for o_ref in (o1_ref, o2_ref, o3_ref, o4_ref)], axis=1)
        return jnp.dot(o, w_ref[...], preferred_element_type=F32)

    mix_next = project(0)
    for r in range(n_sub):
        mix = mix_next
        if r + 1 < n_sub:
            mix_next = project(r + 1)
        rows = slice(r * sub, (r + 1) * sub)
        y = _layernorm(alpha * x_ref[rows, :] + mix, g_ref[...], b_ref[...])
        of_ref[rows, :] = y
        ob_ref[rows, :] = y.astype(ob_ref.dtype)


def _out_proj(layer, outs, w_o, x, g, b, alpha, tm=512):
    m, d = x.shape
    o_spec = pl.BlockSpec((tm, GROUP_WIDTH), lambda i: (i, 0))
    row_spec = pl.BlockSpec((tm, d), lambda i: (i, 0))
    vec_spec = pl.BlockSpec((1, d), lambda i: (0, 0))
    return pl.pallas_call(
        functools.partial(_out_kernel, alpha=alpha),
        grid=(m // tm,),
        in_specs=[o_spec, o_spec, o_spec, o_spec,
                  _resident(w_o, layer), row_spec, vec_spec, vec_spec],
        out_specs=(row_spec, row_spec),
        out_shape=(jax.ShapeDtypeStruct((m, d), F32), jax.ShapeDtypeStruct((m, d), BF16)),
        compiler_params=_params("arbitrary"),
        name="out_proj_ln",
    )(*outs, w_o, x, g, b)


def _ffn_kernel(xb_ref, xh_ref, xf_ref, wup_ref, cp_ref, wd_ref, g_ref, b_ref,
                of_ref, ob_ref, *, alpha, tiles_per_seq, n_chunks):
    i = pl.program_id(0)
    halo = xh_ref[...]
    halo = jnp.where(i % tiles_per_seq == 0, jnp.zeros_like(halo), halo)
    xe = jnp.concatenate([halo, xb_ref[...]], axis=0)

    def cols(c, br):
        return slice(br * D_FF_PAD + c * FF_CHUNK, br * D_FF_PAD + (c + 1) * FF_CHUNK)

    def project(c):
        return [jnp.dot(xe, wup_ref[:, cols(c, br)], preferred_element_type=F32) for br in range(2)]

    def gated(c, us):
        ys = []
        for br, u in enumerate(us):
            cp = cp_ref[:, cols(c, br)]
            y = cp[2:3] * u + cp[1:2] * pltpu.roll(u, 1, 0) + cp[0:1] * pltpu.roll(u, 2, 0) + cp[3:4]
            ys.append(y[CONV_HALO:])
        return (ys[0] * jax.nn.sigmoid(ys[0]) * ys[1]).astype(BF16)

    acc = None
    ahead = [project(c) for c in range(FFN_LOOKAHEAD)]
    for c in range(n_chunks):
        if c + FFN_LOOKAHEAD < n_chunks:
            ahead.append(project(c + FFN_LOOKAHEAD))
        act = gated(c, ahead.pop(0))
        part = jnp.dot(act, wd_ref[c * FF_CHUNK:(c + 1) * FF_CHUNK, :], preferred_element_type=F32)
        acc = part if acc is None else acc + part
    y = _layernorm(alpha * xf_ref[...] + acc, g_ref[...], b_ref[...])
    of_ref[...] = y
    ob_ref[...] = y.astype(ob_ref.dtype)


def _ffn(layer, xb, xf, w_up, conv_p, w_down, g, b, alpha, seq, tm=256):
    m, d = xf.shape
    n_chunks = D_FF_PAD // FF_CHUNK
    halo_blocks = tm // CONV_HALO
    row_spec = pl.BlockSpec((tm, d), lambda i: (i, 0))
    vec_spec = pl.BlockSpec((1, d), lambda i: (0, 0))
    resident = lambda a: _resident(a, layer)
    return pl.pallas_call(
        functools.partial(_ffn_kernel, alpha=alpha, tiles_per_seq=seq // tm, n_chunks=n_chunks),
        grid=(m // tm,),
        in_specs=[row_spec,
                  pl.BlockSpec((CONV_HALO, d), lambda i: (jnp.maximum(i * halo_blocks - 1, 0), 0)),
                  row_spec, resident(w_up), resident(conv_p), resident(w_down), vec_spec, vec_spec],
        out_specs=(row_spec, row_spec),
        out_shape=(jax.ShapeDtypeStruct((m, d), F32), jax.ShapeDtypeStruct((m, d), BF16)),
        compiler_params=_params("arbitrary"),
        name="conv_ffn_ln",
    )(xb, xb, xf, w_up, conv_p, w_down, g, b)


def _swap_halves(w):
    half = w.shape[-1] // 2
    return jnp.concatenate([-w[..., half:], w[..., :half]], axis=-1)


def _value_slab(w_v):
    depth, k, _ = w_v.shape
    w = w_v.reshape(depth, k, GROUP_HEADS // 2, LANES)
    w = jnp.pad(w, ((0, 0), (0, 0), (0, 0), (0, V_ROWS - LANES)))
    return w.reshape(depth, k, V_SLAB)


def _ones_rows(n_slabs):
    row = jnp.arange(n_slabs * V_SLAB) % V_ROWS
    return (row == LANES).astype(F32)[:, None]


def _prep_weights(w_in, mla_w_uq, mla_w_ukv, w_up, conv_w, conv_b, w_down):
    depth, d, _ = w_in.shape
    gw = GROUP_WIDTH
    fox0 = 0
    diff0 = 3 * gw + GROUP_HEADS
    moba0 = diff0 + 3 * gw
    mla0 = moba0 + 3 * gw
    sm = HEAD_DIM ** -0.5 * LOG2E
    sd = DIFF_QK_DIM ** -0.5 * LOG2E
    w_qk = jnp.concatenate([
        w_in[:, :, fox0:fox0 + gw] * sm, w_in[:, :, fox0 + gw:fox0 + 2 * gw],
        w_in[:, :, diff0:diff0 + gw] * sd, w_in[:, :, diff0 + gw:diff0 + 2 * gw],
        w_in[:, :, moba0:moba0 + gw] * sm, w_in[:, :, moba0 + gw:moba0 + 2 * gw]], axis=-1).astype(BF16)
    w_vt = jnp.concatenate([_value_slab(w_in[:, :, g0 + 2 * gw:g0 + 3 * gw])
                            for g0 in (fox0, diff0, moba0)], axis=2).astype(BF16)
    w_gt = jnp.pad(w_in[:, :, 3 * gw:3 * gw + GROUP_HEADS],
                   ((0, 0), (0, 0), (0, 8 - GROUP_HEADS))).astype(BF16)
    kr0 = mla0 + MLA_Q_LORA + MLA_KV_LORA
    w_kr = w_in[:, :, kr0:kr0 + MLA_ROPE]
    w_lat = jnp.concatenate([w_in[:, :, mla0:kr0], w_kr, _swap_halves(w_kr),
                             jnp.zeros((depth, d, LANES - 2 * MLA_ROPE), F32)], axis=-1).astype(BF16)

    uq = mla_w_uq.reshape(depth, MLA_Q_LORA, GROUP_HEADS, MLA_NOPE + MLA_ROPE)
    pad = LANES - MLA_NOPE - MLA_ROPE
    wqa = jnp.pad(uq, ((0, 0), (0, 0), (0, 0), (0, pad)))
    wqb = jnp.concatenate([jnp.zeros_like(uq[..., :MLA_NOPE]), _swap_halves(uq[..., MLA_NOPE:]),
                           jnp.zeros(uq.shape[:-1] + (pad,), F32)], axis=-1)
    ukv = mla_w_ukv.reshape(depth, MLA_KV_LORA, GROUP_HEADS, MLA_NOPE + HEAD_DIM)
    wk = jnp.pad(ukv[..., :MLA_NOPE], ((0, 0), (0, 0), (0, 0), (0, LANES - MLA_NOPE)))
    flat = lambda a: a.reshape(depth, a.shape[1], -1).astype(BF16)
    wvt = _value_slab(flat(ukv[..., MLA_NOPE:])).astype(BF16)

    ffp = D_FF_PAD - D_FF
    padc = lambda a: jnp.pad(a, ((0, 0), (0, 0), (0, ffp)))
    w_up_p = jnp.concatenate([padc(w_up[:, :, :D_FF]), padc(w_up[:, :, D_FF:])], axis=-1).astype(BF16)
    conv = jnp.concatenate([conv_w, conv_b[:, None, :],
                            jnp.zeros((depth, 8 - CONV_WIDTH - 1, 2 * D_FF), F32)], axis=1)
    conv_p = jnp.concatenate([padc(conv[:, :, :D_FF]), padc(conv[:, :, D_FF:])], axis=-1)
    w_down_p = jnp.pad(w_down, ((0, 0), (0, ffp), (0, 0))).astype(BF16)
    return w_qk, w_vt, w_gt, w_lat, flat(wqa), flat(wqb), flat(wk), wvt, w_up_p, conv_p, w_down_p


def _rope_constants(seq):
    inv = ROPE_THETA ** (-jnp.arange(0, MLA_ROPE // 2, dtype=F32) * 2.0 / MLA_ROPE)
    ang = jnp.arange(seq, dtype=F32)[:, None] * inv[None, :]
    cos, sin = jnp.cos(ang), jnp.sin(ang)
    cos2 = jnp.concatenate([cos, cos], axis=1)
    sin2 = jnp.concatenate([sin, sin], axis=1)
    pad = jnp.zeros((seq, LANES - MLA_NOPE - MLA_ROPE), F32)
    tab_q = jnp.concatenate([jnp.ones((seq, MLA_NOPE), F32), cos2, pad,
                             jnp.zeros((seq, MLA_NOPE), F32), sin2, pad], axis=1)
    tab_k = jnp.concatenate([cos2, sin2, jnp.zeros((seq, LANES - 2 * MLA_ROPE), F32)], axis=1)
    r = jnp.arange(LANES)[:, None]
    c = jnp.arange(GROUP_HEADS * LANES)[None, :]
    e_mat = ((r < 2 * MLA_ROPE) & (c % LANES == MLA_NOPE + r % MLA_ROPE)).astype(BF16)
    return tab_q, tab_k, e_mat


def kernel(x, w_in, b_forget, diff_lambda, diff_subln, mla_q_norm, mla_kv_norm, mla_w_uq, mla_w_ukv,
           rel_bias, w_o, ln1_g, ln1_b, w_up, conv_w, conv_b, w_down, ln2_g, ln2_b):
    batch, seq, d = x.shape
    depth = w_in.shape[0]
    alpha = (2 * depth) ** 0.25
    assert seq % ATTN_TILE == 0 and ATTN_TILE % MOBA_BLOCK == 0 and seq // MOBA_BLOCK <= LANES

    (w_qk, w_vt, w_gt, w_lat, wqa, wqb, wk, wvt, w_up_p, conv_p, w_down_p) = _prep_weights(
        w_in, mla_w_uq, mla_w_ukv, w_up, conv_w, conv_b, w_down)
    w_o_b = w_o.astype(BF16)
    tab_q, tab_k, e_mat = _rope_constants(seq)
    bias_d, bias_p = _bias_blocks(rel_bias, BIAS_BLOCK)
    b_col = jnp.pad(b_forget, ((0, 0), (0, 8 - GROUP_HEADS)))[:, :, None]
    gain = jnp.tile(diff_subln, (1, GROUP_HEADS))[:, None, :]

    xf = x.reshape(batch * seq, d)
    xb = xf.astype(BF16)
    for l in range(depth):
        h_bf, v_t, gate_t, q_m, k_m, v_m = _in_proj(
            l, xb, w_lat, w_qk, w_vt, _ones_rows(3), w_gt, mla_q_norm[l][None], mla_kv_norm[l][None],
            wqa, wqb, wk, wvt, _ones_rows(1), e_mat, tab_q, tab_k, seq)
        cum = _fox_cumlog(gate_t, b_col[l], batch, seq)
        fox_o = _fox_attention(h_bf, v_t, cum.T, batch, seq, 0, 0)
        lam_init = 0.8 - 0.6 * math.exp(-0.3 * l)
        lam_c = jnp.concatenate([jnp.full((1, LANES), lam_init, F32),
                                 jnp.full((1, LANES), 1.0 - lam_init, F32),
                                 jnp.zeros((6, LANES), F32)], axis=0)
        diff_o = _diff_attention(h_bf, v_t, bias_d, bias_p, diff_lambda[l], lam_c, gain[l], batch, seq, 2, 1)
        moba_o = _moba_attention(h_bf, v_t, bias_d, bias_p, batch, seq, 4, 2)
        mla_o = _mla_attention(q_m, k_m, v_m, batch, seq)
        xf, xb = _out_proj(l, (fox_o, diff_o, moba_o, mla_o), w_o_b, xf, ln1_g[l][None], ln1_b[l][None], alpha)
        xf, xb = _ffn(l, xb, xf, w_up_p, conv_p, w_down_p, ln2_g[l][None], ln2_b[l][None], alpha, seq)
    return xf.reshape(batch, seq, d)
```

```python
import functools
import math

import jax
import jax.numpy as jnp
from jax import lax
from jax.experimental import pallas as pl
from jax.experimental.pallas import tpu as pltpu

F32 = jnp.float32
BF16 = jnp.bfloat16

HEAD_DIM = 64
GROUP_HEADS = 4
GROUP_WIDTH = GROUP_HEADS * HEAD_DIM
LANES = 128
DIFF_QK_DIM = HEAD_DIM // 2
DIFF_SUBLN_EPS = 1e-5
MOBA_BLOCK = 256
MOBA_TOPK = 3
MLA_Q_LORA = 256
MLA_KV_LORA = 128
MLA_NOPE = 64
MLA_ROPE = 32
ROPE_THETA = 10000.0
REL_BUCKETS = 32
REL_MAX_DIST = 128
D_FF = 2752
FF_CHUNK = 256
D_FF_PAD = -(-D_FF // FF_CHUNK) * FF_CHUNK
FFN_LOOKAHEAD = 2
OUT_SUBTILE = 128
CONV_WIDTH = 3
CONV_HALO = 16
LN_EPS = 1e-5
RMS_EPS = 1e-6
NEG = -1e30
LOG2E = 1.4426950408889634

BIAS_BLOCK = 128
ATTN_TILE = 512
V_ROWS = LANES + 16
V_SLAB = (GROUP_HEADS // 2) * V_ROWS
SCORE_LOOKAHEAD = 3
FAR_UNROLL = 4
SCORE_SLOTS = 4
VMEM_LIMIT = 56 * 1024 * 1024

_NT = (((1,), (1,)), ((), ()))
_TN = (((0,), (1,)), ((), ()))


def _params(*sem):
    return pltpu.CompilerParams(dimension_semantics=sem, vmem_limit_bytes=VMEM_LIMIT)


def _cum_kernel(g_ref, b_ref, o_ref):
    x = g_ref[...] + b_ref[...]
    x = jnp.minimum(x, 0.0) - jnp.log(1.0 + jnp.exp(-jnp.abs(x)))
    n = x.shape[1]
    lane = lax.broadcasted_iota(jnp.int32, x.shape, 1)
    k = 1
    while k < n:
        x = x + jnp.where(lane >= k, pltpu.roll(x, k, 1), 0.0)
        k *= 2
    o_ref[...] = x * LOG2E


def _fox_cumlog(logits_t, b_col, batch, seq):
    return pl.pallas_call(
        _cum_kernel,
        grid=(batch,),
        in_specs=[pl.BlockSpec((8, seq), lambda b: (0, b)),
                  pl.BlockSpec((8, 1), lambda b: (0, 0))],
        out_specs=pl.BlockSpec((8, seq), lambda b: (0, b)),
        out_shape=jax.ShapeDtypeStruct(logits_t.shape, F32),
        compiler_params=_params("arbitrary"),
        name="fox_cumlog",
    )(logits_t, b_col)


def _bias_kernel(rb_ref, d_ref, p_ref, *, t):
    h = pl.program_id(0)
    i = lax.broadcasted_iota(jnp.int32, (t, t), 0)
    j = lax.broadcasted_iota(jnp.int32, (t, t), 1)
    max_exact = REL_BUCKETS // 2
    far = rb_ref[(REL_BUCKETS - 1) * 8 + h]
    for out_ref, off in ((d_ref, 0), (p_ref, t)):
        dist = j - i + off
        d_large = jnp.maximum(dist, max_exact).astype(F32)
        large = max_exact + (jnp.log(d_large / max_exact) / math.log(REL_MAX_DIST / max_exact)
                             * (REL_BUCKETS - max_exact)).astype(jnp.int32)
        large = jnp.minimum(large, REL_BUCKETS - 1)
        bucket = jnp.where(dist < max_exact, dist, large)
        acc = jnp.zeros((t, t), F32)
        for bkt in range(REL_BUCKETS - 1):
            acc = jnp.where(bucket == bkt, (rb_ref[bkt * 8 + h] - far) * LOG2E, acc)
        if off == 0:
            acc = jnp.where(dist >= 0, acc, NEG)
        out_ref[0] = acc


def _bias_blocks(rel_bias, t):
    n_heads = rel_bias.shape[1]
    shape = jax.ShapeDtypeStruct((n_heads, t, t), F32)
    spec = pl.BlockSpec((1, t, t), lambda h: (h, 0, 0))
    return pl.pallas_call(
        functools.partial(_bias_kernel, t=t),
        grid=(n_heads,),
        in_specs=[pl.BlockSpec(memory_space=pltpu.SMEM)],
        out_specs=(spec, spec),
        out_shape=(shape, shape),
        compiler_params=_params("arbitrary"),
        name="rel_bias_blocks",
    )(rel_bias.reshape(-1))


def _online_update(s_t, tile_max, v_t, carry):
    m, acc = carry
    m_new = jnp.maximum(m, tile_max)
    p = jnp.exp2((s_t - m_new).astype(BF16))
    acc = jnp.exp2(m - m_new) * acc + jnp.dot(v_t, p, preferred_element_type=F32)
    return m_new, acc


def _normalized(acc):
    return acc[:LANES] / acc[LANES:LANES + 1]


def _sweep(n, t, qi, raw, fix, values, near_bias, s_ref):
    la = SCORE_LOOKAHEAD
    slots = s_ref.shape[0]
    assert n % slots == 0 and la < slots

    def produce(c, j):
        s_t = raw(c, j)
        s_ref[c % slots] = s_t
        return jnp.max(s_t, axis=0, keepdims=True)

    def step(j, state, kind):
        carries, tops = state
        tops = dict(enumerate(tops))
        nxt = []
        out = []
        for c in range(n):
            if c + la < n:
                tops[c + la] = produce(c + la, j)
            elif kind != "diag":
                nxt.append(produce(c + la - n, j + 1))
            s_t, top = s_ref[c % slots], tops.pop(c)
            if kind != "far":
                s_t = fix(c, s_t, kind)
                top = jnp.max(s_t, axis=0, keepdims=True)
            out.append(_online_update(s_t, top, values(c, j), carries[c]))
        return tuple(out), (tuple(nxt) if nxt else state[1])

    def far_steps(count, state):
        assert FAR_UNROLL & (FAR_UNROLL - 1) == 0
        done = jnp.int32(0)
        size = FAR_UNROLL
        while size >= 1:
            def group(i, s, size=size, done=done):
                for u in range(size):
                    s = step(done + size * i + u, s, "far")
                return s

            trips = lax.shift_right_logical(count - done, size.bit_length() - 1)
            state = lax.fori_loop(0, trips, group, state)
            done = done + trips * size
            size //= 2
        return state

    state = (tuple(_init_carry(t) for _ in range(n)), tuple(produce(c, 0) for c in range(la)))
    if near_bias:
        state = far_steps(jnp.maximum(qi - 1, 0), state)
        state = lax.cond(qi >= 1, lambda s: step(qi - 1, s, "near"), lambda s: s, state)
    else:
        state = far_steps(qi, state)
    return step(qi, state, "diag")[0]


def _init_carry(t):
    return (jnp.full((1, t), NEG, F32), jnp.zeros((V_ROWS, t), F32))


def _lane_select(x, lo, hi):
    lane = lax.broadcasted_iota(jnp.int32, x.shape, 1)
    return jnp.where((lane >= lo) & (lane < hi), x, jnp.zeros_like(x))


def _merge_pair(a, b):
    row = lax.broadcasted_iota(jnp.int32, a.shape, 0)
    return jnp.where(row < HEAD_DIM, a, b)


def _causal_t(t):
    return (lax.broadcasted_iota(jnp.int32, (t, t), 0) <= lax.broadcasted_iota(jnp.int32, (t, t), 1))


def _band_bias(s_t, d_t, p_t, diag):
    n = s_t.shape[0] // BIAS_BLOCK
    rows = []
    for b in range(n):
        blocks = []
        for a in range(n):
            blk = s_t[b * BIAS_BLOCK:(b + 1) * BIAS_BLOCK, a * BIAS_BLOCK:(a + 1) * BIAS_BLOCK]
            if diag and b > a:
                blk = jnp.full_like(blk, NEG)
            elif diag and b == a:
                blk = blk + d_t
            elif (diag and b == a - 1) or (not diag and b == n - 1 and a == 0):
                blk = blk + p_t
            blocks.append(blk)
        rows.append(jnp.concatenate(blocks, axis=1))
    return jnp.concatenate(rows, axis=0)


def _attn_specs(t, seq, nq, cb, vrow):
    w = GROUP_WIDTH
    return [pl.BlockSpec((t, w), lambda b, i: (b * nq + i, cb)),
            pl.BlockSpec((seq, w), lambda b, i: (b, cb + 1)),
            pl.BlockSpec((V_SLAB, seq), lambda b, i: (vrow, b))]


def _attn_call(kernel_fn, name, batch, seq, t, in_specs, args, scratch=()):
    nq = seq // t
    return pl.pallas_call(
        kernel_fn,
        grid=(batch, nq),
        in_specs=in_specs,
        out_specs=pl.BlockSpec((t, GROUP_WIDTH), lambda b, i: (b * nq + i, 0)),
        out_shape=jax.ShapeDtypeStruct((batch * seq, GROUP_WIDTH), BF16),
        scratch_shapes=[pltpu.VMEM((SCORE_SLOTS, t, t), F32)] + list(scratch),
        compiler_params=_params("arbitrary", "arbitrary"),
        name=name,
    )(*args)


def _head_queries(q_ref, width):
    out = []
    for lo in range(0, GROUP_WIDTH, width):
        p = lo // LANES
        out.append(_lane_select(q_ref[:, p * LANES:(p + 1) * LANES], lo - p * LANES, lo - p * LANES + width))
    return out


def _fox_kernel(q_ref, k_ref, vt_ref, c_ref, o_ref, s_ref, *, t):
    qi = pl.program_id(1)
    causal = _causal_t(t)
    qms = _head_queries(q_ref, HEAD_DIM)

    def raw(h, j):
        sl = slice((h // 2) * LANES, (h // 2 + 1) * LANES)
        start = pl.multiple_of(j * t, t)
        s_t = lax.dot_general(k_ref[pl.ds(start, t), sl], qms[h], _NT, preferred_element_type=F32)
        return s_t - c_ref[pl.ds(start, t), h:h + 1]

    def fix(h, s_t, kind):
        return jnp.where(causal, s_t, NEG)

    def values(h, j):
        return vt_ref[(h // 2) * V_ROWS:(h // 2 + 1) * V_ROWS, pl.ds(pl.multiple_of(j * t, t), t)]

    carries = _sweep(GROUP_HEADS, t, qi, raw, fix, values, False, s_ref)
    outs = [_normalized(acc) for (_, acc) in carries]
    for p in range(GROUP_HEADS // 2):
        o_t = _merge_pair(outs[2 * p], outs[2 * p + 1])
        o_ref[:, p * LANES:(p + 1) * LANES] = o_t.T.astype(o_ref.dtype)


def _fox_attention(h_bf, v_t, cum_cols, batch, seq, cb, vrow, t=ATTN_TILE):
    nq = seq // t
    specs = _attn_specs(t, seq, nq, cb, vrow) + [pl.BlockSpec((seq, 8), lambda b, i: (b, 0))]
    return _attn_call(functools.partial(_fox_kernel, t=t), "fox_attn", batch, seq, t, specs,
                      (h_bf, h_bf, v_t, cum_cols))


def _diff_kernel(q_ref, k_ref, vt_ref, bd_ref, bp_ref, lam_ref, lc_ref, g_ref, o_ref, s_ref, *, t):
    qi = pl.program_id(1)
    lam_p = lam_ref[...]
    lam = (jnp.exp(jnp.sum(lam_p[0:1] * lam_p[1:2], axis=-1, keepdims=True))
           - jnp.exp(jnp.sum(lam_p[2:3] * lam_p[3:4], axis=-1, keepdims=True)) + lc_ref[0:1, 0:1])
    qms = _head_queries(q_ref, DIFF_QK_DIM)

    def raw(c, j):
        sl = slice((c // 4) * LANES, (c // 4 + 1) * LANES)
        return lax.dot_general(k_ref[pl.ds(pl.multiple_of(j * t, t), t), sl], qms[c], _NT,
                               preferred_element_type=F32)

    def fix(c, s_t, kind):
        return _band_bias(s_t, bd_ref[c // 2], bp_ref[c // 2], kind == "diag")

    def values(c, j):
        return vt_ref[(c // 4) * V_ROWS:(c // 4 + 1) * V_ROWS, pl.ds(pl.multiple_of(j * t, t), t)]

    carries = _sweep(2 * GROUP_HEADS, t, qi, raw, fix, values, True, s_ref)
    row = lax.broadcasted_iota(jnp.int32, (LANES, t), 0)
    outs = []
    for h in range(GROUP_HEADS):
        o = _normalized(carries[2 * h][1]) - lam * _normalized(carries[2 * h + 1][1])
        lo = (h % 2) * HEAD_DIM
        o = jnp.where((row >= lo) & (row < lo + HEAD_DIM), o, 0.0)
        ms = jnp.sum(o * o, axis=0, keepdims=True) * (1.0 / HEAD_DIM)
        outs.append(o * lax.rsqrt(ms + DIFF_SUBLN_EPS))
    for p in range(GROUP_HEADS // 2):
        sl = slice(p * LANES, (p + 1) * LANES)
        o_t = _merge_pair(outs[2 * p], outs[2 * p + 1])
        o_ref[:, sl] = (o_t.T * g_ref[:, sl] * lc_ref[1:2, 0:1]).astype(o_ref.dtype)


def _diff_attention(h_bf, v_t, bias_d, bias_p, lam_p, lam_c, gain, batch, seq, cb, vrow, t=ATTN_TILE):
    nq = seq // t
    bias_spec = pl.BlockSpec((GROUP_HEADS, BIAS_BLOCK, BIAS_BLOCK), lambda b, i: (0, 0, 0))
    full = lambda a: pl.BlockSpec(a.shape, lambda b, i: (0, 0))
    specs = _attn_specs(t, seq, nq, cb, vrow) + [bias_spec, bias_spec, full(lam_p), full(lam_c), full(gain)]
    return _attn_call(functools.partial(_diff_kernel, t=t), "diff_attn", batch, seq, t, specs,
                      (h_bf, h_bf, v_t, bias_d, bias_p, lam_p, lam_c, gain))


def _moba_kernel(q_ref, k_ref, vt_ref, bd_ref, bp_ref, o_ref, s_ref, kmean_ref, *, t, n_blocks):
    qi = pl.program_id(1)
    per_tile = t // MOBA_BLOCK

    @pl.when(qi == 0)
    def _():
        kmean_ref[...] = jnp.zeros_like(kmean_ref)
        for blk in range(n_blocks):
            kb = k_ref[blk * MOBA_BLOCK:(blk + 1) * MOBA_BLOCK, :].astype(F32)
            kmean_ref[blk:blk + 1, :] = jnp.sum(kb, axis=0, keepdims=True) * (1.0 / MOBA_BLOCK)

    def tile_block(pos):
        return sum((pos >= n * MOBA_BLOCK).astype(jnp.int32) for n in range(1, per_tile))

    nb = -(-n_blocks // 8) * 8
    blk_t = lax.broadcasted_iota(jnp.int32, (nb, t), 0)
    blk_tf = blk_t.astype(F32)
    own_t = qi * per_tile + tile_block(lax.broadcasted_iota(jnp.int32, (nb, t), 1))
    qms = _head_queries(q_ref, HEAD_DIM)
    q_cats = []
    for h in range(GROUP_HEADS):
        km = kmean_ref[:, (h // 2) * LANES:(h // 2 + 1) * LANES]
        km_hi = km.astype(BF16)
        km_lo = (km - km_hi.astype(F32)).astype(BF16)
        gate = (lax.dot_general(km_hi, qms[h], _NT, preferred_element_type=F32)
                + lax.dot_general(km_lo, qms[h], _NT, preferred_element_type=F32))[:nb]
        g = jnp.where(blk_t < own_t, gate, NEG)
        keep = jnp.where(blk_t == own_t, 1.0, 0.0)
        for _ in range(MOBA_TOPK):
            mx = jnp.max(g, axis=0, keepdims=True)
            first = jnp.min(jnp.where(g == mx, blk_tf, 1e9), axis=0, keepdims=True)
            pick = jnp.where(mx > 0.5 * NEG, jnp.where(blk_tf == first, 1.0, 0.0), 0.0)
            keep = jnp.maximum(keep, pick)
            g = jnp.where(pick > 0.0, NEG, g)
        drop = jnp.concatenate([1.0 - keep, jnp.zeros((LANES - nb, t), F32)], axis=0)
        q_cats.append(jnp.concatenate([qms[h], drop.T.astype(BF16)], axis=1))

    blk_i = lax.broadcasted_iota(jnp.int32, (t, LANES), 1)
    key_blk = blk_i - tile_block(lax.broadcasted_iota(jnp.int32, (t, LANES), 0))

    def raw(h, j):
        sl = slice((h // 2) * LANES, (h // 2 + 1) * LANES)
        reject = jnp.where(key_blk == j * per_tile, NEG, 0.0).astype(BF16)
        k_cat = jnp.concatenate([k_ref[pl.ds(pl.multiple_of(j * t, t), t), sl], reject], axis=1)
        return lax.dot_general(k_cat, q_cats[h], _NT, preferred_element_type=F32)

    def fix(h, s_t, kind):
        return _band_bias(s_t, bd_ref[h], bp_ref[h], kind == "diag")

    def values(h, j):
        return vt_ref[(h // 2) * V_ROWS:(h // 2 + 1) * V_ROWS, pl.ds(pl.multiple_of(j * t, t), t)]

    carries = _sweep(GROUP_HEADS, t, qi, raw, fix, values, True, s_ref)
    outs = [_normalized(acc) for (_, acc) in carries]
    for p in range(GROUP_HEADS // 2):
        o_t = _merge_pair(outs[2 * p], outs[2 * p + 1])
        o_ref[:, p * LANES:(p + 1) * LANES] = o_t.T.astype(o_ref.dtype)


def _moba_attention(h_bf, v_t, bias_d, bias_p, batch, seq, cb, vrow, t=ATTN_TILE):
    nq = seq // t
    bias_spec = pl.BlockSpec((GROUP_HEADS, BIAS_BLOCK, BIAS_BLOCK), lambda b, i: (1, 0, 0))
    specs = _attn_specs(t, seq, nq, cb, vrow) + [bias_spec, bias_spec]
    return _attn_call(functools.partial(_moba_kernel, t=t, n_blocks=seq // MOBA_BLOCK), "moba_attn",
                      batch, seq, t, specs, (h_bf, h_bf, v_t, bias_d, bias_p),
                      scratch=[pltpu.VMEM((LANES, GROUP_WIDTH), F32)])


def _in_proj_kernel(x_ref, wlat_ref, wqk_ref, wvt3_ref, ones3_ref, wgt_ref,
                    gq_ref, gkv_ref, wqa_ref, wqb_ref, wk_ref, wvt_ref, ones_ref, e_ref, tq_ref, tk_ref,
                    h_ref, vt3_ref, gate_ref, q_ref, k_ref, vt_ref):
    x = x_ref[...]
    c = jnp.dot(x, wlat_ref[...], preferred_element_type=F32)
    h_ref[...] = jnp.dot(x, wqk_ref[...], preferred_element_type=F32).astype(h_ref.dtype)
    v3 = lax.dot_general(wvt3_ref[...], x, _TN, preferred_element_type=F32) + ones3_ref[...]
    vt3_ref[...] = v3.astype(vt3_ref.dtype)
    gate_ref[...] = lax.dot_general(wgt_ref[...], x, _TN, preferred_element_type=F32)
    cq = c[:, :MLA_Q_LORA]
    ckv = c[:, MLA_Q_LORA:MLA_Q_LORA + MLA_KV_LORA]
    kr = c[:, MLA_Q_LORA + MLA_KV_LORA:]
    cq = cq * lax.rsqrt(jnp.mean(cq * cq, axis=-1, keepdims=True) + RMS_EPS) * gq_ref[...]
    ckv = ckv * lax.rsqrt(jnp.mean(ckv * ckv, axis=-1, keepdims=True) + RMS_EPS) * gkv_ref[...]
    cq = cq.astype(BF16)
    ckv = ckv.astype(BF16)
    tq = tq_ref[...]
    cos_q = jnp.concatenate([tq[:, :LANES]] * GROUP_HEADS, axis=1)
    sin_q = jnp.concatenate([tq[:, LANES:]] * GROUP_HEADS, axis=1)
    q = (jnp.dot(cq, wqa_ref[...], preferred_element_type=F32) * cos_q
         + jnp.dot(cq, wqb_ref[...], preferred_element_type=F32) * sin_q)
    q_ref[...] = (q * ((MLA_NOPE + MLA_ROPE) ** -0.5 * LOG2E)).astype(q_ref.dtype)
    k_rope = (kr * tk_ref[...]).astype(BF16)
    k = (jnp.dot(ckv, wk_ref[...], preferred_element_type=F32)
         + jnp.dot(k_rope, e_ref[...], preferred_element_type=F32))
    k_ref[...] = k.astype(k_ref.dtype)
    v_t = lax.dot_general(wvt_ref[...], ckv, _TN, preferred_element_type=F32) + ones_ref[...]
    vt_ref[...] = v_t.astype(vt_ref.dtype)


def _resident(a, layer):
    if a.ndim == 2:
        return pl.BlockSpec(a.shape, lambda *_: (0, 0), pipeline_mode=pl.Buffered(1))
    return pl.BlockSpec((None,) + a.shape[1:], lambda *_: (layer, 0, 0), pipeline_mode=pl.Buffered(1))


def _in_proj(layer, xb, w_lat, w_qk, w_vt3, ones3, w_gt, gq, gkv, wqa, wqb, wk, wvt, ones, e_mat, tab_q, tab_k,
             seq, tm=512):
    m, d = xb.shape
    ns = seq // tm
    rows = lambda w: pl.BlockSpec((tm, w), lambda i: (i, 0))
    cols = lambda r: pl.BlockSpec((r, tm), lambda i: (0, i))
    slot_w = GROUP_HEADS * LANES
    consts = (w_lat, w_qk, w_vt3, ones3, w_gt, gq, gkv, wqa, wqb, wk, wvt, ones, e_mat)
    return pl.pallas_call(
        _in_proj_kernel,
        grid=(m // tm,),
        in_specs=[rows(d)] + [_resident(a, layer) for a in consts]
                 + [pl.BlockSpec((tm, tab_q.shape[1]), lambda i: (i % ns, 0)),
                    pl.BlockSpec((tm, tab_k.shape[1]), lambda i: (i % ns, 0))],
        out_specs=(rows(w_qk.shape[-1]), cols(w_vt3.shape[-1]), cols(w_gt.shape[-1]),
                   rows(slot_w), rows(slot_w), cols(V_SLAB)),
        out_shape=(jax.ShapeDtypeStruct((m, w_qk.shape[-1]), BF16),
                   jax.ShapeDtypeStruct((w_vt3.shape[-1], m), BF16),
                   jax.ShapeDtypeStruct((w_gt.shape[-1], m), F32),
                   jax.ShapeDtypeStruct((m, slot_w), BF16),
                   jax.ShapeDtypeStruct((m, slot_w), BF16),
                   jax.ShapeDtypeStruct((V_SLAB, m), BF16)),
        compiler_params=_params("arbitrary"),
        name="in_proj",
    )(xb, *consts, tab_q, tab_k)


def _mla_kernel(q_ref, k_ref, vt_ref, o_ref, s_ref, *, t):
    qi = pl.program_id(1)
    causal = _causal_t(t)

    def raw(h, j):
        hsl = slice(h * LANES, (h + 1) * LANES)
        return lax.dot_general(k_ref[pl.ds(pl.multiple_of(j * t, t), t), hsl], q_ref[:, hsl], _NT,
                               preferred_element_type=F32)

    def fix(h, s_t, kind):
        return jnp.where(causal, s_t, NEG)

    def values(h, j):
        return vt_ref[(h // 2) * V_ROWS:(h // 2 + 1) * V_ROWS, pl.ds(pl.multiple_of(j * t, t), t)]

    carries = _sweep(GROUP_HEADS, t, qi, raw, fix, values, False, s_ref)
    outs = [_normalized(acc) for (_, acc) in carries]
    for p in range(GROUP_HEADS // 2):
        o_t = _merge_pair(outs[2 * p], outs[2 * p + 1])
        o_ref[:, p * LANES:(p + 1) * LANES] = o_t.T.astype(o_ref.dtype)


def _mla_attention(q, k, v_t, batch, seq, t=ATTN_TILE):
    nq = seq // t
    slot_w = q.shape[1]
    specs = [pl.BlockSpec((t, slot_w), lambda b, i: (b * nq + i, 0)),
             pl.BlockSpec((seq, slot_w), lambda b, i: (b, 0)),
             pl.BlockSpec((V_SLAB, seq), lambda b, i: (0, b))]
    return _attn_call(functools.partial(_mla_kernel, t=t), "mla_attn", batch, seq, t, specs, (q, k, v_t))


def _layernorm(t, g, b):
    mu = jnp.mean(t, axis=-1, keepdims=True)
    d = t - mu
    var = jnp.mean(d * d, axis=-1, keepdims=True)
    return d * lax.rsqrt(var + LN_EPS) * g + b


def _out_kernel(o1_ref, o2_ref, o3_ref, o4_ref, w_ref, x_ref, g_ref, b_ref, of_ref, ob_ref, *, alpha):
    sub = OUT_SUBTILE
    n_sub = x_ref.shape[0] // sub

    def project(r):
        rows = slice(r * sub, (r + 1) * sub)
        o = jnp.concatenate([o_ref[rows, :] for o_ref in (o1_ref, o2_ref, o3_ref, o4_ref)], axis=1)
        return jnp.dot(o, w_ref[...], preferred_element_type=F32)

    mix_next = project(0)
    for r in range(n_sub):
        mix = mix_next
        if r + 1 < n_sub:
            mix_next = project(r + 1)
        rows = slice(r * sub, (r + 1) * sub)
        y = _layernorm(alpha * x_ref[rows, :] + mix, g_ref[...], b_ref[...])
        of_ref[rows, :] = y
        ob_ref[rows, :] = y.astype(ob_ref.dtype)


def _out_proj(layer, outs, w_o, x, g, b, alpha, tm=512):
    m, d = x.shape
    o_spec = pl.BlockSpec((tm, GROUP_WIDTH), lambda i: (i, 0))
    row_spec = pl.BlockSpec((tm, d), lambda i: (i, 0))
    vec_spec = pl.BlockSpec((1, d), lambda i: (0, 0))
    return pl.pallas_call(
        functools.partial(_out_kernel, alpha=alpha),
        grid=(m // tm,),
        in_specs=[o_spec, o_spec, o_spec, o_spec,
                  _resident(w_o, layer), row_spec, vec_spec, vec_spec],
        out_specs=(row_spec, row_spec),
        out_shape=(jax.ShapeDtypeStruct((m, d), F32), jax.ShapeDtypeStruct((m, d), BF16)),
        compiler_params=_params("arbitrary"),
        name="out_proj_ln",
    )(*outs, w_o, x, g, b)


def _ffn_kernel(xb_ref, xh_ref, xf_ref, wup_ref, cp_ref, wd_ref, g_ref, b_ref,
                of_ref, ob_ref, *, alpha, tiles_per_seq, n_chunks):
    i = pl.program_id(0)
    halo = xh_ref[...]
    halo = jnp.where(i % tiles_per_seq == 0, jnp.zeros_like(halo), halo)
    xe = jnp.concatenate([halo, xb_ref[...]], axis=0)

    def cols(c, br):
        return slice(br * D_FF_PAD + c * FF_CHUNK, br * D_FF_PAD + (c + 1) * FF_CHUNK)

    def project(c):
        return [jnp.dot(xe, wup_ref[:, cols(c, br)], preferred_element_type=F32) for br in range(2)]

    def gated(c, us):
        ys = []
        for br, u in enumerate(us):
            cp = cp_ref[:, cols(c, br)]
            y = cp[2:3] * u + cp[1:2] * pltpu.roll(u, 1, 0) + cp[0:1] * pltpu.roll(u, 2, 0) + cp[3:4]
            ys.append(y[CONV_HALO:])
        return (ys[0] * jax.nn.sigmoid(ys[0]) * ys[1]).astype(BF16)

    acc = None
    ahead = [project(c) for c in range(FFN_LOOKAHEAD)]
    for c in range(n_chunks):
        if c + FFN_LOOKAHEAD < n_chunks:
            ahead.append(project(c + FFN_LOOKAHEAD))
        act = gated(c, ahead.pop(0))
        part = jnp.dot(act, wd_ref[c * FF_CHUNK:(c + 1) * FF_CHUNK, :], preferred_element_type=F32)
        acc = part if acc is None else acc + part
    y = _layernorm(alpha * xf_ref[...] + acc, g_ref[...], b_ref[...])
    of_ref[...] = y
    ob_ref[...] = y.astype(ob_ref.dtype)


def _ffn(layer, xb, xf, w_up, conv_p, w_down, g, b, alpha, seq, tm=256):
    m, d = xf.shape
    n_chunks = D_FF_PAD // FF_CHUNK
    halo_blocks = tm // CONV_HALO
    row_spec = pl.BlockSpec((tm, d), lambda i: (i, 0))
    vec_spec = pl.BlockSpec((1, d), lambda i: (0, 0))
    resident = lambda a: _resident(a, layer)
    return pl.pallas_call(
        functools.partial(_ffn_kernel, alpha=alpha, tiles_per_seq=seq // tm, n_chunks=n_chunks),
        grid=(m // tm,),
        in_specs=[row_spec,
                  pl.BlockSpec((CONV_HALO, d), lambda i: (jnp.maximum(i * halo_blocks - 1, 0), 0)),
                  row_spec, resident(w_up), resident(conv_p), resident(w_down), vec_spec, vec_spec],
        out_specs=(row_spec, row_spec),
        out_shape=(jax.ShapeDtypeStruct((m, d), F32), jax.ShapeDtypeStruct((m, d), BF16)),
        compiler_params=_params("arbitrary"),
        name="conv_ffn_ln",
    )(xb, xb, xf, w_up, conv_p, w_down, g, b)


def _swap_halves(w):
    half = w.shape[-1] // 2
    return jnp.concatenate([-w[..., half:], w[..., :half]], axis=-1)


def _value_slab(w_v):
    depth, k, _ = w_v.shape
    w = w_v.reshape(depth, k, GROUP_HEADS // 2, LANES)
    w = jnp.pad(w, ((0, 0), (0, 0), (0, 0), (0, V_ROWS - LANES)))
    return w.reshape(depth, k, V_SLAB)


def _ones_rows(n_slabs):
    row = jnp.arange(n_slabs * V_SLAB) % V_ROWS
    return (row == LANES).astype(F32)[:, None]


def _prep_weights(w_in, mla_w_uq, mla_w_ukv, w_up, conv_w, conv_b, w_down):
    depth, d, _ = w_in.shape
    gw = GROUP_WIDTH
    fox0 = 0
    diff0 = 3 * gw + GROUP_HEADS
    moba0 = diff0 + 3 * gw
    mla0 = moba0 + 3 * gw
    sm = HEAD_DIM ** -0.5 * LOG2E
    sd = DIFF_QK_DIM ** -0.5 * LOG2E
    w_qk = jnp.concatenate([
        w_in[:, :, fox0:fox0 + gw] * sm, w_in[:, :, fox0 + gw:fox0 + 2 * gw],
        w_in[:, :, diff0:diff0 + gw] * sd, w_in[:, :, diff0 + gw:diff0 + 2 * gw],
        w_in[:, :, moba0:moba0 + gw] * sm, w_in[:, :, moba0 + gw:moba0 + 2 * gw]], axis=-1).astype(BF16)
    w_vt = jnp.concatenate([_value_slab(w_in[:, :, g0 + 2 * gw:g0 + 3 * gw])
                            for g0 in (fox0, diff0, moba0)], axis=2).astype(BF16)
    w_gt = jnp.pad(w_in[:, :, 3 * gw:3 * gw + GROUP_HEADS],
                   ((0, 0), (0, 0), (0, 8 - GROUP_HEADS))).astype(BF16)
    kr0 = mla0 + MLA_Q_LORA + MLA_KV_LORA
    w_kr = w_in[:, :, kr0:kr0 + MLA_ROPE]
    w_lat = jnp.concatenate([w_in[:, :, mla0:kr0], w_kr, _swap_halves(w_kr),
                             jnp.zeros((depth, d, LANES - 2 * MLA_ROPE), F32)], axis=-1).astype(BF16)

    uq = mla_w_uq.reshape(depth, MLA_Q_LORA, GROUP_HEADS, MLA_NOPE + MLA_ROPE)
    pad = LANES - MLA_NOPE - MLA_ROPE
    wqa = jnp.pad(uq, ((0, 0), (0, 0), (0, 0), (0, pad)))
    wqb = jnp.concatenate([jnp.zeros_like(uq[..., :MLA_NOPE]), _swap_halves(uq[..., MLA_NOPE:]),
                           jnp.zeros(uq.shape[:-1] + (pad,), F32)], axis=-1)
    ukv = mla_w_ukv.reshape(depth, MLA_KV_LORA, GROUP_HEADS, MLA_NOPE + HEAD_DIM)
    wk = jnp.pad(ukv[..., :MLA_NOPE], ((0, 0), (0, 0), (0, 0), (0, LANES - MLA_NOPE)))
    flat = lambda a: a.reshape(depth, a.shape[1], -1).astype(BF16)
    wvt = _value_slab(flat(ukv[..., MLA_NOPE:])).astype(BF16)

    ffp = D_FF_PAD - D_FF
    padc = lambda a: jnp.pad(a, ((0, 0), (0, 0), (0, ffp)))
    w_up_p = jnp.concatenate([padc(w_up[:, :, :D_FF]), padc(w_up[:, :, D_FF:])], axis=-1).astype(BF16)
    conv = jnp.concatenate([conv_w, conv_b[:, None, :],
                            jnp.zeros((depth, 8 - CONV_WIDTH - 1, 2 * D_FF), F32)], axis=1)
    conv_p = jnp.concatenate([padc(conv[:, :, :D_FF]), padc(conv[:, :, D_FF:])], axis=-1)
    w_down_p = jnp.pad(w_down, ((0, 0), (0, ffp), (0, 0))).astype(BF16)
    return w_qk, w_vt, w_gt, w_lat, flat(wqa), flat(wqb), flat(wk), wvt, w_up_p, conv_p, w_down_p


def _rope_constants(seq):
    inv = ROPE_THETA ** (-jnp.arange(0, MLA_ROPE // 2, dtype=F32) * 2.0 / MLA_ROPE)
    ang = jnp.arange(seq, dtype=F32)[:, None] * inv[None, :]
    cos, sin = jnp.cos(ang), jnp.sin(ang)
    cos2 = jnp.concatenate([cos, cos], axis=1)
    sin2 = jnp.concatenate([sin, sin], axis=1)
    pad = jnp.zeros((seq, LANES - MLA_NOPE - MLA_ROPE), F32)
    tab_q = jnp.concatenate([jnp.ones((seq, MLA_NOPE), F32), cos2, pad,
                             jnp.zeros((seq, MLA_NOPE), F32), sin2, pad], axis=1)
    tab_k = jnp.concatenate([cos2, sin2, jnp.zeros((seq, LANES - 2 * MLA_ROPE), F32)], axis=1)
    r = jnp.arange(LANES)[:, None]
    c = jnp.arange(GROUP_HEADS * LANES)[None, :]
    e_mat = ((r < 2 * MLA_ROPE) & (c % LANES == MLA_NOPE + r % MLA_ROPE)).astype(BF16)
    return tab_q, tab_k, e_mat


def kernel(x, w_in, b_forget, diff_lambda, diff_subln, mla_q_norm, mla_kv_norm, mla_w_uq, mla_w_ukv,
           rel_bias, w_o, ln1_g, ln1_b, w_up, conv_w, conv_b, w_down, ln2_g, ln2_b):
    batch, seq, d = x.shape
    depth = w_in.shape[0]
    alpha = (2 * depth) ** 0.25
    assert seq % ATTN_TILE == 0 and ATTN_TILE % MOBA_BLOCK == 0 and seq // MOBA_BLOCK <= LANES

    (w_qk, w_vt, w_gt, w_lat, wqa, wqb, wk, wvt, w_up_p, conv_p, w_down_p) = _prep_weights(
        w_in, mla_w_uq, mla_w_ukv, w_up, conv_w, conv_b, w_down)
    w_o_b = w_o.astype(BF16)
    tab_q, tab_k, e_mat = _rope_constants(seq)
    bias_d, bias_p = _bias_blocks(rel_bias, BIAS_BLOCK)
    b_col = jnp.pad(b_forget, ((0, 0), (0, 8 - GROUP_HEADS)))[:, :, None]
    gain = jnp.tile(diff_subln, (1, GROUP_HEADS))[:, None, :]

    xf = x.reshape(batch * seq, d)
    xb = xf.astype(BF16)
    for l in range(depth):
        h_bf, v_t, gate_t, q_m, k_m, v_m = _in_proj(
            l, xb, w_lat, w_qk, w_vt, _ones_rows(3), w_gt, mla_q_norm[l][None], mla_kv_norm[l][None],
            wqa, wqb, wk, wvt, _ones_rows(1), e_mat, tab_q, tab_k, seq)
        cum = _fox_cumlog(gate_t, b_col[l], batch, seq)
        fox_o = _fox_attention(h_bf, v_t, cum.T, batch, seq, 0, 0)
        lam_init = 0.8 - 0.6 * math.exp(-0.3 * l)
        lam_c = jnp.concatenate([jnp.full((1, LANES), lam_init, F32),
                                 jnp.full((1, LANES), 1.0 - lam_init, F32),
                                 jnp.zeros((6, LANES), F32)], axis=0)
        diff_o = _diff_attention(h_bf, v_t, bias_d, bias_p, diff_lambda[l], lam_c, gain[l], batch, seq, 2, 1)
        moba_o = _moba_attention(h_bf, v_t, bias_d, bias_p, batch, seq, 4, 2)
        mla_o = _mla_attention(q_m, k_m, v_m, batch, seq)
        xf, xb = _out_proj(l, (fox_o, diff_o, moba_o, mla_o), w_o_b, xf, ln1_g[l][None], ln1_b[l][None], alpha)
        xf, xb = _ffn(l, xb, xf, w_up_p, conv_p, w_down_p, ln2_g[l][None], ln2_b[l][None], alpha, seq)
    return xf.reshape(batch, seq, d)
```

```python
import functools
import math

import jax
import jax.numpy as jnp
from jax import lax
from jax.experimental import pallas as pl
from jax.experimental.pallas import tpu as pltpu

F32 = jnp.float32
BF16 = jnp.bfloat16

HEAD_DIM = 64
GROUP_HEADS = 4
GROUP_WIDTH = GROUP_HEADS * HEAD_DIM
LANES = 128
DIFF_QK_DIM = HEAD_DIM // 2
DIFF_SUBLN_EPS = 1e-5
MOBA_BLOCK = 256
MOBA_TOPK = 3
MLA_Q_LORA = 256
MLA_KV_LORA = 128
MLA_NOPE = 64
MLA_ROPE = 32
ROPE_THETA = 10000.0
REL_BUCKETS = 32
REL_MAX_DIST = 128
D_FF = 2752
FF_CHUNK = 256
D_FF_PAD = -(-D_FF // FF_CHUNK) * FF_CHUNK
FFN_LOOKAHEAD = 2
OUT_SUBTILE = 128
CONV_WIDTH = 3
CONV_HALO = 16
LN_EPS = 1e-5
RMS_EPS = 1e-6
NEG = -1e30
LOG2E = 1.4426950408889634

BIAS_BLOCK = 128
ATTN_TILE = 512
V_ROWS = LANES + 16
V_SLAB = (GROUP_HEADS // 2) * V_ROWS
SCORE_LOOKAHEAD = 3
FAR_UNROLL = 4
SCORE_SLOTS = 4
VMEM_LIMIT = 56 * 1024 * 1024

_NT = (((1,), (1,)), ((), ()))
_TN = (((0,), (1,)), ((), ()))


def _params(*sem):
    return pltpu.CompilerParams(dimension_semantics=sem, vmem_limit_bytes=VMEM_LIMIT)


def _cum_kernel(g_ref, b_ref, o_ref):
    x = g_ref[...] + b_ref[...]
    x = jnp.minimum(x, 0.0) - jnp.log(1.0 + jnp.exp(-jnp.abs(x)))
    n = x.shape[1]
    lane = lax.broadcasted_iota(jnp.int32, x.shape, 1)
    k = 1
    while k < n:
        x = x + jnp.where(lane >= k, pltpu.roll(x, k, 1), 0.0)
        k *= 2
    o_ref[...] = x * LOG2E


def _fox_cumlog(logits_t, b_col, batch, seq):
    return pl.pallas_call(
        _cum_kernel,
        grid=(batch,),
        in_specs=[pl.BlockSpec((8, seq), lambda b: (0, b)),
                  pl.BlockSpec((8, 1), lambda b: (0, 0))],
        out_specs=pl.BlockSpec((8, seq), lambda b: (0, b)),
        out_shape=jax.ShapeDtypeStruct(logits_t.shape, F32),
        compiler_params=_params("arbitrary"),
        name="fox_cumlog",
    )(logits_t, b_col)


def _bias_kernel(rb_ref, d_ref, p_ref, *, t):
    h = pl.program_id(0)
    i = lax.broadcasted_iota(jnp.int32, (t, t), 0)
    j = lax.broadcasted_iota(jnp.int32, (t, t), 1)
    max_exact = REL_BUCKETS // 2
    far = rb_ref[(REL_BUCKETS - 1) * 8 + h]
    for out_ref, off in ((d_ref, 0), (p_ref, t)):
        dist = j - i + off
        d_large = jnp.maximum(dist, max_exact).astype(F32)
        large = max_exact + (jnp.log(d_large / max_exact) / math.log(REL_MAX_DIST / max_exact)
                             * (REL_BUCKETS - max_exact)).astype(jnp.int32)
        large = jnp.minimum(large, REL_BUCKETS - 1)
        bucket = jnp.where(dist < max_exact, dist, large)
        acc = jnp.zeros((t, t), F32)
        for bkt in range(REL_BUCKETS - 1):
            acc = jnp.where(bucket == bkt, (rb_ref[bkt * 8 + h] - far) * LOG2E, acc)
        if off == 0:
            acc = jnp.where(dist >= 0, acc, NEG)
        out_ref[0] = acc


def _bias_blocks(rel_bias, t):
    n_heads = rel_bias.shape[1]
    shape = jax.ShapeDtypeStruct((n_heads, t, t), F32)
    spec = pl.BlockSpec((1, t, t), lambda h: (h, 0, 0))
    return pl.pallas_call(
        functools.partial(_bias_kernel, t=t),
        grid=(n_heads,),
        in_specs=[pl.BlockSpec(memory_space=pltpu.SMEM)],
        out_specs=(spec, spec),
        out_shape=(shape, shape),
        compiler_params=_params("arbitrary"),
        name="rel_bias_blocks",
    )(rel_bias.reshape(-1))


def _online_update(s_t, tile_max, v_t, carry):
    m, acc = carry
    m_new = jnp.maximum(m, tile_max)
    p = jnp.exp2((s_t - m_new).astype(BF16))
    acc = jnp.exp2(m - m_new) * acc + jnp.dot(v_t, p, preferred_element_type=F32)
    return m_new, acc


def _normalized(acc):
    return acc[:LANES] / acc[LANES:LANES + 1]


def _sweep(n, t, qi, raw, fix, values, near_bias, s_ref):
    la = SCORE_LOOKAHEAD
    slots = s_ref.shape[0]
    assert n % slots == 0 and la < slots

    def produce(c, j):
        s_t = raw(c, j)
        s_ref[c % slots] = s_t
        return jnp.max(s_t, axis=0, keepdims=True)

    def step(j, state, kind):
        carries, tops = state
        tops = dict(enumerate(tops))
        nxt = []
        out = []
        for c in range(n):
            if c + la < n:
                tops[c + la] = produce(c + la, j)
            elif kind != "diag":
                nxt.append(produce(c + la - n, j + 1))
            s_t, top = s_ref[c % slots], tops.pop(c)
            if kind != "far":
                s_t = fix(c, s_t, kind)
                top = jnp.max(s_t, axis=0, keepdims=True)
            out.append(_online_update(s_t, top, values(c, j), carries[c]))
        return tuple(out), (tuple(nxt) if nxt else state[1])

    def far_steps(count, state):
        def group(i, s):
            for u in range(FAR_UNROLL):
                s = step(FAR_UNROLL * i + u, s, "far")
            return s

        assert FAR_UNROLL & (FAR_UNROLL - 1) == 0
        groups = lax.shift_right_logical(count, FAR_UNROLL.bit_length() - 1)
        state = lax.fori_loop(0, groups, group, state)
        return lax.fori_loop(groups * FAR_UNROLL, count, lambda j, s: step(j, s, "far"), state)

    state = (tuple(_init_carry(t) for _ in range(n)), tuple(produce(c, 0) for c in range(la)))
    if near_bias:
        state = far_steps(jnp.maximum(qi - 1, 0), state)
        state = lax.cond(qi >= 1, lambda s: step(qi - 1, s, "near"), lambda s: s, state)
    else:
        state = far_steps(qi, state)
    return step(qi, state, "diag")[0]


def _init_carry(t):
    return (jnp.full((1, t), NEG, F32), jnp.zeros((V_ROWS, t), F32))


def _lane_select(x, lo, hi):
    lane = lax.broadcasted_iota(jnp.int32, x.shape, 1)
    return jnp.where((lane >= lo) & (lane < hi), x, jnp.zeros_like(x))


def _merge_pair(a, b):
    row = lax.broadcasted_iota(jnp.int32, a.shape, 0)
    return jnp.where(row < HEAD_DIM, a, b)


def _causal_t(t):
    return (lax.broadcasted_iota(jnp.int32, (t, t), 0) <= lax.broadcasted_iota(jnp.int32, (t, t), 1))


def _band_bias(s_t, d_t, p_t, diag):
    n = s_t.shape[0] // BIAS_BLOCK
    rows = []
    for b in range(n):
        blocks = []
        for a in range(n):
            blk = s_t[b * BIAS_BLOCK:(b + 1) * BIAS_BLOCK, a * BIAS_BLOCK:(a + 1) * BIAS_BLOCK]
            if diag and b > a:
                blk = jnp.full_like(blk, NEG)
            elif diag and b == a:
                blk = blk + d_t
            elif (diag and b == a - 1) or (not diag and b == n - 1 and a == 0):
                blk = blk + p_t
            blocks.append(blk)
        rows.append(jnp.concatenate(blocks, axis=1))
    return jnp.concatenate(rows, axis=0)


def _attn_specs(t, seq, nq, cb, vrow):
    w = GROUP_WIDTH
    return [pl.BlockSpec((t, w), lambda b, i: (b * nq + i, cb)),
            pl.BlockSpec((seq, w), lambda b, i: (b, cb + 1)),
            pl.BlockSpec((V_SLAB, seq), lambda b, i: (vrow, b))]


def _attn_call(kernel_fn, name, batch, seq, t, in_specs, args, scratch=()):
    nq = seq // t
    return pl.pallas_call(
        kernel_fn,
        grid=(batch, nq),
        in_specs=in_specs,
        out_specs=pl.BlockSpec((t, GROUP_WIDTH), lambda b, i: (b * nq + i, 0)),
        out_shape=jax.ShapeDtypeStruct((batch * seq, GROUP_WIDTH), BF16),
        scratch_shapes=[pltpu.VMEM((SCORE_SLOTS, t, t), F32)] + list(scratch),
        compiler_params=_params("arbitrary", "arbitrary"),
        name=name,
    )(*args)


def _head_queries(q_ref, width):
    out = []
    for lo in range(0, GROUP_WIDTH, width):
        p = lo // LANES
        out.append(_lane_select(q_ref[:, p * LANES:(p + 1) * LANES], lo - p * LANES, lo - p * LANES + width))
    return out


def _fox_kernel(q_ref, k_ref, vt_ref, c_ref, o_ref, s_ref, *, t):
    qi = pl.program_id(1)
    causal = _causal_t(t)
    qms = _head_queries(q_ref, HEAD_DIM)

    def raw(h, j):
        sl = slice((h // 2) * LANES, (h // 2 + 1) * LANES)
        start = pl.multiple_of(j * t, t)
        s_t = lax.dot_general(k_ref[pl.ds(start, t), sl], qms[h], _NT, preferred_element_type=F32)
        return s_t - c_ref[pl.ds(start, t), h:h + 1]

    def fix(h, s_t, kind):
        return jnp.where(causal, s_t, NEG)

    def values(h, j):
        return vt_ref[(h // 2) * V_ROWS:(h // 2 + 1) * V_ROWS, pl.ds(pl.multiple_of(j * t, t), t)]

    carries = _sweep(GROUP_HEADS, t, qi, raw, fix, values, False, s_ref)
    outs = [_normalized(acc) for (_, acc) in carries]
    for p in range(GROUP_HEADS // 2):
        o_t = _merge_pair(outs[2 * p], outs[2 * p + 1])
        o_ref[:, p * LANES:(p + 1) * LANES] = o_t.T.astype(o_ref.dtype)


def _fox_attention(h_bf, v_t, cum_cols, batch, seq, cb, vrow, t=ATTN_TILE):
    nq = seq // t
    specs = _attn_specs(t, seq, nq, cb, vrow) + [pl.BlockSpec((seq, 8), lambda b, i: (b, 0))]
    return _attn_call(functools.partial(_fox_kernel, t=t), "fox_attn", batch, seq, t, specs,
                      (h_bf, h_bf, v_t, cum_cols))


def _diff_kernel(q_ref, k_ref, vt_ref, bd_ref, bp_ref, lam_ref, lc_ref, g_ref, o_ref, s_ref, *, t):
    qi = pl.program_id(1)
    lam_p = lam_ref[...]
    lam = (jnp.exp(jnp.sum(lam_p[0:1] * lam_p[1:2], axis=-1, keepdims=True))
           - jnp.exp(jnp.sum(lam_p[2:3] * lam_p[3:4], axis=-1, keepdims=True)) + lc_ref[0:1, 0:1])
    qms = _head_queries(q_ref, DIFF_QK_DIM)

    def raw(c, j):
        sl = slice((c // 4) * LANES, (c // 4 + 1) * LANES)
        return lax.dot_general(k_ref[pl.ds(pl.multiple_of(j * t, t), t), sl], qms[c], _NT,
                               preferred_element_type=F32)

    def fix(c, s_t, kind):
        return _band_bias(s_t, bd_ref[c // 2], bp_ref[c // 2], kind == "diag")

    def values(c, j):
        return vt_ref[(c // 4) * V_ROWS:(c // 4 + 1) * V_ROWS, pl.ds(pl.multiple_of(j * t, t), t)]

    carries = _sweep(2 * GROUP_HEADS, t, qi, raw, fix, values, True, s_ref)
    row = lax.broadcasted_iota(jnp.int32, (LANES, t), 0)
    outs = []
    for h in range(GROUP_HEADS):
        o = _normalized(carries[2 * h][1]) - lam * _normalized(carries[2 * h + 1][1])
        lo = (h % 2) * HEAD_DIM
        o = jnp.where((row >= lo) & (row < lo + HEAD_DIM), o, 0.0)
        ms = jnp.sum(o * o, axis=0, keepdims=True) * (1.0 / HEAD_DIM)
        outs.append(o * lax.rsqrt(ms + DIFF_SUBLN_EPS))
    for p in range(GROUP_HEADS // 2):
        sl = slice(p * LANES, (p + 1) * LANES)
        o_t = _merge_pair(outs[2 * p], outs[2 * p + 1])
        o_ref[:, sl] = (o_t.T * g_ref[:, sl] * lc_ref[1:2, 0:1]).astype(o_ref.dtype)


def _diff_attention(h_bf, v_t, bias_d, bias_p, lam_p, lam_c, gain, batch, seq, cb, vrow, t=ATTN_TILE):
    nq = seq // t
    bias_spec = pl.BlockSpec((GROUP_HEADS, BIAS_BLOCK, BIAS_BLOCK), lambda b, i: (0, 0, 0))
    full = lambda a: pl.BlockSpec(a.shape, lambda b, i: (0, 0))
    specs = _attn_specs(t, seq, nq, cb, vrow) + [bias_spec, bias_spec, full(lam_p), full(lam_c), full(gain)]
    return _attn_call(functools.partial(_diff_kernel, t=t), "diff_attn", batch, seq, t, specs,
                      (h_bf, h_bf, v_t, bias_d, bias_p, lam_p, lam_c, gain))


def _moba_kernel(q_ref, k_ref, vt_ref, bd_ref, bp_ref, o_ref, s_ref, kmean_ref, drop_ref, *, t, n_blocks):
    qi = pl.program_id(1)
    per_tile = t // MOBA_BLOCK

    @pl.when(qi == 0)
    def _():
        kmean_ref[...] = jnp.zeros_like(kmean_ref)
        for blk in range(n_blocks):
            kb = k_ref[blk * MOBA_BLOCK:(blk + 1) * MOBA_BLOCK, :].astype(F32)
            kmean_ref[blk:blk + 1, :] = jnp.sum(kb, axis=0, keepdims=True) * (1.0 / MOBA_BLOCK)

    def tile_block(pos):
        return sum((pos >= n * MOBA_BLOCK).astype(jnp.int32) for n in range(1, per_tile))

    nb = -(-n_blocks // 8) * 8
    blk_t = lax.broadcasted_iota(jnp.int32, (nb, t), 0)
    blk_tf = blk_t.astype(F32)
    own_t = qi * per_tile + tile_block(lax.broadcasted_iota(jnp.int32, (nb, t), 1))
    qms = _head_queries(q_ref, HEAD_DIM)
    for h in range(GROUP_HEADS):
        km = kmean_ref[:, (h // 2) * LANES:(h // 2 + 1) * LANES]
        km_hi = km.astype(BF16)
        km_lo = (km - km_hi.astype(F32)).astype(BF16)
        gate = (lax.dot_general(km_hi, qms[h], _NT, preferred_element_type=F32)
                + lax.dot_general(km_lo, qms[h], _NT, preferred_element_type=F32))[:nb]
        g = jnp.where(blk_t < own_t, gate, NEG)
        keep = jnp.where(blk_t == own_t, 1.0, 0.0)
        for _ in range(MOBA_TOPK):
            mx = jnp.max(g, axis=0, keepdims=True)
            first = jnp.min(jnp.where(g == mx, blk_tf, 1e9), axis=0, keepdims=True)
            pick = jnp.where(mx > 0.5 * NEG, jnp.where(blk_tf == first, 1.0, 0.0), 0.0)
            keep = jnp.maximum(keep, pick)
            g = jnp.where(pick > 0.0, NEG, g)
        drop_ref[h] = jnp.where(keep > 0.0, 0.0, NEG)

    key_half = tile_block(lax.broadcasted_iota(jnp.int32, (t, 1), 0))

    def raw(h, j):
        sl = slice((h // 2) * LANES, (h // 2 + 1) * LANES)
        s_t = lax.dot_general(k_ref[pl.ds(pl.multiple_of(j * t, t), t), sl], qms[h], _NT,
                              preferred_element_type=F32)
        reject = drop_ref[h, pl.ds(j * per_tile, 1), :]
        for n in range(1, per_tile):
            reject = jnp.where(key_half >= n, drop_ref[h, pl.ds(j * per_tile + n, 1), :], reject)
        return s_t + reject

    def fix(h, s_t, kind):
        return _band_bias(s_t, bd_ref[h], bp_ref[h], kind == "diag")

    def values(h, j):
        return vt_ref[(h // 2) * V_ROWS:(h // 2 + 1) * V_ROWS, pl.ds(pl.multiple_of(j * t, t), t)]

    carries = _sweep(GROUP_HEADS, t, qi, raw, fix, values, True, s_ref)
    outs = [_normalized(acc) for (_, acc) in carries]
    for p in range(GROUP_HEADS // 2):
        o_t = _merge_pair(outs[2 * p], outs[2 * p + 1])
        o_ref[:, p * LANES:(p + 1) * LANES] = o_t.T.astype(o_ref.dtype)


def _moba_attention(h_bf, v_t, bias_d, bias_p, batch, seq, cb, vrow, t=ATTN_TILE):
    nq = seq // t
    bias_spec = pl.BlockSpec((GROUP_HEADS, BIAS_BLOCK, BIAS_BLOCK), lambda b, i: (1, 0, 0))
    specs = _attn_specs(t, seq, nq, cb, vrow) + [bias_spec, bias_spec]
    return _attn_call(functools.partial(_moba_kernel, t=t, n_blocks=seq // MOBA_BLOCK), "moba_attn",
                      batch, seq, t, specs, (h_bf, h_bf, v_t, bias_d, bias_p),
                      scratch=[pltpu.VMEM((LANES, GROUP_WIDTH), F32),
                               pltpu.VMEM((GROUP_HEADS, -(-(seq // MOBA_BLOCK) // 8) * 8, t), F32)])


def _in_proj_kernel(x_ref, wlat_ref, wqk_ref, wvt3_ref, ones3_ref, wgt_ref,
                    gq_ref, gkv_ref, wqa_ref, wqb_ref, wk_ref, wvt_ref, ones_ref, e_ref, tq_ref, tk_ref,
                    h_ref, vt3_ref, gate_ref, q_ref, k_ref, vt_ref):
    x = x_ref[...]
    c = jnp.dot(x, wlat_ref[...], preferred_element_type=F32)
    h_ref[...] = jnp.dot(x, wqk_ref[...], preferred_element_type=F32).astype(h_ref.dtype)
    v3 = lax.dot_general(wvt3_ref[...], x, _TN, preferred_element_type=F32) + ones3_ref[...]
    vt3_ref[...] = v3.astype(vt3_ref.dtype)
    gate_ref[...] = lax.dot_general(wgt_ref[...], x, _TN, preferred_element_type=F32)
    cq = c[:, :MLA_Q_LORA]
    ckv = c[:, MLA_Q_LORA:MLA_Q_LORA + MLA_KV_LORA]
    kr = c[:, MLA_Q_LORA + MLA_KV_LORA:]
    cq = cq * lax.rsqrt(jnp.mean(cq * cq, axis=-1, keepdims=True) + RMS_EPS) * gq_ref[...]
    ckv = ckv * lax.rsqrt(jnp.mean(ckv * ckv, axis=-1, keepdims=True) + RMS_EPS) * gkv_ref[...]
    cq = cq.astype(BF16)
    ckv = ckv.astype(BF16)
    tq = tq_ref[...]
    cos_q = jnp.concatenate([tq[:, :LANES]] * GROUP_HEADS, axis=1)
    sin_q = jnp.concatenate([tq[:, LANES:]] * GROUP_HEADS, axis=1)
    q = (jnp.dot(cq, wqa_ref[...], preferred_element_type=F32) * cos_q
         + jnp.dot(cq, wqb_ref[...], preferred_element_type=F32) * sin_q)
    q_ref[...] = (q * ((MLA_NOPE + MLA_ROPE) ** -0.5 * LOG2E)).astype(q_ref.dtype)
    k_rope = (kr * tk_ref[...]).astype(BF16)
    k = (jnp.dot(ckv, wk_ref[...], preferred_element_type=F32)
         + jnp.dot(k_rope, e_ref[...], preferred_element_type=F32))
    k_ref[...] = k.astype(k_ref.dtype)
    v_t = lax.dot_general(wvt_ref[...], ckv, _TN, preferred_element_type=F32) + ones_ref[...]
    vt_ref[...] = v_t.astype(vt_ref.dtype)


def _resident(a, layer):
    if a.ndim == 2:
        return pl.BlockSpec(a.shape, lambda *_: (0, 0), pipeline_mode=pl.Buffered(1))
    return pl.BlockSpec((None,) + a.shape[1:], lambda *_: (layer, 0, 0), pipeline_mode=pl.Buffered(1))


def _in_proj(layer, xb, w_lat, w_qk, w_vt3, ones3, w_gt, gq, gkv, wqa, wqb, wk, wvt, ones, e_mat, tab_q, tab_k,
             seq, tm=512):
    m, d = xb.shape
    ns = seq // tm
    rows = lambda w: pl.BlockSpec((tm, w), lambda i: (i, 0))
    cols = lambda r: pl.BlockSpec((r, tm), lambda i: (0, i))
    slot_w = GROUP_HEADS * LANES
    consts = (w_lat, w_qk, w_vt3, ones3, w_gt, gq, gkv, wqa, wqb, wk, wvt, ones, e_mat)
    return pl.pallas_call(
        _in_proj_kernel,
        grid=(m // tm,),
        in_specs=[rows(d)] + [_resident(a, layer) for a in consts]
                 + [pl.BlockSpec((tm, tab_q.shape[1]), lambda i: (i % ns, 0)),
                    pl.BlockSpec((tm, tab_k.shape[1]), lambda i: (i % ns, 0))],
        out_specs=(rows(w_qk.shape[-1]), cols(w_vt3.shape[-1]), cols(w_gt.shape[-1]),
                   rows(slot_w), rows(slot_w), cols(V_SLAB)),
        out_shape=(jax.ShapeDtypeStruct((m, w_qk.shape[-1]), BF16),
                   jax.ShapeDtypeStruct((w_vt3.shape[-1], m), BF16),
                   jax.ShapeDtypeStruct((w_gt.shape[-1], m), F32),
                   jax.ShapeDtypeStruct((m, slot_w), BF16),
                   jax.ShapeDtypeStruct((m, slot_w), BF16),
                   jax.ShapeDtypeStruct((V_SLAB, m), BF16)),
        compiler_params=_params("arbitrary"),
        name="in_proj",
    )(xb, *consts, tab_q, tab_k)


def _mla_kernel(q_ref, k_ref, vt_ref, o_ref, s_ref, *, t):
    qi = pl.program_id(1)
    causal = _causal_t(t)

    def raw(h, j):
        hsl = slice(h * LANES, (h + 1) * LANES)
        return lax.dot_general(k_ref[pl.ds(pl.multiple_of(j * t, t), t), hsl], q_ref[:, hsl], _NT,
                               preferred_element_type=F32)

    def fix(h, s_t, kind):
        return jnp.where(causal, s_t, NEG)

    def values(h, j):
        return vt_ref[(h // 2) * V_ROWS:(h // 2 + 1) * V_ROWS, pl.ds(pl.multiple_of(j * t, t), t)]

    carries = _sweep(GROUP_HEADS, t, qi, raw, fix, values, False, s_ref)
    outs = [_normalized(acc) for (_, acc) in carries]
    for p in range(GROUP_HEADS // 2):
        o_t = _merge_pair(outs[2 * p], outs[2 * p + 1])
        o_ref[:, p * LANES:(p + 1) * LANES] = o_t.T.astype(o_ref.dtype)


def _mla_attention(q, k, v_t, batch, seq, t=ATTN_TILE):
    nq = seq // t
    slot_w = q.shape[1]
    specs = [pl.BlockSpec((t, slot_w), lambda b, i: (b * nq + i, 0)),
             pl.BlockSpec((seq, slot_w), lambda b, i: (b, 0)),
             pl.BlockSpec((V_SLAB, seq), lambda b, i: (0, b))]
    return _attn_call(functools.partial(_mla_kernel, t=t), "mla_attn", batch, seq, t, specs, (q, k, v_t))


def _layernorm(t, g, b):
    mu = jnp.mean(t, axis=-1, keepdims=True)
    d = t - mu
    var = jnp.mean(d * d, axis=-1, keepdims=True)
    return d * lax.rsqrt(var + LN_EPS) * g + b


def _out_kernel(o1_ref, o2_ref, o3_ref, o4_ref, w_ref, x_ref, g_ref, b_ref, of_ref, ob_ref, *, alpha):
    sub = OUT_SUBTILE
    n_sub = x_ref.shape[0] // sub

    def project(r):
        rows = slice(r * sub, (r + 1) * sub)
        o = jnp.concatenate([o_ref[rows, :] for o_ref in (o1_ref, o2_ref, o3_ref, o4_ref)], axis=1)
        return jnp.dot(o, w_ref[...], preferred_element_type=F32)

    mix_next = project(0)
    for r in range(n_sub):
        mix = mix_next
        if r + 1 < n_sub:
            mix_next = project(r + 1)
        rows = slice(r * sub, (r + 1) * sub)
        y = _layernorm(alpha * x_ref[rows, :] + mix, g_ref[...], b_ref[...])
        of_ref[rows, :] = y
        ob_ref[rows, :] = y.astype(ob_ref.dtype)


def _out_proj(layer, outs, w_o, x, g, b, alpha, tm=512):
    m, d = x.shape
    o_spec = pl.BlockSpec((tm, GROUP_WIDTH), lambda i: (i, 0))
    row_spec = pl.BlockSpec((tm, d), lambda i: (i, 0))
    vec_spec = pl.BlockSpec((1, d), lambda i: (0, 0))
    return pl.pallas_call(
        functools.partial(_out_kernel, alpha=alpha),
        grid=(m // tm,),
        in_specs=[o_spec, o_spec, o_spec, o_spec,
                  _resident(w_o, layer), row_spec, vec_spec, vec_spec],
        out_specs=(row_spec, row_spec),
        out_shape=(jax.ShapeDtypeStruct((m, d), F32), jax.ShapeDtypeStruct((m, d), BF16)),
        compiler_params=_params("arbitrary"),
        name="out_proj_ln",
    )(*outs, w_o, x, g, b)


def _ffn_kernel(xb_ref, xh_ref, xf_ref, wup_ref, cp_ref, wd_ref, g_ref, b_ref,
                of_ref, ob_ref, *, alpha, tiles_per_seq, n_chunks):
    i = pl.program_id(0)
    halo = xh_ref[...]
    halo = jnp.where(i % tiles_per_seq == 0, jnp.zeros_like(halo), halo)
    xe = jnp.concatenate([halo, xb_ref[...]], axis=0)

    def cols(c, br):
        return slice(br * D_FF_PAD + c * FF_CHUNK, br * D_FF_PAD + (c + 1) * FF_CHUNK)

    def project(c):
        return [jnp.dot(xe, wup_ref[:, cols(c, br)], preferred_element_type=F32) for br in range(2)]

    def gated(c, us):
        ys = []
        for br, u in enumerate(us):
            cp = cp_ref[:, cols(c, br)]
            y = cp[2:3] * u + cp[1:2] * pltpu.roll(u, 1, 0) + cp[0:1] * pltpu.roll(u, 2, 0) + cp[3:4]
            ys.append(y[CONV_HALO:])
        return (ys[0] * jax.nn.sigmoid(ys[0]) * ys[1]).astype(BF16)

    acc = None
    ahead = [project(c) for c in range(FFN_LOOKAHEAD)]
    for c in range(n_chunks):
        if c + FFN_LOOKAHEAD < n_chunks:
            ahead.append(project(c + FFN_LOOKAHEAD))
        act = gated(c, ahead.pop(0))
        part = jnp.dot(act, wd_ref[c * FF_CHUNK:(c + 1) * FF_CHUNK, :], preferred_element_type=F32)
        acc = part if acc is None else acc + part
    y = _layernorm(alpha * xf_ref[...] + acc, g_ref[...], b_ref[...])
    of_ref[...] = y
    ob_ref[...] = y.astype(ob_ref.dtype)


def _ffn(layer, xb, xf, w_up, conv_p, w_down, g, b, alpha, seq, tm=256):
    m, d = xf.shape
    n_chunks = D_FF_PAD // FF_CHUNK
    halo_blocks = tm // CONV_HALO
    row_spec = pl.BlockSpec((tm, d), lambda i: (i, 0))
    vec_spec = pl.BlockSpec((1, d), lambda i: (0, 0))
    resident = lambda a: _resident(a, layer)
    return pl.pallas_call(
        functools.partial(_ffn_kernel, alpha=alpha, tiles_per_seq=seq // tm, n_chunks=n_chunks),
        grid=(m // tm,),
        in_specs=[row_spec,
                  pl.BlockSpec((CONV_HALO, d), lambda i: (jnp.maximum(i * halo_blocks - 1, 0), 0)),
                  row_spec, resident(w_up), resident(conv_p), resident(w_down), vec_spec, vec_spec],
        out_specs=(row_spec, row_spec),
        out_shape=(jax.ShapeDtypeStruct((m, d), F32), jax.ShapeDtypeStruct((m, d), BF16)),
        compiler_params=_params("arbitrary"),
        name="conv_ffn_ln",
    )(xb, xb, xf, w_up, conv_p, w_down, g, b)


def _swap_halves(w):
    half = w.shape[-1] // 2
    return jnp.concatenate([-w[..., half:], w[..., :half]], axis=-1)


def _value_slab(w_v):
    depth, k, _ = w_v.shape
    w = w_v.reshape(depth, k, GROUP_HEADS // 2, LANES)
    w = jnp.pad(w, ((0, 0), (0, 0), (0, 0), (0, V_ROWS - LANES)))
    return w.reshape(depth, k, V_SLAB)


def _ones_rows(n_slabs):
    row = jnp.arange(n_slabs * V_SLAB) % V_ROWS
    return (row == LANES).astype(F32)[:, None]


def _prep_weights(w_in, mla_w_uq, mla_w_ukv, w_up, conv_w, conv_b, w_down):
    depth, d, _ = w_in.shape
    gw = GROUP_WIDTH
    fox0 = 0
    diff0 = 3 * gw + GROUP_HEADS
    moba0 = diff0 + 3 * gw
    mla0 = moba0 + 3 * gw
    sm = HEAD_DIM ** -0.5 * LOG2E
    sd = DIFF_QK_DIM ** -0.5 * LOG2E
    w_qk = jnp.concatenate([
        w_in[:, :, fox0:fox0 + gw] * sm, w_in[:, :, fox0 + gw:fox0 + 2 * gw],
        w_in[:, :, diff0:diff0 + gw] * sd, w_in[:, :, diff0 + gw:diff0 + 2 * gw],
        w_in[:, :, moba0:moba0 + gw] * sm, w_in[:, :, moba0 + gw:moba0 + 2 * gw]], axis=-1).astype(BF16)
    w_vt = jnp.concatenate([_value_slab(w_in[:, :, g0 + 2 * gw:g0 + 3 * gw])
                            for g0 in (fox0, diff0, moba0)], axis=2).astype(BF16)
    w_gt = jnp.pad(w_in[:, :, 3 * gw:3 * gw + GROUP_HEADS],
                   ((0, 0), (0, 0), (0, 8 - GROUP_HEADS))).astype(BF16)
    kr0 = mla0 + MLA_Q_LORA + MLA_KV_LORA
    w_kr = w_in[:, :, kr0:kr0 + MLA_ROPE]
    w_lat = jnp.concatenate([w_in[:, :, mla0:kr0], w_kr, _swap_halves(w_kr),
                             jnp.zeros((depth, d, LANES - 2 * MLA_ROPE), F32)], axis=-1).astype(BF16)

    uq = mla_w_uq.reshape(depth, MLA_Q_LORA, GROUP_HEADS, MLA_NOPE + MLA_ROPE)
    pad = LANES - MLA_NOPE - MLA_ROPE
    wqa = jnp.pad(uq, ((0, 0), (0, 0), (0, 0), (0, pad)))
    wqb = jnp.concatenate([jnp.zeros_like(uq[..., :MLA_NOPE]), _swap_halves(uq[..., MLA_NOPE:]),
                           jnp.zeros(uq.shape[:-1] + (pad,), F32)], axis=-1)
    ukv = mla_w_ukv.reshape(depth, MLA_KV_LORA, GROUP_HEADS, MLA_NOPE + HEAD_DIM)
    wk = jnp.pad(ukv[..., :MLA_NOPE], ((0, 0), (0, 0), (0, 0), (0, LANES - MLA_NOPE)))
    flat = lambda a: a.reshape(depth, a.shape[1], -1).astype(BF16)
    wvt = _value_slab(flat(ukv[..., MLA_NOPE:])).astype(BF16)

    ffp = D_FF_PAD - D_FF
    padc = lambda a: jnp.pad(a, ((0, 0), (0, 0), (0, ffp)))
    w_up_p = jnp.concatenate([padc(w_up[:, :, :D_FF]), padc(w_up[:, :, D_FF:])], axis=-1).astype(BF16)
    conv = jnp.concatenate([conv_w, conv_b[:, None, :],
                            jnp.zeros((depth, 8 - CONV_WIDTH - 1, 2 * D_FF), F32)], axis=1)
    conv_p = jnp.concatenate([padc(conv[:, :, :D_FF]), padc(conv[:, :, D_FF:])], axis=-1)
    w_down_p = jnp.pad(w_down, ((0, 0), (0, ffp), (0, 0))).astype(BF16)
    return w_qk, w_vt, w_gt, w_lat, flat(wqa), flat(wqb), flat(wk), wvt, w_up_p, conv_p, w_down_p


def _rope_constants(seq):
    inv = ROPE_THETA ** (-jnp.arange(0, MLA_ROPE // 2, dtype=F32) * 2.0 / MLA_ROPE)
    ang = jnp.arange(seq, dtype=F32)[:, None] * inv[None, :]
    cos, sin = jnp.cos(ang), jnp.sin(ang)
    cos2 = jnp.concatenate([cos, cos], axis=1)
    sin2 = jnp.concatenate([sin, sin], axis=1)
    pad = jnp.zeros((seq, LANES - MLA_NOPE - MLA_ROPE), F32)
    tab_q = jnp.concatenate([jnp.ones((seq, MLA_NOPE), F32), cos2, pad,
                             jnp.zeros((seq, MLA_NOPE), F32), sin2, pad], axis=1)
    tab_k = jnp.concatenate([cos2, sin2, jnp.zeros((seq, LANES - 2 * MLA_ROPE), F32)], axis=1)
    r = jnp.arange(LANES)[:, None]
    c = jnp.arange(GROUP_HEADS * LANES)[None, :]
    e_mat = ((r < 2 * MLA_ROPE) & (c % LANES == MLA_NOPE + r % MLA_ROPE)).astype(BF16)
    return tab_q, tab_k, e_mat


def kernel(x, w_in, b_forget, diff_lambda, diff_subln, mla_q_norm, mla_kv_norm, mla_w_uq, mla_w_ukv,
           rel_bias, w_o, ln1_g, ln1_b, w_up, conv_w, conv_b, w_down, ln2_g, ln2_b):
    batch, seq, d = x.shape
    depth = w_in.shape[0]
    alpha = (2 * depth) ** 0.25
    assert seq % ATTN_TILE == 0 and ATTN_TILE % MOBA_BLOCK == 0 and seq // MOBA_BLOCK <= LANES

    (w_qk, w_vt, w_gt, w_lat, wqa, wqb, wk, wvt, w_up_p, conv_p, w_down_p) = _prep_weights(
        w_in, mla_w_uq, mla_w_ukv, w_up, conv_w, conv_b, w_down)
    w_o_b = w_o.astype(BF16)
    tab_q, tab_k, e_mat = _rope_constants(seq)
    bias_d, bias_p = _bias_blocks(rel_bias, BIAS_BLOCK)
    b_col = jnp.pad(b_forget, ((0, 0), (0, 8 - GROUP_HEADS)))[:, :, None]
    gain = jnp.tile(diff_subln, (1, GROUP_HEADS))[:, None, :]

    xf = x.reshape(batch * seq, d)
    xb = xf.astype(BF16)
    for l in range(depth):
        h_bf, v_t, gate_t, q_m, k_m, v_m = _in_proj(
            l, xb, w_lat, w_qk, w_vt, _ones_rows(3), w_gt, mla_q_norm[l][None], mla_kv_norm[l][None],
            wqa, wqb, wk, wvt, _ones_rows(1), e_mat, tab_q, tab_k, seq)
        cum = _fox_cumlog(gate_t, b_col[l], batch, seq)
        fox_o = _fox_attention(h_bf, v_t, cum.T, batch, seq, 0, 0)
        lam_init = 0.8 - 0.6 * math.exp(-0.3 * l)
        lam_c = jnp.concatenate([jnp.full((1, LANES), lam_init, F32),
                                 jnp.full((1, LANES), 1.0 - lam_init, F32),
                                 jnp.zeros((6, LANES), F32)], axis=0)
        diff_o = _diff_attention(h_bf, v_t, bias_d, bias_p, diff_lambda[l], lam_c, gain[l], batch, seq, 2, 1)
        moba_o = _moba_attention(h_bf, v_t, bias_d, bias_p, batch, seq, 4, 2)
        mla_o = _mla_attention(q_m, k_m, v_m, batch, seq)
        xf, xb = _out_proj(l, (fox_o, diff_o, moba_o, mla_o), w_o_b, xf, ln1_g[l][None], ln1_b[l][None], alpha)
        xf, xb = _ffn(l, xb, xf, w_up_p, conv_p, w_down_p, ln2_g[l][None], ln2_b[l][None], alpha, seq)
    return xf.reshape(batch, seq, d)
```
